```python
import math
import jax, jax.numpy as jnp
from jax import lax
import numpy as np

D_MODEL = 1024
BATCH = 32
SEQ = 256
DEPTH = 2
DEC_BATCH = 4
DEC_SEQ = 1024
PAST_LEN = 512

GRID_W = 64
N_MIXERS = 2
EPS = 1e-6
D_RNN = 1280
LRU_BLOCKS = 16
LRU_BLOCK = D_RNN // LRU_BLOCKS
CONV_W = 4
CONV_LEFT = 2
LRU_C = 8.0
N_HEADS = 8
HEAD_DIM = 64
V_DIM = 2 * HEAD_DIM
QK_W = N_HEADS * 2 * HEAD_DIM
ROPE_PAIRS = HEAD_DIM // 4
ROPE_THETA = 10000.0
Q_BLOCK = 128
N_EXPERTS = 64
N_GROUPS = 8
TOPK_GROUPS = 4
TOP_K = 8
D_EXPERT = 256
D_SHARED = 256
ROUTED_SCALE = 2.5

kernel_name = 'hybrid_rglru_diffattn_moe_diffusion_step'


def rms_norm(x, g):
    xf = x.astype(jnp.float32)
    y = xf * lax.rsqrt(jnp.mean(xf * xf, axis=-1, keepdims=True) + EPS)
    return (y * g.astype(jnp.float32)).astype(x.dtype)


def modulation(cond, w, b):
    m = jax.nn.silu(cond) @ w + b
    return m.reshape(cond.shape[0], 1, 6, D_MODEL)


def modulate_in(x, g, shift, scale):
    return rms_norm(x, g) * (1.0 + scale) + shift


def residual_out(x, y, g, gate):
    return x + gate * rms_norm(y, g)


def centred_conv(x, w, b):
    t = x.shape[1]
    xp = jnp.pad(x, ((0, 0), (CONV_LEFT, CONV_W - 1 - CONV_LEFT), (0, 0)))
    y = b
    for j in range(CONV_W):
        y = y + xp[:, j:j + t] * w[j]
    return y


def block_diag(x, w, b):
    bsz, t = x.shape[:2]
    y = jnp.einsum('bthi,hij->bthj', x.reshape(bsz, t, LRU_BLOCKS, LRU_BLOCK), w)
    return y.reshape(bsz, t, D_RNN) + b


def rglru_scan(x, gate_w, gate_b, lam, h0, reverse):
    r = jax.nn.sigmoid(block_diag(x, gate_w[0], gate_b[0]).astype(jnp.float32))
    i = jax.nn.sigmoid(block_diag(x, gate_w[1], gate_b[1]).astype(jnp.float32))
    log_a = -LRU_C * r * jax.nn.softplus(-lam.astype(jnp.float32))
    a = jnp.exp(log_a)
    u = jnp.sqrt(-jnp.expm1(2.0 * log_a)) * (i * x.astype(jnp.float32))

    def step(h, au):
        a_t, u_t = au
        h = a_t * h + u_t
        return h, h

    h_last, hs = lax.scan(step, h0.astype(jnp.float32),
                          (jnp.swapaxes(a, 0, 1), jnp.swapaxes(u, 0, 1)), reverse=reverse)
    return jnp.swapaxes(hs, 0, 1).astype(x.dtype), h_last.astype(x.dtype)


def lru_mixer(h, h0_fwd, h0_bwd, w_in, conv_w, conv_b, gate_w, gate_b, lam, w_out):
    xb = h @ w_in
    gate = jax.nn.gelu(xb[..., :D_RNN])
    xr = centred_conv(xb[..., D_RNN:], conv_w, conv_b)
    y_f, h_f = rglru_scan(xr, gate_w[0], gate_b[0], lam[0], h0_fwd, False)
    y_b, h_b = rglru_scan(xr, gate_w[1], gate_b[1], lam[1], h0_bwd, True)
    return ((y_f + y_b) * gate) @ w_out, h_f, h_b


def axial_rope_tables(n):
    rows = n // GRID_W
    row = jnp.repeat(jnp.arange(rows, dtype=jnp.float32), GRID_W)
    col = jnp.tile(jnp.arange(GRID_W, dtype=jnp.float32), rows)
    freqs = ROPE_THETA ** (-jnp.arange(ROPE_PAIRS, dtype=jnp.float32) / ROPE_PAIRS)
    ar = row[:, None] * freqs
    ac = col[:, None] * freqs
    cos = jnp.concatenate([jnp.cos(ar), jnp.cos(ar), jnp.cos(ac), jnp.cos(ac)], axis=-1)
    sin = jnp.concatenate([jnp.sin(ar), jnp.sin(ar), jnp.sin(ac), jnp.sin(ac)], axis=-1)
    return cos, sin


def apply_rope(x, cos, sin):
    xs = x.reshape(x.shape[:-1] + (4, ROPE_PAIRS))
    rot = jnp.stack([-xs[..., 1, :], xs[..., 0, :], -xs[..., 3, :], xs[..., 2, :]],
                    axis=-2).reshape(x.shape)
    c = cos[None, :, None, None, :].astype(x.dtype)
    s = sin[None, :, None, None, :].astype(x.dtype)
    return x * c + rot * s


def diff_qkv(h, w_qkv):
    bsz, t = h.shape[:2]
    qkv = h @ w_qkv
    q = qkv[..., :QK_W].reshape(bsz, t, N_HEADS, 2, HEAD_DIM)
    k = qkv[..., QK_W:2 * QK_W].reshape(bsz, t, N_HEADS, 2, HEAD_DIM)
    v = qkv[..., 2 * QK_W:].reshape(bsz, t, N_HEADS, V_DIM)
    return q, k, v


def diff_lambda(lp, lam_init):
    lp = lp.astype(jnp.float32)
    return jnp.exp(jnp.sum(lp[0] * lp[1])) - jnp.exp(jnp.sum(lp[2] * lp[3])) + lam_init


def diff_attend(q, k, v, lam):
    bsz, tq = q.shape[:2]
    nb = tq // Q_BLOCK
    qb = jnp.moveaxis(q.reshape(bsz, nb, Q_BLOCK, N_HEADS, 2, HEAD_DIM), 1, 0)
    scale = HEAD_DIM ** -0.5

    def one_block(qblk):
        s = jnp.einsum('bqhcd,bkhcd->bhcqk', qblk, k, preferred_element_type=jnp.float32) * scale
        p = jax.nn.softmax(s, axis=-1)
        w = p[:, :, 0] - lam * p[:, :, 1]
        return jnp.einsum('bhqk,bkhe->bqhe', w.astype(v.dtype), v)

    o = lax.map(one_block, qb)
    return jnp.moveaxis(o, 0, 1).reshape(bsz, tq, N_HEADS, V_DIM)


def diff_out(o, g, lam_init, w_o):
    bsz, t = o.shape[:2]
    o = rms_norm(o, g) * (1.0 - lam_init)
    return o.reshape(bsz, t, N_HEADS * V_DIM) @ w_o


def moe(x, router, router_bias, w_gate, w_up, w_down, ws_gate, ws_up, ws_down):
    scores = jax.nn.sigmoid(x.astype(jnp.float32) @ router.astype(jnp.float32))
    sel = scores + router_bias.astype(jnp.float32)
    grp = sel.reshape(sel.shape[:-1] + (N_GROUPS, N_EXPERTS // N_GROUPS))
    gscore = jnp.sum(lax.top_k(grp, 2)[0], axis=-1)
    _, gidx = lax.top_k(gscore, TOPK_GROUPS)
    gmask = jnp.any(gidx[..., :, None] == jnp.arange(N_GROUPS), axis=-2)
    emask = jnp.repeat(gmask, N_EXPERTS // N_GROUPS, axis=-1)
    _, eidx = lax.top_k(jnp.where(emask, sel, -jnp.inf), TOP_K)
    wsel = jnp.take_along_axis(scores, eidx, axis=-1)
    wsel = wsel / jnp.sum(wsel, axis=-1, keepdims=True) * ROUTED_SCALE
    combine = jnp.sum(jax.nn.one_hot(eidx, N_EXPERTS, dtype=jnp.float32) * wsel[..., None], axis=-2)
    h = jax.nn.silu(jnp.einsum('btd,edf->btef', x, w_gate)) * jnp.einsum('btd,edf->btef', x, w_up)
    routed = jnp.einsum('btef,efd->btd', h * combine[..., None].astype(x.dtype), w_down)
    shared = (jax.nn.silu(x @ ws_gate) * (x @ ws_up)) @ ws_down
    return routed + shared


def setup_inputs(seed: int = 0) -> dict:
    key = jax.random.key(seed)
    ks = jax.random.split(key, 32)
    n_lru = (DEPTH + N_MIXERS - 1) // N_MIXERS
    n_attn = DEPTH // N_MIXERS
    D = D_MODEL

    def nrm(k, shape, scale):
        return jax.random.normal(k, shape, jnp.float32) * scale

    u = jax.random.uniform(ks[15], (n_lru, 2, D_RNN), jnp.float32, 0.9, 0.999)
    return {
        'x_prompt': nrm(ks[0], (BATCH, SEQ, D), 1.0),
        'x_sample': nrm(ks[1], (DEC_BATCH, DEC_SEQ, D), 1.0),
        'c': nrm(ks[2], (DEC_BATCH, D), 1.0),
        'state_lru': nrm(ks[3], (DEC_BATCH, n_lru, 2, D_RNN), 0.5),
        'cache_k': nrm(ks[4], (DEC_BATCH, n_attn, PAST_LEN, N_HEADS, 2 * HEAD_DIM), 1.0),
        'cache_v': nrm(ks[5], (DEC_BATCH, n_attn, PAST_LEN, N_HEADS, V_DIM), 1.0),
        'c_ctx': nrm(ks[6], (D,), 1.0),
        'mod_w': nrm(ks[7], (DEPTH, D, 6 * D), 0.5 * D ** -0.5),
        'mod_b': nrm(ks[8], (DEPTH, 6 * D), 0.02),
        'norm_g': 1.0 + nrm(ks[9], (DEPTH, 4, D), 0.05),
        'lru_w_in': nrm(ks[10], (n_lru, D, 2 * D_RNN), D ** -0.5),
        'lru_conv_w': nrm(ks[11], (n_lru, CONV_W, D_RNN), CONV_W ** -0.5),
        'lru_conv_b': nrm(ks[12], (n_lru, D_RNN), 0.02),
        'lru_gate_w': nrm(ks[13], (n_lru, 2, 2, LRU_BLOCKS, LRU_BLOCK, LRU_BLOCK), LRU_BLOCK ** -0.5),
        'lru_gate_b': nrm(ks[14], (n_lru, 2, 2, D_RNN), 0.1),
        'lru_lambda': jnp.log(u) - jnp.log1p(-u),
        'lru_w_out': nrm(ks[16], (n_lru, D_RNN, D), D_RNN ** -0.5),
        'attn_w_qkv': nrm(ks[17], (n_attn, D, 3 * QK_W), D ** -0.5),
        'attn_lambda': nrm(ks[18], (n_attn, 4, HEAD_DIM), 0.1),
        'attn_subln': 1.0 + nrm(ks[19], (n_attn, V_DIM), 0.05),
        'attn_w_o': nrm(ks[20], (n_attn, N_HEADS * V_DIM, D), (N_HEADS * V_DIM) ** -0.5),
        'moe_router': nrm(ks[21], (DEPTH, D, N_EXPERTS), D ** -0.5),
        'moe_router_bias': nrm(ks[22], (DEPTH, N_EXPERTS), 0.01),
        'moe_w_gate': nrm(ks[23], (DEPTH, N_EXPERTS, D, D_EXPERT), D ** -0.5),
        'moe_w_up': nrm(ks[24], (DEPTH, N_EXPERTS, D, D_EXPERT), D ** -0.5),
        'moe_w_down': nrm(ks[25], (DEPTH, N_EXPERTS, D_EXPERT, D), D_EXPERT ** -0.5),
        'shared_w_gate': nrm(ks[26], (DEPTH, D, D_SHARED), D ** -0.5),
        'shared_w_up': nrm(ks[27], (DEPTH, D, D_SHARED), D ** -0.5),
        'shared_w_down': nrm(ks[28], (DEPTH, D_SHARED, D), D_SHARED ** -0.5),
    }


def reference(x_prompt, x_sample, c, state_lru, cache_k, cache_v, c_ctx, mod_w, mod_b, norm_g,
              lru_w_in, lru_conv_w, lru_conv_b, lru_gate_w, lru_gate_b, lru_lambda, lru_w_out,
              attn_w_qkv, attn_lambda, attn_subln, attn_w_o,
              moe_router, moe_router_bias, moe_w_gate, moe_w_up, moe_w_down,
              shared_w_gate, shared_w_up, shared_w_down):
    bp = x_prompt.shape[0]
    bs = x_sample.shape[0]
    n_lat = x_sample.shape[1]
    past = cache_k.shape[2]
    cos, sin = axial_rope_tables(n_lat)
    xp, xs = x_prompt, x_sample
    new_lru, new_k, new_v = [], [], []
    for i in range(DEPTH):
        j = i // N_MIXERS
        mc = modulation(c_ctx[None], mod_w[i], mod_b[i])
        ms = modulation(c, mod_w[i], mod_b[i])
        hp = modulate_in(xp, norm_g[i, 0], mc[:, :, 0], mc[:, :, 1])
        hs = modulate_in(xs, norm_g[i, 0], ms[:, :, 0], ms[:, :, 1])
        if i % N_MIXERS == 0:
            lru_p = (lru_w_in[j], lru_conv_w[j], lru_conv_b[j], lru_gate_w[j], lru_gate_b[j],
                     lru_lambda[j], lru_w_out[j])
            h0 = jnp.zeros((bp, D_RNN), xp.dtype)
            yp, hf, hb = lru_mixer(hp, h0, h0, *lru_p)
            new_lru.append(jnp.stack([hf, hb], axis=1))
            ys, _, _ = lru_mixer(hs, state_lru[:, j, 0], state_lru[:, j, 1], *lru_p)
        else:
            lam_init = 0.8 - 0.6 * math.exp(-0.3 * i)
            lam = diff_lambda(attn_lambda[j], lam_init)
            q, k, v = diff_qkv(hp, attn_w_qkv[j])
            yp = diff_out(diff_attend(q, k, v, lam), attn_subln[j], lam_init, attn_w_o[j])
            new_k.append(k.reshape(bp, k.shape[1], N_HEADS, 2 * HEAD_DIM))
            new_v.append(v)
            q, k, v = diff_qkv(hs, attn_w_qkv[j])
            q = apply_rope(q, cos, sin)
            k = apply_rope(k, cos, sin)
            kc = cache_k[:, j].reshape(bs, past, N_HEADS, 2, HEAD_DIM)
            k_all = jnp.concatenate([kc, k], axis=1)
            v_all = jnp.concatenate([cache_v[:, j], v], axis=1)
            ys = diff_out(diff_attend(q, k_all, v_all, lam), attn_subln[j], lam_init, attn_w_o[j])
        xp = residual_out(xp, yp, norm_g[i, 1], mc[:, :, 2])
        xs = residual_out(xs, ys, norm_g[i, 1], ms[:, :, 2])
        moe_p = (moe_router[i], moe_router_bias[i], moe_w_gate[i], moe_w_up[i], moe_w_down[i],
                 shared_w_gate[i], shared_w_up[i], shared_w_down[i])
        fp = moe(modulate_in(xp, norm_g[i, 2], mc[:, :, 3], mc[:, :, 4]), *moe_p)
        fs = moe(modulate_in(xs, norm_g[i, 2], ms[:, :, 3], ms[:, :, 4]), *moe_p)
        xp = residual_out(xp, fp, norm_g[i, 3], mc[:, :, 5])
        xs = residual_out(xs, fs, norm_g[i, 3], ms[:, :, 5])
    new_state_lru = jnp.stack(new_lru, axis=1)
    new_cache_k = jnp.stack(new_k, axis=1)
    new_cache_v = jnp.stack(new_v, axis=1)
    return (xp, xs, new_state_lru, new_cache_k, new_cache_v)
```

```python
import functools
import math

import jax
import jax.numpy as jnp
from jax import lax
from jax.experimental import pallas as pl
from jax.experimental.pallas import tpu as pltpu

D_MODEL = 1024
DEPTH = 2
N_MIXERS = 2
GRID_W = 64
EPS = 1e-6
D_RNN = 1280
LRU_BLOCKS = 16
LRU_BLOCK = D_RNN // LRU_BLOCKS
CONV_W = 4
CONV_LEFT = 2
LRU_C = 8.0
N_HEADS = 8
HEAD_DIM = 64
V_DIM = 2 * HEAD_DIM
QK_W = N_HEADS * 2 * HEAD_DIM
ROPE_PAIRS = HEAD_DIM // 4
ROPE_THETA = 10000.0
N_EXPERTS = 64
N_GROUPS = 8
GROUP_SIZE = N_EXPERTS // N_GROUPS
TOPK_GROUPS = 4
TOP_K = 8
D_EXPERT = 256
D_SHARED = 256
ROUTED_SCALE = 2.5

MOD_ROWS = 8
LRU_HALF = D_RNN // 2
CONV_PAD = 8
VMEM_LIMIT = 56 * 1024 * 1024
BF16 = jnp.bfloat16
F32 = jnp.float32
NEG_INF = float("-inf")


def _params(sem):
    return pltpu.CompilerParams(dimension_semantics=sem, vmem_limit_bytes=VMEM_LIMIT)


def _rms(x, g):
    return x * lax.rsqrt(jnp.mean(x * x, axis=-1, keepdims=True) + EPS) * g


def _mod_in(x, g, shift, scale):
    return _rms(x, g) * (1.0 + scale) + shift


def _dot(a, b):
    return jnp.dot(a, b, preferred_element_type=F32)


def _dot_nt(a, b, precision=None):
    return lax.dot_general(a, b, (((1,), (1,)), ((), ())), precision=precision,
                           preferred_element_type=F32)


def _mod_kernel(cond_ref, w_ref, b_ref, o_ref):
    cnd = cond_ref[...]
    s = cnd * jax.nn.sigmoid(cnd)
    o_ref[...] = jnp.dot(s, w_ref[...], precision=lax.Precision.HIGHEST,
                         preferred_element_type=F32) + b_ref[...]


def _modulation(cond, mod_w, mod_b):
    tn = 1536
    out = pl.pallas_call(
        _mod_kernel,
        out_shape=jax.ShapeDtypeStruct((DEPTH, MOD_ROWS, 6 * D_MODEL), F32),
        grid=(DEPTH, 6 * D_MODEL // tn),
        in_specs=[
            pl.BlockSpec((MOD_ROWS, D_MODEL), lambda l, n: (0, 0)),
            pl.BlockSpec((None, D_MODEL, tn), lambda l, n: (l, 0, n)),
            pl.BlockSpec((None, 1, tn), lambda l, n: (l, 0, n)),
        ],
        out_specs=pl.BlockSpec((None, MOD_ROWS, tn), lambda l, n: (l, 0, n)),
        compiler_params=_params(("arbitrary", "arbitrary")),
        name="modulation",
    )(cond, mod_w, mod_b.reshape(DEPTH, 1, 6 * D_MODEL))
    return out.reshape(DEPTH, MOD_ROWS, 6, D_MODEL)


def _mod_spec(tm, rows_per_seq, first_row):
    def index(i, *_):
        return (first_row + (i * tm) // rows_per_seq if rows_per_seq else first_row, 0, 0)
    return pl.BlockSpec((None, 6, D_MODEL), index)


def _lru_in_kernel(x_ref, mod_ref, g_ref, w_ref, gate_ref, xr_ref):
    h = _mod_in(x_ref[...], g_ref[...], mod_ref[0:1, :], mod_ref[1:2, :])
    xb = _dot(h.astype(BF16), w_ref[...])
    gate_ref[...] = jax.nn.gelu(xb[:, :D_RNN]).astype(BF16)
    xr_ref[...] = xb[:, D_RNN:]


def _lru_in(x, mod, g, w_in, rows_per_seq, first_row):
    n = x.shape[0]
    tm = 512
    return pl.pallas_call(
        _lru_in_kernel,
        out_shape=(jax.ShapeDtypeStruct((n, D_RNN), BF16), jax.ShapeDtypeStruct((n, D_RNN), F32)),
        grid=(n // tm,),
        in_specs=[
            pl.BlockSpec((tm, D_MODEL), lambda i: (i, 0)),
            _mod_spec(tm, rows_per_seq, first_row),
            pl.BlockSpec((1, D_MODEL), lambda i: (0, 0)),
            pl.BlockSpec((D_MODEL, 2 * D_RNN), lambda i: (0, 0)),
        ],
        out_specs=(pl.BlockSpec((tm, D_RNN), lambda i: (i, 0)),
                   pl.BlockSpec((tm, D_RNN), lambda i: (i, 0))),
        compiler_params=_params(("arbitrary",)),
        name="lru_in",
    )(x, mod, g, w_in)


def _lru_core_kernel(xr_ref, gate_ref, h0_ref, cw_ref, cb_ref, gw_ref, gb_ref, lam_ref,
                     m_ref, hl_ref, xpad, a_f, u_f, a_b, u_b, *, seq, chunk):
    c = LRU_HALF
    xpad[0:CONV_PAD, :] = jnp.zeros((CONV_PAD, c), F32)
    xpad[CONV_PAD + seq:, :] = jnp.zeros((CONV_PAD, c), F32)
    xpad[CONV_PAD:CONV_PAD + seq, :] = xr_ref[...]
    lam = lam_ref[...]
    coef = -LRU_C * jax.nn.softplus(-lam)
    for r0 in range(0, seq, chunk):
        xr = cb_ref[...]
        for j in range(CONV_W):
            off = CONV_PAD - CONV_LEFT + j + r0
            xr = xr + xpad[off:off + chunk, :] * cw_ref[j:j + 1, :]
        g = _dot(xr.astype(BF16), gw_ref[...]) + gb_ref[...]
        for d, (a_s, u_s) in enumerate(((a_f, u_f), (a_b, u_b))):
            r = jax.nn.sigmoid(g[:, 2 * d * c:(2 * d + 1) * c])
            i = jax.nn.sigmoid(g[:, (2 * d + 1) * c:(2 * d + 2) * c])
            log_a = coef[d:d + 1, :] * r
            a = jnp.exp(log_a)
            a_s[r0:r0 + chunk, :] = a
            u_s[r0:r0 + chunk, :] = jnp.sqrt(-jnp.tanh(log_a) * (a * a + 1.0)) * (i * xr)

    def step(t, carry):
        hf, hb = carry
        tb = seq - 1 - t
        hf = a_f[pl.ds(t, 1), :] * hf + u_f[pl.ds(t, 1), :]
        u_f[pl.ds(t, 1), :] = hf
        hb = a_b[pl.ds(tb, 1), :] * hb + u_b[pl.ds(tb, 1), :]
        u_b[pl.ds(tb, 1), :] = hb
        return hf, hb

    hf, hb = lax.fori_loop(0, seq, step, (h0_ref[0:1, :], h0_ref[1:2, :]), unroll=8)
    hl_ref[0:1, :] = hf
    hl_ref[1:2, :] = hb
    m_ref[...] = ((u_f[...] + u_b[...]) * gate_ref[...].astype(F32)).astype(BF16)


def _lru_core(xr, gate, h0, conv_w, conv_b, gate_w, gate_b, lam, n_seq, seq):
    c = LRU_HALF
    chunk = min(seq, 256)
    kernel = functools.partial(_lru_core_kernel, seq=seq, chunk=chunk)
    return pl.pallas_call(
        kernel,
        out_shape=(jax.ShapeDtypeStruct((n_seq * seq, D_RNN), BF16),
                   jax.ShapeDtypeStruct((n_seq, 2, D_RNN), F32)),
        grid=(n_seq, 2),
        in_specs=[
            pl.BlockSpec((seq, c), lambda b, j: (b, j)),
            pl.BlockSpec((seq, c), lambda b, j: (b, j)),
            pl.BlockSpec((None, 2, c), lambda b, j: (b, 0, j)),
            pl.BlockSpec((CONV_W, c), lambda b, j: (0, j)),
            pl.BlockSpec((1, c), lambda b, j: (0, j)),
            pl.BlockSpec((None, c, 4 * c), lambda b, j: (j, 0, 0)),
            pl.BlockSpec((None, 1, 4 * c), lambda b, j: (j, 0, 0)),
            pl.BlockSpec((2, c), lambda b, j: (0, j)),
        ],
        out_specs=(pl.BlockSpec((seq, c), lambda b, j: (b, j)),
                   pl.BlockSpec((None, 2, c), lambda b, j: (b, 0, j))),
        scratch_shapes=[pltpu.VMEM((seq + 2 * CONV_PAD, c), F32)] + [pltpu.VMEM((seq, c), F32)] * 4,
        compiler_params=_params(("arbitrary", "arbitrary")),
        name="lru_core",
    )(xr, gate, h0, conv_w, conv_b, gate_w, gate_b, lam)


def _route(sel, scores):
    tm = sel.shape[1]
    io8 = lax.broadcasted_iota(jnp.int32, (GROUP_SIZE, tm), 0)
    blocks, gscore = [], []
    for g in range(N_GROUPS):
        blk = sel[g * GROUP_SIZE:(g + 1) * GROUP_SIZE, :]
        m1 = jnp.max(blk, axis=0, keepdims=True)
        first = jnp.min(jnp.where(blk == m1, io8, GROUP_SIZE), axis=0, keepdims=True)
        m2 = jnp.max(jnp.where(io8 == first, NEG_INF, blk), axis=0, keepdims=True)
        blocks.append(blk)
        gscore.append(m1 + m2)
    masked = []
    for g in range(N_GROUPS):
        rank = jnp.zeros((1, tm), jnp.int32)
        for o in range(N_GROUPS):
            if o == g:
                continue
            beats = (gscore[o] >= gscore[g]) if o < g else (gscore[o] > gscore[g])
            rank = rank + beats.astype(jnp.int32)
        masked.append(jnp.where(rank < TOPK_GROUPS, blocks[g], NEG_INF))
    v = jnp.concatenate(masked, axis=0)
    ioe = lax.broadcasted_iota(jnp.int32, (N_EXPERTS, tm), 0)
    chosen = jnp.zeros((N_EXPERTS, tm), F32)
    for _ in range(TOP_K):
        mx = jnp.max(v, axis=0, keepdims=True)
        first = jnp.min(jnp.where(v == mx, ioe, N_EXPERTS), axis=0, keepdims=True)
        pick = ioe == first
        chosen = jnp.where(pick, 1.0, chosen)
        v = jnp.where(pick, NEG_INF, v)
    wsel = chosen * scores
    return wsel / jnp.sum(wsel, axis=0, keepdims=True) * ROUTED_SCALE


def _mix_out_kernel(m_ref, w_ref, x_ref, mod_ref, g_ref, rt_ref, rb_ref, x1_ref, h2_ref, comb_ref):
    y = _dot(m_ref[...], w_ref[...])
    x1 = x_ref[...] + mod_ref[2:3, :] * _rms(y, g_ref[1:2, :])
    x1_ref[...] = x1
    h2 = _mod_in(x1, g_ref[2:3, :], mod_ref[3:4, :], mod_ref[4:5, :])
    h2_ref[...] = h2.astype(BF16)
    logits = _dot_nt(rt_ref[...], h2, precision=lax.Precision.HIGHEST)
    scores = jax.nn.sigmoid(logits)
    comb_ref[...] = _route(scores + rb_ref[...], scores)


def _mix_out(m, w, x, mod, g, router_t, router_b, rows_per_seq, first_row):
    n, k = m.shape
    tm = 512
    return pl.pallas_call(
        _mix_out_kernel,
        out_shape=(jax.ShapeDtypeStruct((n, D_MODEL), F32),
                   jax.ShapeDtypeStruct((n, D_MODEL), BF16),
                   jax.ShapeDtypeStruct((N_EXPERTS, n), F32)),
        grid=(n // tm,),
        in_specs=[
            pl.BlockSpec((tm, k), lambda i: (i, 0)),
            pl.BlockSpec((k, D_MODEL), lambda i: (0, 0)),
            pl.BlockSpec((tm, D_MODEL), lambda i: (i, 0)),
            _mod_spec(tm, rows_per_seq, first_row),
            pl.BlockSpec((4, D_MODEL), lambda i: (0, 0)),
            pl.BlockSpec((N_EXPERTS, D_MODEL), lambda i: (0, 0)),
            pl.BlockSpec((N_EXPERTS, 1), lambda i: (0, 0)),
        ],
        out_specs=(pl.BlockSpec((tm, D_MODEL), lambda i: (i, 0)),
                   pl.BlockSpec((tm, D_MODEL), lambda i: (i, 0)),
                   pl.BlockSpec((N_EXPERTS, tm), lambda i: (0, i))),
        compiler_params=_params(("arbitrary",)),
        name="mix_out",
    )(m, w, x, mod, g, router_t, router_b)


def _swiglu(x, w_gu, width):
    gu = _dot(x, w_gu)
    return jax.nn.silu(gu[:, :width]) * gu[:, width:]


def _moe_kernel(h_ref, comb_ref, wgu_ref, wd_ref, sgu_ref, sd_ref, x_ref, mod_ref, g_ref,
                o_ref, acc_ref):
    e = pl.program_id(1)
    h = h_ref[...]

    @pl.when(e == 0)
    def _():
        acc_ref[...] = _dot(_swiglu(h, sgu_ref[...], D_SHARED).astype(BF16), sd_ref[...])

    comb = comb_ref[...]
    lane = lax.broadcasted_iota(jnp.int32, comb.shape, 1)
    ce = jnp.sum(jnp.where(lane == e, comb, 0.0), axis=1, keepdims=True)
    act = _swiglu(h, wgu_ref[...], D_EXPERT) * ce
    acc_ref[...] += _dot(act.astype(BF16), wd_ref[...])

    @pl.when(e == N_EXPERTS - 1)
    def _():
        o_ref[...] = x_ref[...] + mod_ref[5:6, :] * _rms(acc_ref[...], g_ref[3:4, :])


def _moe(h2, comb, w_gu, w_d, s_gu, s_d, x1, mod, g, rows_per_seq, first_row):
    n = h2.shape[0]
    tm = 1024
    return pl.pallas_call(
        _moe_kernel,
        out_shape=jax.ShapeDtypeStruct((n, D_MODEL), F32),
        grid=(n // tm, N_EXPERTS),
        in_specs=[
            pl.BlockSpec((tm, D_MODEL), lambda i, e: (i, 0)),
            pl.BlockSpec((tm, N_EXPERTS), lambda i, e: (i, 0)),
            pl.BlockSpec((None, D_MODEL, 2 * D_EXPERT), lambda i, e: (e, 0, 0)),
            pl.BlockSpec((None, D_EXPERT, D_MODEL), lambda i, e: (e, 0, 0)),
            pl.BlockSpec((D_MODEL, 2 * D_SHARED), lambda i, e: (0, 0)),
            pl.BlockSpec((D_SHARED, D_MODEL), lambda i, e: (0, 0)),
            pl.BlockSpec((tm, D_MODEL), lambda i, e: (i, 0)),
            _mod_spec(tm, rows_per_seq, first_row),
            pl.BlockSpec((4, D_MODEL), lambda i, e: (0, 0)),
        ],
        out_specs=pl.BlockSpec((tm, D_MODEL), lambda i, e: (i, 0)),
        scratch_shapes=[pltpu.VMEM((tm, D_MODEL), F32)],
        compiler_params=_params(("arbitrary", "arbitrary")),
        name="moe",
    )(h2, comb, w_gu, w_d, s_gu, s_d, x1, mod, g)


def _rope(x, cos, sin_up, sin_dn):
    out = []
    for h in range(N_HEADS):
        xs = x[:, h * V_DIM:(h + 1) * V_DIM]
        up = pltpu.roll(xs, V_DIM - ROPE_PAIRS, 1)
        dn = pltpu.roll(xs, ROPE_PAIRS, 1)
        out.append(xs * cos + up * sin_up + dn * sin_dn)
    return jnp.concatenate(out, axis=1)


def _qkv_kernel(x_ref, mod_ref, g_ref, w_ref, *rest, rope):
    if rope:
        cos_ref, sup_ref, sdn_ref, q_ref, k_ref, v_ref = rest
    else:
        q_ref, k_ref, v_ref = rest
    h = _mod_in(x_ref[...], g_ref[...], mod_ref[0:1, :], mod_ref[1:2, :])
    qkv = _dot(h.astype(BF16), w_ref[...])
    q, k, v = qkv[:, :QK_W], qkv[:, QK_W:2 * QK_W], qkv[:, 2 * QK_W:]
    if rope:
        q = _rope(q, cos_ref[...], sup_ref[...], sdn_ref[...])
        k = _rope(k, cos_ref[...], sup_ref[...], sdn_ref[...])
    q_ref[...] = (q * HEAD_DIM ** -0.5).astype(q_ref.dtype)
    k_ref[...] = k.astype(k_ref.dtype)
    v_ref[...] = v.astype(v_ref.dtype)


def _qkv(x, mod, g, w_qkv, rows_per_seq, first_row, rope_tables, kv_dtype):
    n = x.shape[0]
    tm = 512
    rope = rope_tables is not None
    in_specs = [
        pl.BlockSpec((tm, D_MODEL), lambda i: (i, 0)),
        _mod_spec(tm, rows_per_seq, first_row),
        pl.BlockSpec((1, D_MODEL), lambda i: (0, 0)),
        pl.BlockSpec((D_MODEL, 3 * QK_W), lambda i: (0, 0)),
    ]
    args = [x, mod, g, w_qkv]
    if rope:
        tiles_per_seq = rows_per_seq // tm
        in_specs += [pl.BlockSpec((tm, V_DIM), lambda i: (i % tiles_per_seq, 0))] * 3
        args += list(rope_tables)
    return pl.pallas_call(
        functools.partial(_qkv_kernel, rope=rope),
        out_shape=(jax.ShapeDtypeStruct((n, QK_W), BF16),
                   jax.ShapeDtypeStruct((n, QK_W), kv_dtype),
                   jax.ShapeDtypeStruct((n, N_HEADS * V_DIM), kv_dtype)),
        grid=(n // tm,),
        in_specs=in_specs,
        out_specs=(pl.BlockSpec((tm, QK_W), lambda i: (i, 0)),) * 3,
        compiler_params=_params(("arbitrary",)),
        name="attn_qkv_rope" if rope else "attn_qkv",
    )(*args)


def _softmax_parts(s):
    e = jnp.exp(s - jnp.max(s, axis=-1, keepdims=True))
    return e, 1.0 / jnp.sum(e, axis=-1, keepdims=True)


def _attn_kernel(lp_ref, sub_ref, q_ref, k_ref, v_ref, o_ref, *, lam_init):
    lp = lp_ref[...]
    lam = (jnp.exp(jnp.sum(lp[0:1, :] * lp[1:2, :], axis=1, keepdims=True))
           - jnp.exp(jnp.sum(lp[2:3, :] * lp[3:4, :], axis=1, keepdims=True)) + lam_init)
    q = q_ref[...]
    k = k_ref[...].astype(BF16)
    e1, r1 = _softmax_parts(_dot_nt(q[:, :HEAD_DIM], k[:, :HEAD_DIM]))
    e2, r2 = _softmax_parts(_dot_nt(q[:, HEAD_DIM:], k[:, HEAD_DIM:]))
    w = e1 * r1 - lam * (e2 * r2)
    o = _dot(w.astype(BF16), v_ref[...].astype(BF16))
    o_ref[...] = (_rms(o, sub_ref[...]) * (1.0 - lam_init)).astype(BF16)


def _attention(lp, subln, q, k, v, lam_init, tq):
    bsz, t = q.shape[:2]
    tk = k.shape[1]
    return pl.pallas_call(
        functools.partial(_attn_kernel, lam_init=lam_init),
        out_shape=jax.ShapeDtypeStruct((bsz, t, N_HEADS * V_DIM), BF16),
        grid=(bsz, N_HEADS, t // tq),
        in_specs=[
            pl.BlockSpec((4, HEAD_DIM), lambda b, h, i: (0, 0)),
            pl.BlockSpec((1, V_DIM), lambda b, h, i: (0, 0)),
            pl.BlockSpec((None, tq, V_DIM), lambda b, h, i: (b, i, h)),
            pl.BlockSpec((None, tk, V_DIM), lambda b, h, i: (b, 0, h)),
            pl.BlockSpec((None, tk, V_DIM), lambda b, h, i: (b, 0, h)),
        ],
        out_specs=pl.BlockSpec((None, tq, V_DIM), lambda b, h, i: (b, i, h)),
        compiler_params=_params(("arbitrary", "arbitrary", "arbitrary")),
        name="diff_attention",
    )(lp, subln, q, k, v)


def _lru_gate_layout(gate_w, gate_b):
    nb = LRU_BLOCKS // 2
    w = gate_w.reshape(2, 2, 2, nb, LRU_BLOCK, LRU_BLOCK)
    bd = jnp.einsum("dgsbio,bc->sbidgco", w, jnp.eye(nb, dtype=w.dtype))
    bd = bd.reshape(2, LRU_HALF, 4 * LRU_HALF).astype(BF16)
    b = gate_b.reshape(2, 2, 2, LRU_HALF).transpose(2, 0, 1, 3).reshape(2, 1, 4 * LRU_HALF)
    return bd, b


def _rope_tables(n):
    rows = n // GRID_W
    row = jnp.repeat(jnp.arange(rows, dtype=F32), GRID_W)
    col = jnp.tile(jnp.arange(GRID_W, dtype=F32), rows)
    freqs = ROPE_THETA ** (-jnp.arange(ROPE_PAIRS, dtype=F32) / ROPE_PAIRS)
    ar = row[:, None] * freqs
    ac = col[:, None] * freqs
    zero = jnp.zeros_like(ar)
    cos = jnp.concatenate([jnp.cos(ar), jnp.cos(ar), jnp.cos(ac), jnp.cos(ac)], axis=-1)
    sin_up = jnp.concatenate([-jnp.sin(ar), zero, -jnp.sin(ac), zero], axis=-1)
    sin_dn = jnp.concatenate([zero, jnp.sin(ar), zero, jnp.sin(ac)], axis=-1)
    return tuple(jnp.tile(t, (1, 2)) for t in (cos, sin_up, sin_dn))


def kernel(x_prompt, x_sample, c, state_lru, cache_k, cache_v, c_ctx, mod_w, mod_b, norm_g,
           lru_w_in, lru_conv_w, lru_conv_b, lru_gate_w, lru_gate_b, lru_lambda, lru_w_out,
           attn_w_qkv, attn_lambda, attn_subln, attn_w_o,
           moe_router, moe_router_bias, moe_w_gate, moe_w_up, moe_w_down,
           shared_w_gate, shared_w_up, shared_w_down):
    bp, tp = x_prompt.shape[:2]
    bs, ts = x_sample.shape[:2]
    past = cache_k.shape[2]
    cond = jnp.concatenate([c_ctx[None], c, jnp.zeros((MOD_ROWS - 1 - bs, D_MODEL), F32)], axis=0)
    mod = _modulation(cond, mod_w, mod_b)
    streams = [dict(x=x_prompt.reshape(bp * tp, D_MODEL), rps=0, row=0, b=bp, t=tp),
               dict(x=x_sample.reshape(bs * ts, D_MODEL), rps=ts, row=1, b=bs, t=ts)]
    new_lru, new_k, new_v = [], [], []
    for i in range(DEPTH):
        j = i // N_MIXERS
        g = norm_g[i]
        router_t = moe_router[i].T
        router_b = moe_router_bias[i].reshape(N_EXPERTS, 1)
        w_gu = jnp.concatenate([moe_w_gate[i], moe_w_up[i]], axis=-1).astype(BF16)
        w_d = moe_w_down[i].astype(BF16)
        s_gu = jnp.concatenate([shared_w_gate[i], shared_w_up[i]], axis=-1).astype(BF16)
        s_d = shared_w_down[i].astype(BF16)
        if i % N_MIXERS == 0:
            w_in = lru_w_in[j].astype(BF16)
            w_mix = lru_w_out[j].astype(BF16)
            gate_w, gate_b = _lru_gate_layout(lru_gate_w[j], lru_gate_b[j])
        else:
            lam_init = 0.8 - 0.6 * math.exp(-0.3 * i)
            w_qkv = attn_w_qkv[j].astype(BF16)
            w_mix = attn_w_o[j].astype(BF16)
            tables = _rope_tables(ts)
        for si, s in enumerate(streams):
            if i % N_MIXERS == 0:
                gate, xr = _lru_in(s["x"], mod[i], g[0:1], w_in, s["rps"], s["row"])
                h0 = jnp.zeros((bp, 2, D_RNN), F32) if si == 0 else state_lru[:, j]
                m, h_last = _lru_core(xr, gate, h0, lru_conv_w[j], lru_conv_b[j][None], gate_w, gate_b,
                                      lru_lambda[j], s["b"], s["t"])
                if si == 0:
                    new_lru.append(h_last)
            else:
                if si == 0:
                    q, k, v = _qkv(s["x"], mod[i], g[0:1], w_qkv, s["rps"], s["row"], None, F32)
                    new_k.append(k.reshape(bp, tp, N_HEADS, 2 * HEAD_DIM))
                    new_v.append(v.reshape(bp, tp, N_HEADS, V_DIM))
                    k3 = k.reshape(bp, tp, QK_W)
                    v3 = v.reshape(bp, tp, N_HEADS * V_DIM)
                    tq = tp
                else:
                    q, k, v = _qkv(s["x"], mod[i], g[0:1], w_qkv, s["rps"], s["row"], tables, BF16)
                    k3 = jnp.concatenate([cache_k[:, j].reshape(bs, past, QK_W).astype(BF16),
                                          k.reshape(bs, ts, QK_W)], axis=1)
                    v3 = jnp.concatenate([cache_v[:, j].reshape(bs, past, N_HEADS * V_DIM).astype(BF16),
                                          v.reshape(bs, ts, N_HEADS * V_DIM)], axis=1)
                    tq = 512
                o = _attention(attn_lambda[j], attn_subln[j][None], q.reshape(s["b"], s["t"], QK_W),
                               k3, v3, lam_init, tq)
                m = o.reshape(s["b"] * s["t"], N_HEADS * V_DIM)
            x1, h2, comb_t = _mix_out(m, w_mix, s["x"], mod[i], g, router_t, router_b, s["rps"], s["row"])
            s["x"] = _moe(h2, comb_t.T, w_gu, w_d, s_gu, s_d, x1, mod[i], g, s["rps"], s["row"])
    return (streams[0]["x"].reshape(bp, tp, D_MODEL),
            streams[1]["x"].reshape(bs, ts, D_MODEL),
            jnp.stack(new_lru, axis=1),
            jnp.stack(new_k, axis=1),
            jnp.stack(new_v, axis=1))
```

```python
import functools
import math

import jax
import jax.numpy as jnp
from jax import lax
from jax.experimental import pallas as pl
from jax.experimental.pallas import tpu as pltpu

D_MODEL = 1024
DEPTH = 2
N_MIXERS = 2
GRID_W = 64
EPS = 1e-6
D_RNN = 1280
LRU_BLOCKS = 16
LRU_BLOCK = D_RNN // LRU_BLOCKS
CONV_W = 4
CONV_LEFT = 2
LRU_C = 8.0
N_HEADS = 8
HEAD_DIM = 64
V_DIM = 2 * HEAD_DIM
QK_W = N_HEADS * 2 * HEAD_DIM
ROPE_PAIRS = HEAD_DIM // 4
ROPE_THETA = 10000.0
N_EXPERTS = 64
N_GROUPS = 8
GROUP_SIZE = N_EXPERTS // N_GROUPS
TOPK_GROUPS = 4
TOP_K = 8
D_EXPERT = 256
D_SHARED = 256
ROUTED_SCALE = 2.5

MOD_ROWS = 8
LRU_HALF = D_RNN // 2
CONV_PAD = 8
MOE_TM = 256
MOE_TSB = 1024
MOE_CHUNK_ALIGN = 8
MOE_PIECE = 32
LANES = 128
TOK_SUB = D_MODEL // LANES
VMEM_LIMIT = 56 * 1024 * 1024
BF16 = jnp.bfloat16
F32 = jnp.float32
NEG_INF = float("-inf")


def _params(sem):
    return pltpu.CompilerParams(dimension_semantics=sem, vmem_limit_bytes=VMEM_LIMIT)


def _rms(x, g):
    return x * lax.rsqrt(jnp.mean(x * x, axis=-1, keepdims=True) + EPS) * g


def _mod_in(x, g, shift, scale):
    return _rms(x, g) * (1.0 + scale) + shift


def _dot(a, b):
    return jnp.dot(a, b, preferred_element_type=F32)


def _dot_nt(a, b, precision=None):
    return lax.dot_general(a, b, (((1,), (1,)), ((), ())), precision=precision,
                           preferred_element_type=F32)


def _mod_kernel(cond_ref, w_ref, b_ref, o_ref):
    cnd = cond_ref[...]
    s = cnd * jax.nn.sigmoid(cnd)
    o_ref[...] = jnp.dot(s, w_ref[...], precision=lax.Precision.HIGHEST,
                         preferred_element_type=F32) + b_ref[...]


def _modulation(cond, mod_w, mod_b):
    tn = 1536
    out = pl.pallas_call(
        _mod_kernel,
        out_shape=jax.ShapeDtypeStruct((DEPTH, MOD_ROWS, 6 * D_MODEL), F32),
        grid=(DEPTH, 6 * D_MODEL // tn),
        in_specs=[
            pl.BlockSpec((MOD_ROWS, D_MODEL), lambda l, n: (0, 0)),
            pl.BlockSpec((None, D_MODEL, tn), lambda l, n: (l, 0, n)),
            pl.BlockSpec((None, 1, tn), lambda l, n: (l, 0, n)),
        ],
        out_specs=pl.BlockSpec((None, MOD_ROWS, tn), lambda l, n: (l, 0, n)),
        compiler_params=_params(("arbitrary", "arbitrary")),
        name="modulation",
    )(cond, mod_w, mod_b.reshape(DEPTH, 1, 6 * D_MODEL))
    return out.reshape(DEPTH, MOD_ROWS, 6, D_MODEL)


def _mod_spec(tm, rows_per_seq, first_row):
    def index(i, *_):
        return (first_row + (i * tm) // rows_per_seq if rows_per_seq else first_row, 0, 0)
    return pl.BlockSpec((None, 6, D_MODEL), index)


def _lru_in_kernel(x_ref, mod_ref, g_ref, w_ref, gate_ref, xr_ref):
    h = _mod_in(x_ref[...], g_ref[...], mod_ref[0:1, :], mod_ref[1:2, :])
    xb = _dot(h.astype(BF16), w_ref[...])
    gate_ref[...] = jax.nn.gelu(xb[:, :D_RNN]).astype(BF16)
    xr_ref[...] = xb[:, D_RNN:]


def _lru_in(x, mod, g, w_in, rows_per_seq, first_row):
    n = x.shape[0]
    tm = 512
    return pl.pallas_call(
        _lru_in_kernel,
        out_shape=(jax.ShapeDtypeStruct((n, D_RNN), BF16), jax.ShapeDtypeStruct((n, D_RNN), F32)),
        grid=(n // tm,),
        in_specs=[
            pl.BlockSpec((tm, D_MODEL), lambda i: (i, 0)),
            _mod_spec(tm, rows_per_seq, first_row),
            pl.BlockSpec((1, D_MODEL), lambda i: (0, 0)),
            pl.BlockSpec((D_MODEL, 2 * D_RNN), lambda i: (0, 0)),
        ],
        out_specs=(pl.BlockSpec((tm, D_RNN), lambda i: (i, 0)),
                   pl.BlockSpec((tm, D_RNN), lambda i: (i, 0))),
        compiler_params=_params(("arbitrary",)),
        name="lru_in",
    )(x, mod, g, w_in)


def _lru_core_kernel(xr_ref, gate_ref, h0_ref, cw_ref, cb_ref, gw_ref, gb_ref, lam_ref,
                     m_ref, hl_ref, xpad, a_f, u_f, a_b, u_b, *, seq, chunk):
    c = LRU_HALF
    xpad[0:CONV_PAD, :] = jnp.zeros((CONV_PAD, c), F32)
    xpad[CONV_PAD + seq:, :] = jnp.zeros((CONV_PAD, c), F32)
    xpad[CONV_PAD:CONV_PAD + seq, :] = xr_ref[...]
    lam = lam_ref[...]
    coef = -LRU_C * jax.nn.softplus(-lam)
    for r0 in range(0, seq, chunk):
        xr = cb_ref[...]
        for j in range(CONV_W):
            off = CONV_PAD - CONV_LEFT + j + r0
            xr = xr + xpad[off:off + chunk, :] * cw_ref[j:j + 1, :]
        g = _dot(xr.astype(BF16), gw_ref[...]) + gb_ref[...]
        for d, (a_s, u_s) in enumerate(((a_f, u_f), (a_b, u_b))):
            r = jax.nn.sigmoid(g[:, 2 * d * c:(2 * d + 1) * c])
            i = jax.nn.sigmoid(g[:, (2 * d + 1) * c:(2 * d + 2) * c])
            log_a = coef[d:d + 1, :] * r
            a = jnp.exp(log_a)
            a_s[r0:r0 + chunk, :] = a
            u_s[r0:r0 + chunk, :] = jnp.sqrt(-jnp.tanh(log_a) * (a * a + 1.0)) * (i * xr)

    def step(t, carry):
        hf, hb = carry
        tb = seq - 1 - t
        hf = a_f[pl.ds(t, 1), :] * hf + u_f[pl.ds(t, 1), :]
        u_f[pl.ds(t, 1), :] = hf
        hb = a_b[pl.ds(tb, 1), :] * hb + u_b[pl.ds(tb, 1), :]
        u_b[pl.ds(tb, 1), :] = hb
        return hf, hb

    hf, hb = lax.fori_loop(0, seq, step, (h0_ref[0:1, :], h0_ref[1:2, :]), unroll=8)
    hl_ref[0:1, :] = hf
    hl_ref[1:2, :] = hb
    m_ref[...] = ((u_f[...] + u_b[...]) * gate_ref[...].astype(F32)).astype(BF16)


def _lru_core(xr, gate, h0, conv_w, conv_b, gate_w, gate_b, lam, n_seq, seq):
    c = LRU_HALF
    chunk = min(seq, 256)
    kernel = functools.partial(_lru_core_kernel, seq=seq, chunk=chunk)
    return pl.pallas_call(
        kernel,
        out_shape=(jax.ShapeDtypeStruct((n_seq * seq, D_RNN), BF16),
                   jax.ShapeDtypeStruct((n_seq, 2, D_RNN), F32)),
        grid=(n_seq, 2),
        in_specs=[
            pl.BlockSpec((seq, c), lambda b, j: (b, j)),
            pl.BlockSpec((seq, c), lambda b, j: (b, j)),
            pl.BlockSpec((None, 2, c), lambda b, j: (b, 0, j)),
            pl.BlockSpec((CONV_W, c), lambda b, j: (0, j)),
            pl.BlockSpec((1, c), lambda b, j: (0, j)),
            pl.BlockSpec((None, c, 4 * c), lambda b, j: (j, 0, 0)),
            pl.BlockSpec((None, 1, 4 * c), lambda b, j: (j, 0, 0)),
            pl.BlockSpec((2, c), lambda b, j: (0, j)),
        ],
        out_specs=(pl.BlockSpec((seq, c), lambda b, j: (b, j)),
                   pl.BlockSpec((None, 2, c), lambda b, j: (b, 0, j))),
        scratch_shapes=[pltpu.VMEM((seq + 2 * CONV_PAD, c), F32)] + [pltpu.VMEM((seq, c), F32)] * 4,
        compiler_params=_params(("arbitrary", "arbitrary")),
        name="lru_core",
    )(xr, gate, h0, conv_w, conv_b, gate_w, gate_b, lam)


def _route(sel, scores):
    tm = sel.shape[1]
    io8 = lax.broadcasted_iota(jnp.int32, (GROUP_SIZE, tm), 0)
    blocks, gscore = [], []
    for g in range(N_GROUPS):
        blk = sel[g * GROUP_SIZE:(g + 1) * GROUP_SIZE, :]
        m1 = jnp.max(blk, axis=0, keepdims=True)
        first = jnp.min(jnp.where(blk == m1, io8, GROUP_SIZE), axis=0, keepdims=True)
        m2 = jnp.max(jnp.where(io8 == first, NEG_INF, blk), axis=0, keepdims=True)
        blocks.append(blk)
        gscore.append(m1 + m2)
    masked = []
    for g in range(N_GROUPS):
        rank = jnp.zeros((1, tm), jnp.int32)
        for o in range(N_GROUPS):
            if o == g:
                continue
            beats = (gscore[o] >= gscore[g]) if o < g else (gscore[o] > gscore[g])
            rank = rank + beats.astype(jnp.int32)
        masked.append(jnp.where(rank < TOPK_GROUPS, blocks[g], NEG_INF))
    v = jnp.concatenate(masked, axis=0)
    ioe = lax.broadcasted_iota(jnp.int32, (N_EXPERTS, tm), 0)
    chosen = jnp.zeros((N_EXPERTS, tm), F32)
    for _ in range(TOP_K):
        mx = jnp.max(v, axis=0, keepdims=True)
        first = jnp.min(jnp.where(v == mx, ioe, N_EXPERTS), axis=0, keepdims=True)
        pick = ioe == first
        chosen = jnp.where(pick, 1.0, chosen)
        v = jnp.where(pick, NEG_INF, v)
    wsel = chosen * scores
    comb = wsel / jnp.sum(wsel, axis=0, keepdims=True) * ROUTED_SCALE
    return jnp.where(chosen > 0.0, comb, -1.0)


def _rows_to_tiles(x, tmp):
    tm = x.shape[0]
    for c in range(TOK_SUB):
        tmp[pl.ds(c, tm, stride=TOK_SUB), :] = x[:, c * LANES:(c + 1) * LANES]
    return tmp[...].reshape(tm // 2, 2 * TOK_SUB, LANES).astype(BF16)


def _tiles_to_rows(tmp, tm):
    return jnp.concatenate([tmp[pl.ds(c, tm, stride=TOK_SUB), :] for c in range(TOK_SUB)], axis=1)


def _load_token(pairs_ref, t):
    pair = pairs_ref[t >> 1].astype(F32)
    return jnp.where((t & 1) == 1, pair[TOK_SUB:], pair[:TOK_SUB])


def _mix_out_kernel(m_ref, w_ref, x_ref, mod_ref, g_ref, rt_ref, rb_ref, x1_ref, h2_ref, comb_ref, tmp):
    y = _dot(m_ref[...], w_ref[...])
    x1 = x_ref[...] + mod_ref[2:3, :] * _rms(y, g_ref[1:2, :])
    x1_ref[...] = x1
    h2 = _mod_in(x1, g_ref[2:3, :], mod_ref[3:4, :], mod_ref[4:5, :])
    h2_ref[...] = _rows_to_tiles(h2, tmp)
    logits = _dot_nt(rt_ref[...], h2, precision=lax.Precision.HIGHEST)
    scores = jax.nn.sigmoid(logits)
    comb_ref[...] = _route(scores + rb_ref[...], scores)


def _mix_out(m, w, x, mod, g, router_t, router_b, rows_per_seq, first_row):
    n, k = m.shape
    tm = 512
    return pl.pallas_call(
        _mix_out_kernel,
        out_shape=(jax.ShapeDtypeStruct((n, D_MODEL), F32),
                   jax.ShapeDtypeStruct((n // 2, 2 * TOK_SUB, LANES), BF16),
                   jax.ShapeDtypeStruct((N_EXPERTS, n), F32)),
        grid=(n // tm,),
        in_specs=[
            pl.BlockSpec((tm, k), lambda i: (i, 0)),
            pl.BlockSpec((k, D_MODEL), lambda i: (0, 0)),
            pl.BlockSpec((tm, D_MODEL), lambda i: (i, 0)),
            _mod_spec(tm, rows_per_seq, first_row),
            pl.BlockSpec((4, D_MODEL), lambda i: (0, 0)),
            pl.BlockSpec((N_EXPERTS, D_MODEL), lambda i: (0, 0)),
            pl.BlockSpec((N_EXPERTS, 1), lambda i: (0, 0)),
        ],
        out_specs=(pl.BlockSpec((tm, D_MODEL), lambda i: (i, 0)),
                   pl.BlockSpec((tm // 2, 2 * TOK_SUB, LANES), lambda i: (i, 0, 0)),
                   pl.BlockSpec((N_EXPERTS, tm), lambda i: (0, i))),
        scratch_shapes=[pltpu.VMEM((tm * TOK_SUB, LANES), F32)],
        compiler_params=_params(("arbitrary",)),
        name="mix_out",
    )(m, w, x, mod, g, router_t, router_b)


def _moe_plan_kernel(comb_ref, ek_ref, rk_ref, wk_ref, cnt_ref):
    c = comb_ref[...]
    n = c.shape[1]
    chosen = c >= 0.0
    ch = chosen.astype(BF16)
    tri = (lax.broadcasted_iota(jnp.int32, (n, n), 0) <= lax.broadcasted_iota(jnp.int32, (n, n), 1))
    incl = _dot(ch, tri.astype(BF16))
    rank = incl - ch.astype(F32)
    cnt_ref[...] = jnp.broadcast_to(incl[:, n - 1:n], cnt_ref.shape)
    low = (lax.broadcasted_iota(jnp.int32, (N_EXPERTS, N_EXPERTS), 1)
           < lax.broadcasted_iota(jnp.int32, (N_EXPERTS, N_EXPERTS), 0))
    slot = _dot(low.astype(BF16), ch)
    ioe = lax.broadcasted_iota(jnp.int32, c.shape, 0).astype(F32)
    for k in range(TOP_K):
        sel = jnp.logical_and(chosen, slot == float(k))
        ek_ref[k:k + 1, :] = jnp.sum(jnp.where(sel, ioe, 0.0), axis=0, keepdims=True).astype(jnp.int32)
        rk_ref[k:k + 1, :] = jnp.sum(jnp.where(sel, rank, 0.0), axis=0, keepdims=True).astype(jnp.int32)
        wk_ref[k:k + 1, :] = jnp.sum(jnp.where(sel, c, 0.0), axis=0, keepdims=True)


def _moe_plan(comb):
    n = comb.shape[1]
    nsb = n // MOE_TSB
    out_i = jax.ShapeDtypeStruct((TOP_K, n), jnp.int32)
    pair_spec = pl.BlockSpec((TOP_K, MOE_TSB), lambda s: (0, s))
    return pl.pallas_call(
        _moe_plan_kernel,
        out_shape=(out_i, out_i, jax.ShapeDtypeStruct((TOP_K, n), F32),
                   jax.ShapeDtypeStruct((N_EXPERTS, nsb * 128), F32)),
        grid=(nsb,),
        in_specs=[pl.BlockSpec((N_EXPERTS, MOE_TSB), lambda s: (0, s))],
        out_specs=(pair_spec, pair_spec, pair_spec, pl.BlockSpec((N_EXPERTS, 128), lambda s: (0, s))),
        compiler_params=_params(("arbitrary",)),
        name="moe_plan",
    )(comb)


def _moe_num_tiles(n_tok):
    rows = n_tok * TOP_K + N_EXPERTS * (n_tok // MOE_TSB) * (MOE_CHUNK_ALIGN - 1)
    return rows // MOE_TM + N_EXPERTS + 1


def _ceil_to(x, m):
    return (x + m - 1) // m * m


def _moe_layout(ek, rk, wk, cnt):
    n = ek.shape[1]
    nsb = n // MOE_TSB
    nt = _moe_num_tiles(n)
    n_es = cnt[:, ::128].astype(jnp.int32)
    c_al = _ceil_to(n_es, MOE_CHUNK_ALIGN)
    tiles_e = _ceil_to(jnp.sum(c_al, axis=1), MOE_TM) // MOE_TM
    tile0 = jnp.cumsum(tiles_e) - tiles_e
    cstart = tile0[:, None] * MOE_TM + jnp.cumsum(c_al, axis=1) - c_al
    npiece = _ceil_to(n_es, MOE_PIECE) // MOE_PIECE
    lbase = (jnp.cumsum(npiece, axis=0) - npiece) * MOE_PIECE
    flat = ek * nsb + (jnp.arange(n, dtype=jnp.int32) // MOE_TSB)[None, :]
    dest = jnp.take(cstart.reshape(-1), flat) + rk
    ldest = jnp.take(lbase.reshape(-1), flat) + rk
    tok = jnp.broadcast_to(jnp.arange(n, dtype=jnp.int32)[None, :], dest.shape)
    row_token = jnp.zeros((nt * MOE_TM,), jnp.int32).at[dest.reshape(-1)].set(tok.reshape(-1))
    row_w = jnp.zeros((nt * MOE_TM,), F32).at[dest.reshape(-1)].set(wk.reshape(-1))
    tile_expert = jnp.sum(jnp.arange(nt, dtype=jnp.int32)[:, None] >= tile0[None, :], axis=1) - 1
    return dict(row_token=row_token, row_w=row_w.reshape(nt, 1, MOE_TM),
                tile_expert=tile_expert.astype(jnp.int32), n_used=jnp.sum(tiles_e).reshape(1).astype(jnp.int32),
                cstart=cstart.T.reshape(-1), npiece=npiece.T.reshape(-1), lbase=lbase.T.reshape(-1),
                ldest=ldest.T.reshape(-1))


def _moe_expert_kernel(texp_ref, nused_ref, ids_ref, xp_ref, rw_ref, wg_ref, wu_ref, wd_ref, ys_ref,
                       xs, wgb, wub, wdb):
    i = pl.program_id(0)

    @pl.when(i < nused_ref[0])
    def _():
        @pl.when(jnp.logical_or(i == 0, texp_ref[i] != texp_ref[jnp.maximum(i - 1, 0)]))
        def _():
            wgb[...] = wg_ref[...].astype(BF16)
            wub[...] = wu_ref[...].astype(BF16)
            wdb[...] = wd_ref[...].astype(BF16)

        base = i * MOE_TM
        for r in range(MOE_TM):
            xs[r * TOK_SUB:(r + 1) * TOK_SUB, :] = _load_token(xp_ref, ids_ref[base + r])
        lhs = _tiles_to_rows(xs, MOE_TM).astype(BF16)
        gate = _dot(lhs, wgb[...])
        up = _dot(lhs, wub[...])
        wcol = jnp.transpose(jnp.broadcast_to(rw_ref[...], (8, MOE_TM)))[:, 0:1]
        act = jax.nn.silu(gate) * up * wcol
        ys_ref[...] = _rows_to_tiles(_dot(act.astype(BF16), wdb[...]), xs)

    @pl.when(i >= nused_ref[0])
    def _():
        ys_ref[...] = jnp.zeros(ys_ref.shape, BF16)


def _moe_experts(plan, xp, w_gate, w_up, w_down, layer):
    nt = plan["tile_expert"].shape[0]
    grid_spec = pltpu.PrefetchScalarGridSpec(
        num_scalar_prefetch=3,
        grid=(nt,),
        in_specs=[
            pl.BlockSpec(memory_space=pltpu.VMEM),
            pl.BlockSpec((None, 1, MOE_TM), lambda i, te, nu, ids: (i, 0, 0)),
            pl.BlockSpec((None, None, D_MODEL, D_EXPERT), lambda i, te, nu, ids: (layer, te[i], 0, 0)),
            pl.BlockSpec((None, None, D_MODEL, D_EXPERT), lambda i, te, nu, ids: (layer, te[i], 0, 0)),
            pl.BlockSpec((None, None, D_EXPERT, D_MODEL), lambda i, te, nu, ids: (layer, te[i], 0, 0)),
        ],
        out_specs=pl.BlockSpec((MOE_TM // 2, 2 * TOK_SUB, LANES), lambda i, te, nu, ids: (i, 0, 0)),
        scratch_shapes=[pltpu.VMEM((MOE_TM * TOK_SUB, LANES), F32),
                        pltpu.VMEM((D_MODEL, D_EXPERT), BF16),
                        pltpu.VMEM((D_MODEL, D_EXPERT), BF16),
                        pltpu.VMEM((D_EXPERT, D_MODEL), BF16)],
    )
    return pl.pallas_call(
        _moe_expert_kernel,
        out_shape=jax.ShapeDtypeStruct((nt * MOE_TM // 2, 2 * TOK_SUB, LANES), BF16),
        grid_spec=grid_spec,
        compiler_params=_params(("arbitrary",)),
        name="moe_experts",
    )(plan["tile_expert"], plan["n_used"], plan["row_token"], xp, plan["row_w"], w_gate, w_up, w_down)


def _moe_combine_kernel(cstart_ref, npiece_ref, lbase_ref, ldest_ref, ys_hbm, xp_ref, sg_ref, su_ref, sd_ref,
                        x_ref, mod_ref, g_ref, o_ref, ysb, ybuf, sem, *, block0):
    blk = pl.program_id(0) + block0

    def piece_copy(src_row, dst_row):
        return pltpu.make_async_copy(ys_hbm.at[pl.ds(src_row // 2, MOE_PIECE // 2)],
                                     ysb.at[pl.ds(dst_row // 2, MOE_PIECE // 2)], sem)

    def per_expert(e, total):
        j = blk * N_EXPERTS + e
        src, dst, npc = cstart_ref[j], lbase_ref[j], npiece_ref[j]

        def per_piece(p, carry):
            piece_copy(pl.multiple_of(src + p * MOE_PIECE, MOE_CHUNK_ALIGN),
                       pl.multiple_of(dst + p * MOE_PIECE, MOE_PIECE)).start()
            return carry

        lax.fori_loop(0, npc, per_piece, 0)
        return total + npc

    total = lax.fori_loop(0, N_EXPERTS, per_expert, 0)

    ybuf[...] = xp_ref[...].astype(F32).reshape(MOE_TSB * TOK_SUB, LANES)
    lhs = _tiles_to_rows(ybuf, MOE_TSB).astype(BF16)
    act = jax.nn.silu(_dot(lhs, sg_ref[...].astype(BF16))) * _dot(lhs, su_ref[...].astype(BF16))
    o_ref[...] = _dot(act.astype(BF16), sd_ref[...].astype(BF16))

    def wait_one(p, carry):
        piece_copy(0, 0).wait()
        return carry

    lax.fori_loop(0, total, wait_one, 0)

    t0 = pl.program_id(0) * MOE_TSB

    def per_token(t, carry):
        acc = _load_token(ysb, ldest_ref[(t0 + t) * TOP_K])
        for k in range(1, TOP_K):
            acc = acc + _load_token(ysb, ldest_ref[(t0 + t) * TOP_K + k])
        ybuf[pl.ds(pl.multiple_of(t * TOK_SUB, TOK_SUB), TOK_SUB), :] = acc
        return carry

    lax.fori_loop(0, MOE_TSB, per_token, 0, unroll=4)
    routed = _tiles_to_rows(ybuf, MOE_TSB)
    o_ref[...] = x_ref[...] + mod_ref[5:6, :] * _rms(routed + o_ref[...], g_ref[3:4, :])


def _moe_combine(plan, ldest, block0, ys, xp, s_gate, s_up, s_down, layer, x1, mod, g, rows_per_seq, first_row):
    n = x1.shape[0]
    cap = TOP_K * MOE_TSB + N_EXPERTS * MOE_PIECE
    grid_spec = pltpu.PrefetchScalarGridSpec(
        num_scalar_prefetch=4,
        grid=(n // MOE_TSB,),
        in_specs=[
            pl.BlockSpec(memory_space=pl.ANY),
            pl.BlockSpec((MOE_TSB // 2, 2 * TOK_SUB, LANES), lambda s, *_: (s, 0, 0)),
            pl.BlockSpec((None, D_MODEL, D_SHARED), lambda s, *_: (layer, 0, 0)),
            pl.BlockSpec((None, D_MODEL, D_SHARED), lambda s, *_: (layer, 0, 0)),
            pl.BlockSpec((None, D_SHARED, D_MODEL), lambda s, *_: (layer, 0, 0)),
            pl.BlockSpec((MOE_TSB, D_MODEL), lambda s, *_: (s, 0)),
            _mod_spec(MOE_TSB, rows_per_seq, first_row),
            pl.BlockSpec((4, D_MODEL), lambda s, *_: (0, 0)),
        ],
        out_specs=pl.BlockSpec((MOE_TSB, D_MODEL), lambda s, *_: (s, 0)),
        scratch_shapes=[pltpu.VMEM((cap // 2, 2 * TOK_SUB, LANES), BF16),
                        pltpu.VMEM((MOE_TSB * TOK_SUB, LANES), F32),
                        pltpu.SemaphoreType.DMA],
    )
    return pl.pallas_call(
        functools.partial(_moe_combine_kernel, block0=block0),
        out_shape=jax.ShapeDtypeStruct((n, D_MODEL), F32),
        grid_spec=grid_spec,
        compiler_params=_params(("arbitrary",)),
        name="moe_combine",
    )(plan["cstart"], plan["npiece"], plan["lbase"], ldest, ys, xp, s_gate, s_up, s_down, x1, mod, g)


def _rope(x, cos, sin_up, sin_dn):
    out = []
    for h in range(N_HEADS):
        xs = x[:, h * V_DIM:(h + 1) * V_DIM]
        up = pltpu.roll(xs, V_DIM - ROPE_PAIRS, 1)
        dn = pltpu.roll(xs, ROPE_PAIRS, 1)
        out.append(xs * cos + up * sin_up + dn * sin_dn)
    return jnp.concatenate(out, axis=1)


def _qkv_kernel(x_ref, mod_ref, g_ref, w_ref, *rest, rope):
    if rope:
        cos_ref, sup_ref, sdn_ref, q_ref, k_ref, v_ref = rest
    else:
        q_ref, k_ref, v_ref = rest
    h = _mod_in(x_ref[...], g_ref[...], mod_ref[0:1, :], mod_ref[1:2, :])
    qkv = _dot(h.astype(BF16), w_ref[...])
    q, k, v = qkv[:, :QK_W], qkv[:, QK_W:2 * QK_W], qkv[:, 2 * QK_W:]
    if rope:
        q = _rope(q, cos_ref[...], sup_ref[...], sdn_ref[...])
        k = _rope(k, cos_ref[...], sup_ref[...], sdn_ref[...])
    q_ref[...] = (q * HEAD_DIM ** -0.5).astype(q_ref.dtype)
    k_ref[...] = k.astype(k_ref.dtype)
    v_ref[...] = v.astype(v_ref.dtype)


def _qkv(x, mod, g, w_qkv, rows_per_seq, first_row, rope_tables, kv_dtype):
    n = x.shape[0]
    tm = 512
    rope = rope_tables is not None
    in_specs = [
        pl.BlockSpec((tm, D_MODEL), lambda i: (i, 0)),
        _mod_spec(tm, rows_per_seq, first_row),
        pl.BlockSpec((1, D_MODEL), lambda i: (0, 0)),
        pl.BlockSpec((D_MODEL, 3 * QK_W), lambda i: (0, 0)),
    ]
    args = [x, mod, g, w_qkv]
    if rope:
        tiles_per_seq = rows_per_seq // tm
        in_specs += [pl.BlockSpec((tm, V_DIM), lambda i: (i % tiles_per_seq, 0))] * 3
        args += list(rope_tables)
    return pl.pallas_call(
        functools.partial(_qkv_kernel, rope=rope),
        out_shape=(jax.ShapeDtypeStruct((n, QK_W), BF16),
                   jax.ShapeDtypeStruct((n, QK_W), kv_dtype),
                   jax.ShapeDtypeStruct((n, N_HEADS * V_DIM), kv_dtype)),
        grid=(n // tm,),
        in_specs=in_specs,
        out_specs=(pl.BlockSpec((tm, QK_W), lambda i: (i, 0)),) * 3,
        compiler_params=_params(("arbitrary",)),
        name="attn_qkv_rope" if rope else "attn_qkv",
    )(*args)


def _softmax_parts(s):
    e = jnp.exp(s - jnp.max(s, axis=-1, keepdims=True))
    return e, 1.0 / jnp.sum(e, axis=-1, keepdims=True)


def _attn_kernel(lp_ref, sub_ref, q_ref, k_ref, v_ref, o_ref, *, lam_init):
    lp = lp_ref[...]
    lam = (jnp.exp(jnp.sum(lp[0:1, :] * lp[1:2, :], axis=1, keepdims=True))
           - jnp.exp(jnp.sum(lp[2:3, :] * lp[3:4, :], axis=1, keepdims=True)) + lam_init)
    q = q_ref[...]
    k = k_ref[...].astype(BF16)
    e1, r1 = _softmax_parts(_dot_nt(q[:, :HEAD_DIM], k[:, :HEAD_DIM]))
    e2, r2 = _softmax_parts(_dot_nt(q[:, HEAD_DIM:], k[:, HEAD_DIM:]))
    w = e1 * r1 - lam * (e2 * r2)
    o = _dot(w.astype(BF16), v_ref[...].astype(BF16))
    o_ref[...] = (_rms(o, sub_ref[...]) * (1.0 - lam_init)).astype(BF16)


def _attention(lp, subln, q, k, v, lam_init, tq):
    bsz, t = q.shape[:2]
    tk = k.shape[1]
    return pl.pallas_call(
        functools.partial(_attn_kernel, lam_init=lam_init),
        out_shape=jax.ShapeDtypeStruct((bsz, t, N_HEADS * V_DIM), BF16),
        grid=(bsz, N_HEADS, t // tq),
        in_specs=[
            pl.BlockSpec((4, HEAD_DIM), lambda b, h, i: (0, 0)),
            pl.BlockSpec((1, V_DIM), lambda b, h, i: (0, 0)),
            pl.BlockSpec((None, tq, V_DIM), lambda b, h, i: (b, i, h)),
            pl.BlockSpec((None, tk, V_DIM), lambda b, h, i: (b, 0, h)),
            pl.BlockSpec((None, tk, V_DIM), lambda b, h, i: (b, 0, h)),
        ],
        out_specs=pl.BlockSpec((None, tq, V_DIM), lambda b, h, i: (b, i, h)),
        compiler_params=_params(("arbitrary", "arbitrary", "arbitrary")),
        name="diff_attention",
    )(lp, subln, q, k, v)


def _lru_gate_layout(gate_w, gate_b):
    nb = LRU_BLOCKS // 2
    w = gate_w.reshape(2, 2, 2, nb, LRU_BLOCK, LRU_BLOCK)
    bd = jnp.einsum("dgsbio,bc->sbidgco", w, jnp.eye(nb, dtype=w.dtype))
    bd = bd.reshape(2, LRU_HALF, 4 * LRU_HALF).astype(BF16)
    b = gate_b.reshape(2, 2, 2, LRU_HALF).transpose(2, 0, 1, 3).reshape(2, 1, 4 * LRU_HALF)
    return bd, b


def _rope_tables(n):
    rows = n // GRID_W
    row = jnp.repeat(jnp.arange(rows, dtype=F32), GRID_W)
    col = jnp.tile(jnp.arange(GRID_W, dtype=F32), rows)
    freqs = ROPE_THETA ** (-jnp.arange(ROPE_PAIRS, dtype=F32) / ROPE_PAIRS)
    ar = row[:, None] * freqs
    ac = col[:, None] * freqs
    zero = jnp.zeros_like(ar)
    cos = jnp.concatenate([jnp.cos(ar), jnp.cos(ar), jnp.cos(ac), jnp.cos(ac)], axis=-1)
    sin_up = jnp.concatenate([-jnp.sin(ar), zero, -jnp.sin(ac), zero], axis=-1)
    sin_dn = jnp.concatenate([zero, jnp.sin(ar), zero, jnp.sin(ac)], axis=-1)
    return tuple(jnp.tile(t, (1, 2)) for t in (cos, sin_up, sin_dn))


def kernel(x_prompt, x_sample, c, state_lru, cache_k, cache_v, c_ctx, mod_w, mod_b, norm_g,
           lru_w_in, lru_conv_w, lru_conv_b, lru_gate_w, lru_gate_b, lru_lambda, lru_w_out,
           attn_w_qkv, attn_lambda, attn_subln, attn_w_o,
           moe_router, moe_router_bias, moe_w_gate, moe_w_up, moe_w_down,
           shared_w_gate, shared_w_up, shared_w_down):
    bp, tp = x_prompt.shape[:2]
    bs, ts = x_sample.shape[:2]
    past = cache_k.shape[2]
    cond = jnp.concatenate([c_ctx[None], c, jnp.zeros((MOD_ROWS - 1 - bs, D_MODEL), F32)], axis=0)
    mod = _modulation(cond, mod_w, mod_b)
    streams = [dict(x=x_prompt.reshape(bp * tp, D_MODEL), rps=0, row=0, b=bp, t=tp),
               dict(x=x_sample.reshape(bs * ts, D_MODEL), rps=ts, row=1, b=bs, t=ts)]
    new_lru, new_k, new_v = [], [], []
    for i in range(DEPTH):
        j = i // N_MIXERS
        g = norm_g[i]
        router_t = moe_router[i].T
        router_b = moe_router_bias[i].reshape(N_EXPERTS, 1)
        if i % N_MIXERS == 0:
            w_in = lru_w_in[j].astype(BF16)
            w_mix = lru_w_out[j].astype(BF16)
            gate_w, gate_b = _lru_gate_layout(lru_gate_w[j], lru_gate_b[j])
        else:
            lam_init = 0.8 - 0.6 * math.exp(-0.3 * i)
            w_qkv = attn_w_qkv[j].astype(BF16)
            w_mix = attn_w_o[j].astype(BF16)
            tables = _rope_tables(ts)
        for si, s in enumerate(streams):
            if i % N_MIXERS == 0:
                gate, xr = _lru_in(s["x"], mod[i], g[0:1], w_in, s["rps"], s["row"])
                h0 = jnp.zeros((bp, 2, D_RNN), F32) if si == 0 else state_lru[:, j]
                m, h_last = _lru_core(xr, gate, h0, lru_conv_w[j], lru_conv_b[j][None], gate_w, gate_b,
                                      lru_lambda[j], s["b"], s["t"])
                if si == 0:
                    new_lru.append(h_last)
            else:
                if si == 0:
                    q, k, v = _qkv(s["x"], mod[i], g[0:1], w_qkv, s["rps"], s["row"], None, F32)
                    new_k.append(k.reshape(bp, tp, N_HEADS, 2 * HEAD_DIM))
                    new_v.append(v.reshape(bp, tp, N_HEADS, V_DIM))
                    k3 = k.reshape(bp, tp, QK_W)
                    v3 = v.reshape(bp, tp, N_HEADS * V_DIM)
                    tq = tp
                else:
                    q, k, v = _qkv(s["x"], mod[i], g[0:1], w_qkv, s["rps"], s["row"], tables, BF16)
                    k3 = jnp.concatenate([cache_k[:, j].reshape(bs, past, QK_W).astype(BF16),
                                          k.reshape(bs, ts, QK_W)], axis=1)
                    v3 = jnp.concatenate([cache_v[:, j].reshape(bs, past, N_HEADS * V_DIM).astype(BF16),
                                          v.reshape(bs, ts, N_HEADS * V_DIM)], axis=1)
                    tq = 512
                o = _attention(attn_lambda[j], attn_subln[j][None], q.reshape(s["b"], s["t"], QK_W),
                               k3, v3, lam_init, tq)
                m = o.reshape(s["b"] * s["t"], N_HEADS * V_DIM)
            s["x1"], s["xp"], s["comb"] = _mix_out(m, w_mix, s["x"], mod[i], g, router_t, router_b,
                                                   s["rps"], s["row"])
        xp = jnp.concatenate([s["xp"] for s in streams], axis=0)
        ek, rk, wk, cnt = _moe_plan(jnp.concatenate([s["comb"] for s in streams], axis=1))
        plan = _moe_layout(ek, rk, wk, cnt)
        ys = _moe_experts(plan, xp, moe_w_gate, moe_w_up, moe_w_down, i)
        tok0 = 0
        for s in streams:
            n = s["x1"].shape[0]
            ldest = plan["ldest"][tok0 * TOP_K:(tok0 + n) * TOP_K]
            s["x"] = _moe_combine(plan, ldest, tok0 // MOE_TSB, ys, s["xp"], shared_w_gate, shared_w_up,
                                  shared_w_down, i, s["x1"], mod[i], g, s["rps"], s["row"])
            tok0 += n
    return (streams[0]["x"].reshape(bp, tp, D_MODEL),
            streams[1]["x"].reshape(bs, ts, D_MODEL),
            jnp.stack(new_lru, axis=1),
            jnp.stack(new_k, axis=1),
            jnp.stack(new_v, axis=1))
```

```python
import functools
import math

import jax
import jax.numpy as jnp
from jax import lax
from jax.experimental import pallas as pl
from jax.experimental.pallas import tpu as pltpu

D_MODEL = 1024
DEPTH = 2
N_MIXERS = 2
GRID_W = 64
EPS = 1e-6
D_RNN = 1280
LRU_BLOCKS = 16
LRU_BLOCK = D_RNN // LRU_BLOCKS
CONV_W = 4
CONV_LEFT = 2
LRU_C = 8.0
N_HEADS = 8
HEAD_DIM = 64
V_DIM = 2 * HEAD_DIM
QK_W = N_HEADS * 2 * HEAD_DIM
ROPE_PAIRS = HEAD_DIM // 4
ROPE_THETA = 10000.0
N_EXPERTS = 64
N_GROUPS = 8
GROUP_SIZE = N_EXPERTS // N_GROUPS
TOPK_GROUPS = 4
TOP_K = 8
D_EXPERT = 256
D_SHARED = 256
ROUTED_SCALE = 2.5

MOD_ROWS = 8
LRU_HALF = D_RNN // 2
CONV_PAD = 8
MOE_TM = 256
MOE_TSB = 1024
MOE_CHUNK_ALIGN = 8
MOE_PIECE = 32
LANES = 128
TOK_SUB = D_MODEL // LANES
VMEM_LIMIT = 56 * 1024 * 1024
BF16 = jnp.bfloat16
F32 = jnp.float32
NEG_INF = float("-inf")


def _params(sem):
    return pltpu.CompilerParams(dimension_semantics=sem, vmem_limit_bytes=VMEM_LIMIT)


def _rms(x, g):
    return x * lax.rsqrt(jnp.mean(x * x, axis=-1, keepdims=True) + EPS) * g


def _mod_in(x, g, shift, scale):
    return _rms(x, g) * (1.0 + scale) + shift


def _dot(a, b):
    return jnp.dot(a, b, preferred_element_type=F32)


def _dot_nt(a, b, precision=None):
    return lax.dot_general(a, b, (((1,), (1,)), ((), ())), precision=precision,
                           preferred_element_type=F32)


def _mod_kernel(cond_ref, w_ref, b_ref, o_ref):
    cnd = cond_ref[...]
    s = cnd * jax.nn.sigmoid(cnd)
    o_ref[...] = jnp.dot(s, w_ref[...], precision=lax.Precision.HIGHEST,
                         preferred_element_type=F32) + b_ref[...]


def _modulation(cond, mod_w, mod_b):
    tn = 1536
    out = pl.pallas_call(
        _mod_kernel,
        out_shape=jax.ShapeDtypeStruct((DEPTH, MOD_ROWS, 6 * D_MODEL), F32),
        grid=(DEPTH, 6 * D_MODEL // tn),
        in_specs=[
            pl.BlockSpec((MOD_ROWS, D_MODEL), lambda l, n: (0, 0)),
            pl.BlockSpec((None, D_MODEL, tn), lambda l, n: (l, 0, n)),
            pl.BlockSpec((None, 1, tn), lambda l, n: (l, 0, n)),
        ],
        out_specs=pl.BlockSpec((None, MOD_ROWS, tn), lambda l, n: (l, 0, n)),
        compiler_params=_params(("arbitrary", "arbitrary")),
        name="modulation",
    )(cond, mod_w, mod_b.reshape(DEPTH, 1, 6 * D_MODEL))
    return out.reshape(DEPTH, MOD_ROWS, 6, D_MODEL)


def _mod_spec(tm, rows_per_seq, first_row):
    def index(i, *_):
        return (first_row + (i * tm) // rows_per_seq if rows_per_seq else first_row, 0, 0)
    return pl.BlockSpec((None, 6, D_MODEL), index)


def _lru_in_kernel(x_ref, mod_ref, g_ref, w_ref, gate_ref, xr_ref):
    h = _mod_in(x_ref[...], g_ref[...], mod_ref[0:1, :], mod_ref[1:2, :])
    xb = _dot(h.astype(BF16), w_ref[...])
    gate_ref[...] = jax.nn.gelu(xb[:, :D_RNN]).astype(BF16)
    xr_ref[...] = xb[:, D_RNN:]


def _lru_in(x, mod, g, w_in, rows_per_seq, first_row):
    n = x.shape[0]
    tm = 512
    return pl.pallas_call(
        _lru_in_kernel,
        out_shape=(jax.ShapeDtypeStruct((n, D_RNN), BF16), jax.ShapeDtypeStruct((n, D_RNN), F32)),
        grid=(n // tm,),
        in_specs=[
            pl.BlockSpec((tm, D_MODEL), lambda i: (i, 0)),
            _mod_spec(tm, rows_per_seq, first_row),
            pl.BlockSpec((1, D_MODEL), lambda i: (0, 0)),
            pl.BlockSpec((D_MODEL, 2 * D_RNN), lambda i: (0, 0)),
        ],
        out_specs=(pl.BlockSpec((tm, D_RNN), lambda i: (i, 0)),
                   pl.BlockSpec((tm, D_RNN), lambda i: (i, 0))),
        compiler_params=_params(("arbitrary",)),
        name="lru_in",
    )(x, mod, g, w_in)


def _lru_core_kernel(xr_ref, gate_ref, h0_ref, cw_ref, cb_ref, gw_ref, gb_ref, lam_ref,
                     m_ref, hl_ref, xpad, a_f, u_f, a_b, u_b, *, seq, chunk):
    c = LRU_HALF
    xpad[0:CONV_PAD, :] = jnp.zeros((CONV_PAD, c), F32)
    xpad[CONV_PAD + seq:, :] = jnp.zeros((CONV_PAD, c), F32)
    xpad[CONV_PAD:CONV_PAD + seq, :] = xr_ref[...]
    lam = lam_ref[...]
    coef = -LRU_C * jax.nn.softplus(-lam)
    for r0 in range(0, seq, chunk):
        xr = cb_ref[...]
        for j in range(CONV_W):
            off = CONV_PAD - CONV_LEFT + j + r0
            xr = xr + xpad[off:off + chunk, :] * cw_ref[j:j + 1, :]
        g = _dot(xr.astype(BF16), gw_ref[...]) + gb_ref[...]
        for d, (a_s, u_s) in enumerate(((a_f, u_f), (a_b, u_b))):
            r = jax.nn.sigmoid(g[:, 2 * d * c:(2 * d + 1) * c])
            i = jax.nn.sigmoid(g[:, (2 * d + 1) * c:(2 * d + 2) * c])
            log_a = coef[d:d + 1, :] * r
            a = jnp.exp(log_a)
            a_s[r0:r0 + chunk, :] = a
            u_s[r0:r0 + chunk, :] = jnp.sqrt(-jnp.tanh(log_a) * (a * a + 1.0)) * (i * xr)

    def step(t, carry):
        hf, hb = carry
        tb = seq - 1 - t
        hf = a_f[pl.ds(t, 1), :] * hf + u_f[pl.ds(t, 1), :]
        u_f[pl.ds(t, 1), :] = hf
        hb = a_b[pl.ds(tb, 1), :] * hb + u_b[pl.ds(tb, 1), :]
        u_b[pl.ds(tb, 1), :] = hb
        return hf, hb

    hf, hb = lax.fori_loop(0, seq, step, (h0_ref[0:1, :], h0_ref[1:2, :]), unroll=8)
    hl_ref[0:1, :] = hf
    hl_ref[1:2, :] = hb
    m_ref[...] = ((u_f[...] + u_b[...]) * gate_ref[...].astype(F32)).astype(BF16)


def _lru_core(xr, gate, h0, conv_w, conv_b, gate_w, gate_b, lam, n_seq, seq):
    c = LRU_HALF
    chunk = min(seq, 256)
    kernel = functools.partial(_lru_core_kernel, seq=seq, chunk=chunk)
    return pl.pallas_call(
        kernel,
        out_shape=(jax.ShapeDtypeStruct((n_seq * seq, D_RNN), BF16),
                   jax.ShapeDtypeStruct((n_seq, 2, D_RNN), F32)),
        grid=(n_seq, 2),
        in_specs=[
            pl.BlockSpec((seq, c), lambda b, j: (b, j)),
            pl.BlockSpec((seq, c), lambda b, j: (b, j)),
            pl.BlockSpec((None, 2, c), lambda b, j: (b, 0, j)),
            pl.BlockSpec((CONV_W, c), lambda b, j: (0, j)),
            pl.BlockSpec((1, c), lambda b, j: (0, j)),
            pl.BlockSpec((None, c, 4 * c), lambda b, j: (j, 0, 0)),
            pl.BlockSpec((None, 1, 4 * c), lambda b, j: (j, 0, 0)),
            pl.BlockSpec((2, c), lambda b, j: (0, j)),
        ],
        out_specs=(pl.BlockSpec((seq, c), lambda b, j: (b, j)),
                   pl.BlockSpec((None, 2, c), lambda b, j: (b, 0, j))),
        scratch_shapes=[pltpu.VMEM((seq + 2 * CONV_PAD, c), F32)] + [pltpu.VMEM((seq, c), F32)] * 4,
        compiler_params=_params(("arbitrary", "arbitrary")),
        name="lru_core",
    )(xr, gate, h0, conv_w, conv_b, gate_w, gate_b, lam)


def _route(sel, scores):
    tm = sel.shape[1]
    io8 = lax.broadcasted_iota(jnp.int32, (GROUP_SIZE, tm), 0)
    blocks, gscore = [], []
    for g in range(N_GROUPS):
        blk = sel[g * GROUP_SIZE:(g + 1) * GROUP_SIZE, :]
        m1 = jnp.max(blk, axis=0, keepdims=True)
        first = jnp.min(jnp.where(blk == m1, io8, GROUP_SIZE), axis=0, keepdims=True)
        m2 = jnp.max(jnp.where(io8 == first, NEG_INF, blk), axis=0, keepdims=True)
        blocks.append(blk)
        gscore.append(m1 + m2)
    masked = []
    for g in range(N_GROUPS):
        rank = jnp.zeros((1, tm), jnp.int32)
        for o in range(N_GROUPS):
            if o == g:
                continue
            beats = (gscore[o] >= gscore[g]) if o < g else (gscore[o] > gscore[g])
            rank = rank + beats.astype(jnp.int32)
        masked.append(jnp.where(rank < TOPK_GROUPS, blocks[g], NEG_INF))
    v = jnp.concatenate(masked, axis=0)
    ioe = lax.broadcasted_iota(jnp.int32, (N_EXPERTS, tm), 0)
    chosen = jnp.zeros((N_EXPERTS, tm), F32)
    for _ in range(TOP_K):
        mx = jnp.max(v, axis=0, keepdims=True)
        first = jnp.min(jnp.where(v == mx, ioe, N_EXPERTS), axis=0, keepdims=True)
        pick = ioe == first
        chosen = jnp.where(pick, 1.0, chosen)
        v = jnp.where(pick, NEG_INF, v)
    wsel = chosen * scores
    comb = wsel / jnp.sum(wsel, axis=0, keepdims=True) * ROUTED_SCALE
    return jnp.where(chosen > 0.0, comb, -1.0)


def _rows_to_tiles(x, tmp):
    tm = x.shape[0]
    for c in range(TOK_SUB):
        tmp[pl.ds(c, tm, stride=TOK_SUB), :] = x[:, c * LANES:(c + 1) * LANES]
    return tmp[...].reshape(tm // 2, 2 * TOK_SUB, LANES).astype(BF16)


def _tiles_to_rows(tmp, tm):
    return jnp.concatenate([tmp[pl.ds(c, tm, stride=TOK_SUB), :] for c in range(TOK_SUB)], axis=1)


def _load_token(pairs_ref, t):
    pair = pairs_ref[t >> 1].astype(F32)
    return jnp.where((t & 1) == 1, pair[TOK_SUB:], pair[:TOK_SUB])


def _mix_out_kernel(m_ref, w_ref, x_ref, mod_ref, g_ref, rt_ref, rb_ref, x1_ref, h2_ref, comb_ref, tmp):
    y = _dot(m_ref[...], w_ref[...])
    x1 = x_ref[...] + mod_ref[2:3, :] * _rms(y, g_ref[1:2, :])
    x1_ref[...] = x1
    h2 = _mod_in(x1, g_ref[2:3, :], mod_ref[3:4, :], mod_ref[4:5, :])
    h2_ref[...] = _rows_to_tiles(h2, tmp)
    logits = _dot_nt(rt_ref[...], h2, precision=lax.Precision.HIGHEST)
    scores = jax.nn.sigmoid(logits)
    comb_ref[...] = _route(scores + rb_ref[...], scores)


def _mix_out(m, w, x, mod, g, router_t, router_b, rows_per_seq, first_row):
    n, k = m.shape
    tm = 512
    return pl.pallas_call(
        _mix_out_kernel,
        out_shape=(jax.ShapeDtypeStruct((n, D_MODEL), F32),
                   jax.ShapeDtypeStruct((n // 2, 2 * TOK_SUB, LANES), BF16),
                   jax.ShapeDtypeStruct((N_EXPERTS, n), F32)),
        grid=(n // tm,),
        in_specs=[
            pl.BlockSpec((tm, k), lambda i: (i, 0)),
            pl.BlockSpec((k, D_MODEL), lambda i: (0, 0)),
            pl.BlockSpec((tm, D_MODEL), lambda i: (i, 0)),
            _mod_spec(tm, rows_per_seq, first_row),
            pl.BlockSpec((4, D_MODEL), lambda i: (0, 0)),
            pl.BlockSpec((N_EXPERTS, D_MODEL), lambda i: (0, 0)),
            pl.BlockSpec((N_EXPERTS, 1), lambda i: (0, 0)),
        ],
        out_specs=(pl.BlockSpec((tm, D_MODEL), lambda i: (i, 0)),
                   pl.BlockSpec((tm // 2, 2 * TOK_SUB, LANES), lambda i: (i, 0, 0)),
                   pl.BlockSpec((N_EXPERTS, tm), lambda i: (0, i))),
        scratch_shapes=[pltpu.VMEM((tm * TOK_SUB, LANES), F32)],
        compiler_params=_params(("arbitrary",)),
        name="mix_out",
    )(m, w, x, mod, g, router_t, router_b)


def _moe_plan_kernel(comb_ref, ek_ref, rk_ref, wk_ref, cnt_ref):
    c = comb_ref[...]
    n = c.shape[1]
    chosen = c >= 0.0
    ch = chosen.astype(BF16)
    tri = (lax.broadcasted_iota(jnp.int32, (n, n), 0) <= lax.broadcasted_iota(jnp.int32, (n, n), 1))
    incl = _dot(ch, tri.astype(BF16))
    rank = incl - ch.astype(F32)
    cnt_ref[...] = jnp.broadcast_to(incl[:, n - 1:n], cnt_ref.shape)
    low = (lax.broadcasted_iota(jnp.int32, (N_EXPERTS, N_EXPERTS), 1)
           < lax.broadcasted_iota(jnp.int32, (N_EXPERTS, N_EXPERTS), 0))
    slot = _dot(low.astype(BF16), ch)
    ioe = lax.broadcasted_iota(jnp.int32, c.shape, 0).astype(F32)
    for k in range(TOP_K):
        sel = jnp.logical_and(chosen, slot == float(k))
        ek_ref[k:k + 1, :] = jnp.sum(jnp.where(sel, ioe, 0.0), axis=0, keepdims=True).astype(jnp.int32)
        rk_ref[k:k + 1, :] = jnp.sum(jnp.where(sel, rank, 0.0), axis=0, keepdims=True).astype(jnp.int32)
        wk_ref[k:k + 1, :] = jnp.sum(jnp.where(sel, c, 0.0), axis=0, keepdims=True)


def _moe_plan(comb):
    n = comb.shape[1]
    nsb = n // MOE_TSB
    out_i = jax.ShapeDtypeStruct((TOP_K, n), jnp.int32)
    pair_spec = pl.BlockSpec((TOP_K, MOE_TSB), lambda s: (0, s))
    return pl.pallas_call(
        _moe_plan_kernel,
        out_shape=(out_i, out_i, jax.ShapeDtypeStruct((TOP_K, n), F32),
                   jax.ShapeDtypeStruct((N_EXPERTS, nsb * 128), F32)),
        grid=(nsb,),
        in_specs=[pl.BlockSpec((N_EXPERTS, MOE_TSB), lambda s: (0, s))],
        out_specs=(pair_spec, pair_spec, pair_spec, pl.BlockSpec((N_EXPERTS, 128), lambda s: (0, s))),
        compiler_params=_params(("arbitrary",)),
        name="moe_plan",
    )(comb)


def _moe_num_tiles(n_tok):
    rows = n_tok * TOP_K + N_EXPERTS * (n_tok // MOE_TSB) * (MOE_CHUNK_ALIGN - 1)
    return rows // MOE_TM + N_EXPERTS + 1


def _ceil_to(x, m):
    return (x + m - 1) // m * m


def _moe_layout(ek, rk, wk, cnt):
    n = ek.shape[1]
    nsb = n // MOE_TSB
    nt = _moe_num_tiles(n)
    n_es = cnt[:, ::128].astype(jnp.int32)
    c_al = _ceil_to(n_es, MOE_CHUNK_ALIGN)
    tiles_e = _ceil_to(jnp.sum(c_al, axis=1), MOE_TM) // MOE_TM
    tile0 = jnp.cumsum(tiles_e) - tiles_e
    cstart = tile0[:, None] * MOE_TM + jnp.cumsum(c_al, axis=1) - c_al
    npiece = _ceil_to(n_es, MOE_PIECE) // MOE_PIECE
    lbase = (jnp.cumsum(npiece, axis=0) - npiece) * MOE_PIECE
    dest, ldest = _moe_dest(ek, rk, cstart, lbase)
    dest_blocks = dest.reshape(TOP_K, nsb, MOE_TSB).transpose(1, 0, 2).reshape(-1)
    row_token = _moe_invert(dest_blocks, nt * MOE_TM)
    tile_expert = jnp.sum(jnp.arange(nt, dtype=jnp.int32)[:, None] >= tile0[None, :], axis=1) - 1
    return dict(row_token=row_token,
                tile_expert=tile_expert.astype(jnp.int32), n_used=jnp.sum(tiles_e).reshape(1).astype(jnp.int32),
                cstart=cstart.T.reshape(-1), npiece=npiece.T.reshape(-1), lbase=lbase.T.reshape(-1),
                ldest=ldest.T.reshape(-1), wk=wk.T.reshape(-1))


def _moe_dest_kernel(ek_ref, rk_ref, cs_ref, lb_ref, dest_ref, ldest_ref):
    ioe = lax.broadcasted_iota(jnp.int32, (N_EXPERTS, ek_ref.shape[1]), 0)
    cs = cs_ref[:, 0:1]
    lb = lb_ref[:, 0:1]
    for k in range(TOP_K):
        hit = ioe == ek_ref[k:k + 1, :]
        rk = rk_ref[k:k + 1, :]
        dest_ref[k:k + 1, :] = jnp.sum(jnp.where(hit, cs, 0.0), axis=0, keepdims=True).astype(jnp.int32) + rk
        ldest_ref[k:k + 1, :] = jnp.sum(jnp.where(hit, lb, 0.0), axis=0, keepdims=True).astype(jnp.int32) + rk


def _moe_dest(ek, rk, cstart, lbase):
    n = ek.shape[1]
    pair_spec = pl.BlockSpec((TOP_K, MOE_TSB), lambda s: (0, s))
    table_spec = pl.BlockSpec((N_EXPERTS, LANES), lambda s: (0, s))
    out = jax.ShapeDtypeStruct((TOP_K, n), jnp.int32)
    spread = lambda tab: jnp.repeat(tab.astype(F32), LANES, axis=1)
    return pl.pallas_call(
        _moe_dest_kernel,
        out_shape=(out, out),
        grid=(n // MOE_TSB,),
        in_specs=[pair_spec, pair_spec, table_spec, table_spec],
        out_specs=(pair_spec, pair_spec),
        compiler_params=_params(("arbitrary",)),
        name="moe_dest",
    )(ek, rk, spread(cstart), spread(lbase))


def _moe_invert_kernel(dest_hbm, rt_ref, dsm, sem, *, n_rows):
    s = pl.program_id(0)
    pairs = TOP_K * MOE_TSB

    @pl.when(s == 0)
    def _():
        def fill(j, carry):
            rt_ref[j] = 0
            return carry
        lax.fori_loop(0, n_rows, fill, 0, unroll=16)

    copy = pltpu.make_async_copy(dest_hbm.at[pl.ds(pl.multiple_of(s * pairs, pairs), pairs)], dsm, sem)
    copy.start()
    copy.wait()

    def per_token(t, carry):
        for k in range(TOP_K):
            rt_ref[dsm[k * MOE_TSB + t]] = s * MOE_TSB + t
        return carry

    lax.fori_loop(0, MOE_TSB, per_token, 0, unroll=4)


def _moe_invert(dest_blocks, n_rows):
    pairs = TOP_K * MOE_TSB
    return pl.pallas_call(
        functools.partial(_moe_invert_kernel, n_rows=n_rows),
        out_shape=jax.ShapeDtypeStruct((n_rows,), jnp.int32),
        grid=(dest_blocks.shape[0] // pairs,),
        in_specs=[pl.BlockSpec(memory_space=pl.ANY)],
        out_specs=pl.BlockSpec(memory_space=pltpu.SMEM),
        scratch_shapes=[pltpu.SMEM((pairs,), jnp.int32), pltpu.SemaphoreType.DMA],
        compiler_params=_params(("arbitrary",)),
        name="moe_invert",
    )(dest_blocks)


def _moe_expert_kernel(texp_ref, nused_ref, ids_ref, xp_ref, wg_ref, wu_ref, wd_ref, ys_ref,
                       xs, tmp, wgb, wub, wdb):
    i = pl.program_id(0)

    def gather(tile, slot):
        for r in range(MOE_TM):
            xs[slot, r * TOK_SUB:(r + 1) * TOK_SUB, :] = _load_token(xp_ref, ids_ref[tile * MOE_TM + r])

    @pl.when(i == 0)
    def _():
        gather(0, 0)

    @pl.when(i < nused_ref[0])
    def _():
        @pl.when(jnp.logical_or(i == 0, texp_ref[i] != texp_ref[jnp.maximum(i - 1, 0)]))
        def _():
            wgb[...] = wg_ref[...].astype(BF16)
            wub[...] = wu_ref[...].astype(BF16)
            wdb[...] = wd_ref[...].astype(BF16)

        slot = i % 2
        lhs = _tiles_to_rows(xs.at[slot], MOE_TM).astype(BF16)
        gather(i + 1, 1 - slot)
        act = jax.nn.silu(_dot(lhs, wgb[...])) * _dot(lhs, wub[...])
        ys_ref[...] = _rows_to_tiles(_dot(act.astype(BF16), wdb[...]), tmp)

    @pl.when(i >= nused_ref[0])
    def _():
        ys_ref[...] = jnp.zeros(ys_ref.shape, BF16)


def _moe_experts(plan, xp, w_gate, w_up, w_down, layer):
    nt = plan["tile_expert"].shape[0]
    grid_spec = pltpu.PrefetchScalarGridSpec(
        num_scalar_prefetch=3,
        grid=(nt,),
        in_specs=[
            pl.BlockSpec(memory_space=pltpu.VMEM),
            pl.BlockSpec((None, None, D_MODEL, D_EXPERT), lambda i, te, nu, ids: (layer, te[i], 0, 0)),
            pl.BlockSpec((None, None, D_MODEL, D_EXPERT), lambda i, te, nu, ids: (layer, te[i], 0, 0)),
            pl.BlockSpec((None, None, D_EXPERT, D_MODEL), lambda i, te, nu, ids: (layer, te[i], 0, 0)),
        ],
        out_specs=pl.BlockSpec((MOE_TM // 2, 2 * TOK_SUB, LANES), lambda i, te, nu, ids: (i, 0, 0)),
        scratch_shapes=[pltpu.VMEM((2, MOE_TM * TOK_SUB, LANES), F32),
                        pltpu.VMEM((MOE_TM * TOK_SUB, LANES), F32),
                        pltpu.VMEM((D_MODEL, D_EXPERT), BF16),
                        pltpu.VMEM((D_MODEL, D_EXPERT), BF16),
                        pltpu.VMEM((D_EXPERT, D_MODEL), BF16)],
    )
    return pl.pallas_call(
        _moe_expert_kernel,
        out_shape=jax.ShapeDtypeStruct((nt * MOE_TM // 2, 2 * TOK_SUB, LANES), BF16),
        grid_spec=grid_spec,
        compiler_params=_params(("arbitrary",)),
        name="moe_experts",
    )(plan["tile_expert"], plan["n_used"], plan["row_token"], xp, w_gate, w_up, w_down)


def _moe_combine_kernel(cstart_ref, npiece_ref, lbase_ref, ldest_ref, wk_ref, ys_hbm, xp_ref, sg_ref, su_ref,
                        sd_ref, x_ref, mod_ref, g_ref, o_ref, ysb, ybuf, sem, *, block0):
    blk = pl.program_id(0) + block0

    def piece_copy(src_row, dst_row):
        return pltpu.make_async_copy(ys_hbm.at[pl.ds(src_row // 2, MOE_PIECE // 2)],
                                     ysb.at[pl.ds(dst_row // 2, MOE_PIECE // 2)], sem)

    def per_expert(e, total):
        j = blk * N_EXPERTS + e
        src, dst, npc = cstart_ref[j], lbase_ref[j], npiece_ref[j]

        def per_piece(p, carry):
            piece_copy(pl.multiple_of(src + p * MOE_PIECE, MOE_CHUNK_ALIGN),
                       pl.multiple_of(dst + p * MOE_PIECE, MOE_PIECE)).start()
            return carry

        lax.fori_loop(0, npc, per_piece, 0)
        return total + npc

    total = lax.fori_loop(0, N_EXPERTS, per_expert, 0)

    ybuf[...] = xp_ref[...].astype(F32).reshape(MOE_TSB * TOK_SUB, LANES)
    lhs = _tiles_to_rows(ybuf, MOE_TSB).astype(BF16)
    act = jax.nn.silu(_dot(lhs, sg_ref[...].astype(BF16))) * _dot(lhs, su_ref[...].astype(BF16))
    o_ref[...] = _dot(act.astype(BF16), sd_ref[...].astype(BF16))

    def wait_one(p, carry):
        piece_copy(0, 0).wait()
        return carry

    lax.fori_loop(0, total, wait_one, 0)

    t0 = pl.program_id(0) * MOE_TSB

    def per_token(t, carry):
        p = (t0 + t) * TOP_K
        acc = wk_ref[p] * _load_token(ysb, ldest_ref[p])
        for k in range(1, TOP_K):
            acc = acc + wk_ref[p + k] * _load_token(ysb, ldest_ref[p + k])
        ybuf[pl.ds(pl.multiple_of(t * TOK_SUB, TOK_SUB), TOK_SUB), :] = acc
        return carry

    lax.fori_loop(0, MOE_TSB, per_token, 0, unroll=4)
    routed = _tiles_to_rows(ybuf, MOE_TSB)
    o_ref[...] = x_ref[...] + mod_ref[5:6, :] * _rms(routed + o_ref[...], g_ref[3:4, :])


def _moe_combine(plan, ldest, wk, block0, ys, xp, s_gate, s_up, s_down, layer, x1, mod, g, rows_per_seq,
                 first_row):
    n = x1.shape[0]
    cap = TOP_K * MOE_TSB + N_EXPERTS * MOE_PIECE
    grid_spec = pltpu.PrefetchScalarGridSpec(
        num_scalar_prefetch=5,
        grid=(n // MOE_TSB,),
        in_specs=[
            pl.BlockSpec(memory_space=pl.ANY),
            pl.BlockSpec((MOE_TSB // 2, 2 * TOK_SUB, LANES), lambda s, *_: (s, 0, 0)),
            pl.BlockSpec((None, D_MODEL, D_SHARED), lambda s, *_: (layer, 0, 0)),
            pl.BlockSpec((None, D_MODEL, D_SHARED), lambda s, *_: (layer, 0, 0)),
            pl.BlockSpec((None, D_SHARED, D_MODEL), lambda s, *_: (layer, 0, 0)),
            pl.BlockSpec((MOE_TSB, D_MODEL), lambda s, *_: (s, 0)),
            _mod_spec(MOE_TSB, rows_per_seq, first_row),
            pl.BlockSpec((4, D_MODEL), lambda s, *_: (0, 0)),
        ],
        out_specs=pl.BlockSpec((MOE_TSB, D_MODEL), lambda s, *_: (s, 0)),
        scratch_shapes=[pltpu.VMEM((cap // 2, 2 * TOK_SUB, LANES), BF16),
                        pltpu.VMEM((MOE_TSB * TOK_SUB, LANES), F32),
                        pltpu.SemaphoreType.DMA],
    )
    return pl.pallas_call(
        functools.partial(_moe_combine_kernel, block0=block0),
        out_shape=jax.ShapeDtypeStruct((n, D_MODEL), F32),
        grid_spec=grid_spec,
        compiler_params=_params(("arbitrary",)),
        name="moe_combine",
    )(plan["cstart"], plan["npiece"], plan["lbase"], ldest, wk, ys, xp, s_gate, s_up, s_down, x1, mod, g)


def _rope(x, cos, sin_up, sin_dn):
    out = []
    for h in range(N_HEADS):
        xs = x[:, h * V_DIM:(h + 1) * V_DIM]
        up = pltpu.roll(xs, V_DIM - ROPE_PAIRS, 1)
        dn = pltpu.roll(xs, ROPE_PAIRS, 1)
        out.append(xs * cos + up * sin_up + dn * sin_dn)
    return jnp.concatenate(out, axis=1)


def _qkv_kernel(x_ref, mod_ref, g_ref, w_ref, *rest, rope):
    if rope:
        cos_ref, sup_ref, sdn_ref, q_ref, k_ref, v_ref = rest
    else:
        q_ref, k_ref, v_ref = rest
    h = _mod_in(x_ref[...], g_ref[...], mod_ref[0:1, :], mod_ref[1:2, :])
    qkv = _dot(h.astype(BF16), w_ref[...])
    q, k, v = qkv[:, :QK_W], qkv[:, QK_W:2 * QK_W], qkv[:, 2 * QK_W:]
    if rope:
        q = _rope(q, cos_ref[...], sup_ref[...], sdn_ref[...])
        k = _rope(k, cos_ref[...], sup_ref[...], sdn_ref[...])
    q_ref[...] = (q * HEAD_DIM ** -0.5).astype(q_ref.dtype)
    k_ref[...] = k.astype(k_ref.dtype)
    v_ref[...] = v.astype(v_ref.dtype)


def _qkv(x, mod, g, w_qkv, rows_per_seq, first_row, rope_tables, kv_dtype):
    n = x.shape[0]
    tm = 512
    rope = rope_tables is not None
    in_specs = [
        pl.BlockSpec((tm, D_MODEL), lambda i: (i, 0)),
        _mod_spec(tm, rows_per_seq, first_row),
        pl.BlockSpec((1, D_MODEL), lambda i: (0, 0)),
        pl.BlockSpec((D_MODEL, 3 * QK_W), lambda i: (0, 0)),
    ]
    args = [x, mod, g, w_qkv]
    if rope:
        tiles_per_seq = rows_per_seq // tm
        in_specs += [pl.BlockSpec((tm, V_DIM), lambda i: (i % tiles_per_seq, 0))] * 3
        args += list(rope_tables)
    return pl.pallas_call(
        functools.partial(_qkv_kernel, rope=rope),
        out_shape=(jax.ShapeDtypeStruct((n, QK_W), BF16),
                   jax.ShapeDtypeStruct((n, QK_W), kv_dtype),
                   jax.ShapeDtypeStruct((n, N_HEADS * V_DIM), kv_dtype)),
        grid=(n // tm,),
        in_specs=in_specs,
        out_specs=(pl.BlockSpec((tm, QK_W), lambda i: (i, 0)),) * 3,
        compiler_params=_params(("arbitrary",)),
        name="attn_qkv_rope" if rope else "attn_qkv",
    )(*args)


def _softmax_parts(s):
    e = jnp.exp(s - jnp.max(s, axis=-1, keepdims=True))
    return e, 1.0 / jnp.sum(e, axis=-1, keepdims=True)


def _attn_kernel(lp_ref, sub_ref, q_ref, k_ref, v_ref, o_ref, *, lam_init):
    lp = lp_ref[...]
    lam = (jnp.exp(jnp.sum(lp[0:1, :] * lp[1:2, :], axis=1, keepdims=True))
           - jnp.exp(jnp.sum(lp[2:3, :] * lp[3:4, :], axis=1, keepdims=True)) + lam_init)
    q = q_ref[...]
    k = k_ref[...].astype(BF16)
    e1, r1 = _softmax_parts(_dot_nt(q[:, :HEAD_DIM], k[:, :HEAD_DIM]))
    e2, r2 = _softmax_parts(_dot_nt(q[:, HEAD_DIM:], k[:, HEAD_DIM:]))
    w = e1 * r1 - lam * (e2 * r2)
    o = _dot(w.astype(BF16), v_ref[...].astype(BF16))
    o_ref[...] = (_rms(o, sub_ref[...]) * (1.0 - lam_init)).astype(BF16)


def _attention(lp, subln, q, k, v, lam_init, tq):
    bsz, t = q.shape[:2]
    tk = k.shape[1]
    return pl.pallas_call(
        functools.partial(_attn_kernel, lam_init=lam_init),
        out_shape=jax.ShapeDtypeStruct((bsz, t, N_HEADS * V_DIM), BF16),
        grid=(bsz, N_HEADS, t // tq),
        in_specs=[
            pl.BlockSpec((4, HEAD_DIM), lambda b, h, i: (0, 0)),
            pl.BlockSpec((1, V_DIM), lambda b, h, i: (0, 0)),
            pl.BlockSpec((None, tq, V_DIM), lambda b, h, i: (b, i, h)),
            pl.BlockSpec((None, tk, V_DIM), lambda b, h, i: (b, 0, h)),
            pl.BlockSpec((None, tk, V_DIM), lambda b, h, i: (b, 0, h)),
        ],
        out_specs=pl.BlockSpec((None, tq, V_DIM), lambda b, h, i: (b, i, h)),
        compiler_params=_params(("arbitrary", "arbitrary", "arbitrary")),
        name="diff_attention",
    )(lp, subln, q, k, v)


def _lru_gate_layout(gate_w, gate_b):
    nb = LRU_BLOCKS // 2
    w = gate_w.reshape(2, 2, 2, nb, LRU_BLOCK, LRU_BLOCK)
    bd = jnp.einsum("dgsbio,bc->sbidgco", w, jnp.eye(nb, dtype=w.dtype))
    bd = bd.reshape(2, LRU_HALF, 4 * LRU_HALF).astype(BF16)
    b = gate_b.reshape(2, 2, 2, LRU_HALF).transpose(2, 0, 1, 3).reshape(2, 1, 4 * LRU_HALF)
    return bd, b


def _rope_tables(n):
    rows = n // GRID_W
    row = jnp.repeat(jnp.arange(rows, dtype=F32), GRID_W)
    col = jnp.tile(jnp.arange(GRID_W, dtype=F32), rows)
    freqs = ROPE_THETA ** (-jnp.arange(ROPE_PAIRS, dtype=F32) / ROPE_PAIRS)
    ar = row[:, None] * freqs
    ac = col[:, None] * freqs
    zero = jnp.zeros_like(ar)
    cos = jnp.concatenate([jnp.cos(ar), jnp.cos(ar), jnp.cos(ac), jnp.cos(ac)], axis=-1)
    sin_up = jnp.concatenate([-jnp.sin(ar), zero, -jnp.sin(ac), zero], axis=-1)
    sin_dn = jnp.concatenate([zero, jnp.sin(ar), zero, jnp.sin(ac)], axis=-1)
    return tuple(jnp.tile(t, (1, 2)) for t in (cos, sin_up, sin_dn))


def kernel(x_prompt, x_sample, c, state_lru, cache_k, cache_v, c_ctx, mod_w, mod_b, norm_g,
           lru_w_in, lru_conv_w, lru_conv_b, lru_gate_w, lru_gate_b, lru_lambda, lru_w_out,
           attn_w_qkv, attn_lambda, attn_subln, attn_w_o,
           moe_router, moe_router_bias, moe_w_gate, moe_w_up, moe_w_down,
           shared_w_gate, shared_w_up, shared_w_down):
    bp, tp = x_prompt.shape[:2]
    bs, ts = x_sample.shape[:2]
    past = cache_k.shape[2]
    cond = jnp.concatenate([c_ctx[None], c, jnp.zeros((MOD_ROWS - 1 - bs, D_MODEL), F32)], axis=0)
    mod = _modulation(cond, mod_w, mod_b)
    streams = [dict(x=x_prompt.reshape(bp * tp, D_MODEL), rps=0, row=0, b=bp, t=tp),
               dict(x=x_sample.reshape(bs * ts, D_MODEL), rps=ts, row=1, b=bs, t=ts)]
    new_lru, new_k, new_v = [], [], []
    for i in range(DEPTH):
        j = i // N_MIXERS
        g = norm_g[i]
        router_t = moe_router[i].T
        router_b = moe_router_bias[i].reshape(N_EXPERTS, 1)
        if i % N_MIXERS == 0:
            w_in = lru_w_in[j].astype(BF16)
            w_mix = lru_w_out[j].astype(BF16)
            gate_w, gate_b = _lru_gate_layout(lru_gate_w[j], lru_gate_b[j])
        else:
            lam_init = 0.8 - 0.6 * math.exp(-0.3 * i)
            w_qkv = attn_w_qkv[j].astype(BF16)
            w_mix = attn_w_o[j].astype(BF16)
            tables = _rope_tables(ts)
        for si, s in enumerate(streams):
            if i % N_MIXERS == 0:
                gate, xr = _lru_in(s["x"], mod[i], g[0:1], w_in, s["rps"], s["row"])
                h0 = jnp.zeros((bp, 2, D_RNN), F32) if si == 0 else state_lru[:, j]
                m, h_last = _lru_core(xr, gate, h0, lru_conv_w[j], lru_conv_b[j][None], gate_w, gate_b,
                                      lru_lambda[j], s["b"], s["t"])
                if si == 0:
                    new_lru.append(h_last)
            else:
                if si == 0:
                    q, k, v = _qkv(s["x"], mod[i], g[0:1], w_qkv, s["rps"], s["row"], None, F32)
                    new_k.append(k.reshape(bp, tp, N_HEADS, 2 * HEAD_DIM))
                    new_v.append(v.reshape(bp, tp, N_HEADS, V_DIM))
                    k3 = k.reshape(bp, tp, QK_W)
                    v3 = v.reshape(bp, tp, N_HEADS * V_DIM)
                    tq = tp
                else:
                    q, k, v = _qkv(s["x"], mod[i], g[0:1], w_qkv, s["rps"], s["row"], tables, BF16)
                    k3 = jnp.concatenate([cache_k[:, j].reshape(bs, past, QK_W).astype(BF16),
                                          k.reshape(bs, ts, QK_W)], axis=1)
                    v3 = jnp.concatenate([cache_v[:, j].reshape(bs, past, N_HEADS * V_DIM).astype(BF16),
                                          v.reshape(bs, ts, N_HEADS * V_DIM)], axis=1)
                    tq = 512
                o = _attention(attn_lambda[j], attn_subln[j][None], q.reshape(s["b"], s["t"], QK_W),
                               k3, v3, lam_init, tq)
                m = o.reshape(s["b"] * s["t"], N_HEADS * V_DIM)
            s["x1"], s["xp"], s["comb"] = _mix_out(m, w_mix, s["x"], mod[i], g, router_t, router_b,
                                                   s["rps"], s["row"])
        xp = jnp.concatenate([s["xp"] for s in streams], axis=0)
        ek, rk, wk, cnt = _moe_plan(jnp.concatenate([s["comb"] for s in streams], axis=1))
        plan = _moe_layout(ek, rk, wk, cnt)
        ys = _moe_experts(plan, xp, moe_w_gate, moe_w_up, moe_w_down, i)
        tok0 = 0
        for s in streams:
            n = s["x1"].shape[0]
            pairs = slice(tok0 * TOP_K, (tok0 + n) * TOP_K)
            s["x"] = _moe_combine(plan, plan["ldest"][pairs], plan["wk"][pairs], tok0 // MOE_TSB, ys, s["xp"],
                                  shared_w_gate, shared_w_up,
                                  shared_w_down, i, s["x1"], mod[i], g, s["rps"], s["row"])
            tok0 += n
    return (streams[0]["x"].reshape(bp, tp, D_MODEL),
            streams[1]["x"].reshape(bs, ts, D_MODEL),
            jnp.stack(new_lru, axis=1),
            jnp.stack(new_k, axis=1),
            jnp.stack(new_v, axis=1))
```

```python
import functools
import math

import jax
import jax.numpy as jnp
from jax import lax
from jax.experimental import pallas as pl
from jax.experimental.pallas import tpu as pltpu

D_MODEL = 1024
DEPTH = 2
N_MIXERS = 2
GRID_W = 64
EPS = 1e-6
D_RNN = 1280
LRU_BLOCKS = 16
LRU_BLOCK = D_RNN // LRU_BLOCKS
CONV_W = 4
CONV_LEFT = 2
LRU_C = 8.0
N_HEADS = 8
HEAD_DIM = 64
V_DIM = 2 * HEAD_DIM
QK_W = N_HEADS * 2 * HEAD_DIM
ROPE_PAIRS = HEAD_DIM // 4
ROPE_THETA = 10000.0
N_EXPERTS = 64
N_GROUPS = 8
GROUP_SIZE = N_EXPERTS // N_GROUPS
TOPK_GROUPS = 4
TOP_K = 8
D_EXPERT = 256
D_SHARED = 256
ROUTED_SCALE = 2.5

MOD_ROWS = 8
LRU_HALF = D_RNN // 2
CONV_PAD = 8
MOE_TM = 256
MOE_TSB = 1024
MOE_CHUNK_ALIGN = 8
MOE_PIECE = 32
LANES = 128
TOK_SUB = D_MODEL // LANES
VMEM_LIMIT = 56 * 1024 * 1024
BF16 = jnp.bfloat16
F32 = jnp.float32
NEG_INF = float("-inf")


def _params(sem):
    return pltpu.CompilerParams(dimension_semantics=sem, vmem_limit_bytes=VMEM_LIMIT)


def _rms(x, g):
    return x * lax.rsqrt(jnp.mean(x * x, axis=-1, keepdims=True) + EPS) * g


def _mod_in(x, g, shift, scale):
    return _rms(x, g) * (1.0 + scale) + shift


def _dot(a, b):
    return jnp.dot(a, b, preferred_element_type=F32)


def _dot_nt(a, b, precision=None):
    return lax.dot_general(a, b, (((1,), (1,)), ((), ())), precision=precision,
                           preferred_element_type=F32)


def _mod_kernel(cond_ref, w_ref, b_ref, o_ref):
    cnd = cond_ref[...]
    s = cnd * jax.nn.sigmoid(cnd)
    o_ref[...] = jnp.dot(s, w_ref[...], precision=lax.Precision.HIGHEST,
                         preferred_element_type=F32) + b_ref[...]


def _modulation(cond, mod_w, mod_b):
    tn = 1536
    out = pl.pallas_call(
        _mod_kernel,
        out_shape=jax.ShapeDtypeStruct((DEPTH, MOD_ROWS, 6 * D_MODEL), F32),
        grid=(DEPTH, 6 * D_MODEL // tn),
        in_specs=[
            pl.BlockSpec((MOD_ROWS, D_MODEL), lambda l, n: (0, 0)),
            pl.BlockSpec((None, D_MODEL, tn), lambda l, n: (l, 0, n)),
            pl.BlockSpec((None, 1, tn), lambda l, n: (l, 0, n)),
        ],
        out_specs=pl.BlockSpec((None, MOD_ROWS, tn), lambda l, n: (l, 0, n)),
        compiler_params=_params(("arbitrary", "arbitrary")),
        name="modulation",
    )(cond, mod_w, mod_b.reshape(DEPTH, 1, 6 * D_MODEL))
    return out.reshape(DEPTH, MOD_ROWS, 6, D_MODEL)


def _mod_spec(tm, rows_per_seq, first_row):
    def index(i, *_):
        return (first_row + (i * tm) // rows_per_seq if rows_per_seq else first_row, 0, 0)
    return pl.BlockSpec((None, 6, D_MODEL), index)


def _lru_in_kernel(x_ref, mod_ref, g_ref, w_ref, gate_ref, xr_ref):
    h = _mod_in(x_ref[...], g_ref[...], mod_ref[0:1, :], mod_ref[1:2, :])
    xb = _dot(h.astype(BF16), w_ref[...])
    gate_ref[...] = jax.nn.gelu(xb[:, :D_RNN]).astype(BF16)
    xr_ref[...] = xb[:, D_RNN:]


def _lru_in(x, mod, g, w_in, rows_per_seq, first_row):
    n = x.shape[0]
    tm = 512
    return pl.pallas_call(
        _lru_in_kernel,
        out_shape=(jax.ShapeDtypeStruct((n, D_RNN), BF16), jax.ShapeDtypeStruct((n, D_RNN), F32)),
        grid=(n // tm,),
        in_specs=[
            pl.BlockSpec((tm, D_MODEL), lambda i: (i, 0)),
            _mod_spec(tm, rows_per_seq, first_row),
            pl.BlockSpec((1, D_MODEL), lambda i: (0, 0)),
            pl.BlockSpec((D_MODEL, 2 * D_RNN), lambda i: (0, 0)),
        ],
        out_specs=(pl.BlockSpec((tm, D_RNN), lambda i: (i, 0)),
                   pl.BlockSpec((tm, D_RNN), lambda i: (i, 0))),
        compiler_params=_params(("arbitrary",)),
        name="lru_in",
    )(x, mod, g, w_in)


def _lru_core_kernel(xr_ref, gate_ref, h0_ref, cw_ref, cb_ref, gw_ref, gb_ref, lam_ref,
                     m_ref, hl_ref, xpad, a_f, u_f, a_b, u_b, *, seq, chunk):
    c = LRU_HALF
    xpad[0:CONV_PAD, :] = jnp.zeros((CONV_PAD, c), F32)
    xpad[CONV_PAD + seq:, :] = jnp.zeros((CONV_PAD, c), F32)
    xpad[CONV_PAD:CONV_PAD + seq, :] = xr_ref[...]
    lam = lam_ref[...]
    coef = -LRU_C * jax.nn.softplus(-lam)
    for r0 in range(0, seq, chunk):
        xr = cb_ref[...]
        for j in range(CONV_W):
            off = CONV_PAD - CONV_LEFT + j + r0
            xr = xr + xpad[off:off + chunk, :] * cw_ref[j:j + 1, :]
        g = _dot(xr.astype(BF16), gw_ref[...]) + gb_ref[...]
        for d, (a_s, u_s) in enumerate(((a_f, u_f), (a_b, u_b))):
            r = jax.nn.sigmoid(g[:, 2 * d * c:(2 * d + 1) * c])
            i = jax.nn.sigmoid(g[:, (2 * d + 1) * c:(2 * d + 2) * c])
            log_a = coef[d:d + 1, :] * r
            a = jnp.exp(log_a)
            a_s[r0:r0 + chunk, :] = a
            u_s[r0:r0 + chunk, :] = jnp.sqrt(-jnp.tanh(log_a) * (a * a + 1.0)) * (i * xr)

    def step(t, carry):
        hf, hb = carry
        tb = seq - 1 - t
        hf = a_f[pl.ds(t, 1), :] * hf + u_f[pl.ds(t, 1), :]
        u_f[pl.ds(t, 1), :] = hf
        hb = a_b[pl.ds(tb, 1), :] * hb + u_b[pl.ds(tb, 1), :]
        u_b[pl.ds(tb, 1), :] = hb
        return hf, hb

    hf, hb = lax.fori_loop(0, seq, step, (h0_ref[0:1, :], h0_ref[1:2, :]), unroll=8)
    hl_ref[0:1, :] = hf
    hl_ref[1:2, :] = hb
    m_ref[...] = ((u_f[...] + u_b[...]) * gate_ref[...].astype(F32)).astype(BF16)


def _lru_core(xr, gate, h0, conv_w, conv_b, gate_w, gate_b, lam, n_seq, seq):
    c = LRU_HALF
    chunk = min(seq, 256)
    kernel = functools.partial(_lru_core_kernel, seq=seq, chunk=chunk)
    return pl.pallas_call(
        kernel,
        out_shape=(jax.ShapeDtypeStruct((n_seq * seq, D_RNN), BF16),
                   jax.ShapeDtypeStruct((n_seq, 2, D_RNN), F32)),
        grid=(n_seq, 2),
        in_specs=[
            pl.BlockSpec((seq, c), lambda b, j: (b, j)),
            pl.BlockSpec((seq, c), lambda b, j: (b, j)),
            pl.BlockSpec((None, 2, c), lambda b, j: (b, 0, j)),
            pl.BlockSpec((CONV_W, c), lambda b, j: (0, j)),
            pl.BlockSpec((1, c), lambda b, j: (0, j)),
            pl.BlockSpec((None, c, 4 * c), lambda b, j: (j, 0, 0)),
            pl.BlockSpec((None, 1, 4 * c), lambda b, j: (j, 0, 0)),
            pl.BlockSpec((2, c), lambda b, j: (0, j)),
        ],
        out_specs=(pl.BlockSpec((seq, c), lambda b, j: (b, j)),
                   pl.BlockSpec((None, 2, c), lambda b, j: (b, 0, j))),
        scratch_shapes=[pltpu.VMEM((seq + 2 * CONV_PAD, c), F32)] + [pltpu.VMEM((seq, c), F32)] * 4,
        compiler_params=_params(("arbitrary", "arbitrary")),
        name="lru_core",
    )(xr, gate, h0, conv_w, conv_b, gate_w, gate_b, lam)


def _route(sel, scores):
    tm = sel.shape[1]
    io8 = lax.broadcasted_iota(jnp.int32, (GROUP_SIZE, tm), 0)
    blocks, gscore = [], []
    for g in range(N_GROUPS):
        blk = sel[g * GROUP_SIZE:(g + 1) * GROUP_SIZE, :]
        m1 = jnp.max(blk, axis=0, keepdims=True)
        first = jnp.min(jnp.where(blk == m1, io8, GROUP_SIZE), axis=0, keepdims=True)
        m2 = jnp.max(jnp.where(io8 == first, NEG_INF, blk), axis=0, keepdims=True)
        blocks.append(blk)
        gscore.append(m1 + m2)
    masked = []
    for g in range(N_GROUPS):
        rank = jnp.zeros((1, tm), jnp.int32)
        for o in range(N_GROUPS):
            if o == g:
                continue
            beats = (gscore[o] >= gscore[g]) if o < g else (gscore[o] > gscore[g])
            rank = rank + beats.astype(jnp.int32)
        masked.append(jnp.where(rank < TOPK_GROUPS, blocks[g], NEG_INF))
    v = jnp.concatenate(masked, axis=0)
    ioe = lax.broadcasted_iota(jnp.int32, (N_EXPERTS, tm), 0)
    chosen = jnp.zeros((N_EXPERTS, tm), F32)
    for _ in range(TOP_K):
        mx = jnp.max(v, axis=0, keepdims=True)
        first = jnp.min(jnp.where(v == mx, ioe, N_EXPERTS), axis=0, keepdims=True)
        pick = ioe == first
        chosen = jnp.where(pick, 1.0, chosen)
        v = jnp.where(pick, NEG_INF, v)
    wsel = chosen * scores
    comb = wsel / jnp.sum(wsel, axis=0, keepdims=True) * ROUTED_SCALE
    return jnp.where(chosen > 0.0, comb, -1.0)


def _rows_to_tiles(x, tmp):
    tm = x.shape[0]
    for c in range(TOK_SUB):
        tmp[pl.ds(c, tm, stride=TOK_SUB), :] = x[:, c * LANES:(c + 1) * LANES]
    return tmp[...].reshape(tm // 2, 2 * TOK_SUB, LANES).astype(BF16)


def _tiles_to_rows(tmp, tm):
    return jnp.concatenate([tmp[pl.ds(c, tm, stride=TOK_SUB), :] for c in range(TOK_SUB)], axis=1)


def _load_token(pairs_ref, t):
    pair = pairs_ref[t >> 1].astype(F32)
    return jnp.where((t & 1) == 1, pair[TOK_SUB:], pair[:TOK_SUB])


def _mix_out_kernel(m_ref, w_ref, x_ref, mod_ref, g_ref, rt_ref, rb_ref, x1_ref, h2_ref, comb_ref, tmp):
    y = _dot(m_ref[...], w_ref[...])
    x1 = x_ref[...] + mod_ref[2:3, :] * _rms(y, g_ref[1:2, :])
    x1_ref[...] = x1
    h2 = _mod_in(x1, g_ref[2:3, :], mod_ref[3:4, :], mod_ref[4:5, :])
    h2_ref[...] = _rows_to_tiles(h2, tmp)
    logits = _dot_nt(rt_ref[...], h2, precision=lax.Precision.HIGHEST)
    scores = jax.nn.sigmoid(logits)
    comb_ref[...] = _route(scores + rb_ref[...], scores)


def _mix_out(m, w, x, mod, g, router_t, router_b, rows_per_seq, first_row):
    n, k = m.shape
    tm = 512
    return pl.pallas_call(
        _mix_out_kernel,
        out_shape=(jax.ShapeDtypeStruct((n, D_MODEL), F32),
                   jax.ShapeDtypeStruct((n // 2, 2 * TOK_SUB, LANES), BF16),
                   jax.ShapeDtypeStruct((N_EXPERTS, n), F32)),
        grid=(n // tm,),
        in_specs=[
            pl.BlockSpec((tm, k), lambda i: (i, 0)),
            pl.BlockSpec((k, D_MODEL), lambda i: (0, 0)),
            pl.BlockSpec((tm, D_MODEL), lambda i: (i, 0)),
            _mod_spec(tm, rows_per_seq, first_row),
            pl.BlockSpec((4, D_MODEL), lambda i: (0, 0)),
            pl.BlockSpec((N_EXPERTS, D_MODEL), lambda i: (0, 0)),
            pl.BlockSpec((N_EXPERTS, 1), lambda i: (0, 0)),
        ],
        out_specs=(pl.BlockSpec((tm, D_MODEL), lambda i: (i, 0)),
                   pl.BlockSpec((tm // 2, 2 * TOK_SUB, LANES), lambda i: (i, 0, 0)),
                   pl.BlockSpec((N_EXPERTS, tm), lambda i: (0, i))),
        scratch_shapes=[pltpu.VMEM((tm * TOK_SUB, LANES), F32)],
        compiler_params=_params(("arbitrary",)),
        name="mix_out",
    )(m, w, x, mod, g, router_t, router_b)


def _moe_plan_kernel(comb_ref, ek_ref, rk_ref, wk_ref, cnt_ref):
    c = comb_ref[...]
    n = c.shape[1]
    chosen = c >= 0.0
    ch = chosen.astype(BF16)
    tri = (lax.broadcasted_iota(jnp.int32, (n, n), 0) <= lax.broadcasted_iota(jnp.int32, (n, n), 1))
    incl = _dot(ch, tri.astype(BF16))
    rank = incl - ch.astype(F32)
    cnt_ref[...] = jnp.broadcast_to(incl[:, n - 1:n], cnt_ref.shape)
    low = (lax.broadcasted_iota(jnp.int32, (N_EXPERTS, N_EXPERTS), 1)
           < lax.broadcasted_iota(jnp.int32, (N_EXPERTS, N_EXPERTS), 0))
    slot = _dot(low.astype(BF16), ch)
    ioe = lax.broadcasted_iota(jnp.int32, c.shape, 0).astype(F32)
    for k in range(TOP_K):
        sel = jnp.logical_and(chosen, slot == float(k))
        ek_ref[k:k + 1, :] = jnp.sum(jnp.where(sel, ioe, 0.0), axis=0, keepdims=True).astype(jnp.int32)
        rk_ref[k:k + 1, :] = jnp.sum(jnp.where(sel, rank, 0.0), axis=0, keepdims=True).astype(jnp.int32)
        wk_ref[k:k + 1, :] = jnp.sum(jnp.where(sel, c, 0.0), axis=0, keepdims=True)


def _moe_plan(comb):
    n = comb.shape[1]
    nsb = n // MOE_TSB
    out_i = jax.ShapeDtypeStruct((TOP_K, n), jnp.int32)
    pair_spec = pl.BlockSpec((TOP_K, MOE_TSB), lambda s: (0, s))
    return pl.pallas_call(
        _moe_plan_kernel,
        out_shape=(out_i, out_i, jax.ShapeDtypeStruct((TOP_K, n), F32),
                   jax.ShapeDtypeStruct((N_EXPERTS, nsb * 128), F32)),
        grid=(nsb,),
        in_specs=[pl.BlockSpec((N_EXPERTS, MOE_TSB), lambda s: (0, s))],
        out_specs=(pair_spec, pair_spec, pair_spec, pl.BlockSpec((N_EXPERTS, 128), lambda s: (0, s))),
        compiler_params=_params(("arbitrary",)),
        name="moe_plan",
    )(comb)


def _moe_num_tiles(n_tok):
    rows = n_tok * TOP_K + N_EXPERTS * (n_tok // MOE_TSB) * (MOE_CHUNK_ALIGN - 1)
    return rows // MOE_TM + N_EXPERTS + 1


def _ceil_to(x, m):
    return (x + m - 1) // m * m


def _moe_layout(ek, rk, wk, cnt):
    n = ek.shape[1]
    nsb = n // MOE_TSB
    nt = _moe_num_tiles(n)
    n_es = cnt[:, ::128].astype(jnp.int32)
    c_al = _ceil_to(n_es, MOE_CHUNK_ALIGN)
    tiles_e = _ceil_to(jnp.sum(c_al, axis=1), MOE_TM) // MOE_TM
    tile0 = jnp.cumsum(tiles_e) - tiles_e
    cstart = tile0[:, None] * MOE_TM + jnp.cumsum(c_al, axis=1) - c_al
    npiece = _ceil_to(n_es, MOE_PIECE) // MOE_PIECE
    lbase = (jnp.cumsum(npiece, axis=0) - npiece) * MOE_PIECE
    dest, ldest = _moe_dest(ek, rk, cstart, lbase)
    dest_blocks = dest.reshape(TOP_K, nsb, MOE_TSB).transpose(1, 0, 2).reshape(-1)
    row_token = _moe_invert(dest_blocks, nt * MOE_TM)
    tile_expert = jnp.sum(jnp.arange(nt, dtype=jnp.int32)[:, None] >= tile0[None, :], axis=1) - 1
    return dict(row_token=row_token,
                tile_expert=tile_expert.astype(jnp.int32), n_used=jnp.sum(tiles_e).reshape(1).astype(jnp.int32),
                cstart=cstart.T.reshape(-1), npiece=npiece.T.reshape(-1), lbase=lbase.T.reshape(-1),
                ldest=ldest.T.reshape(-1), wk=wk.T.reshape(-1))


def _moe_dest_kernel(ek_ref, rk_ref, cs_ref, lb_ref, dest_ref, ldest_ref):
    ioe = lax.broadcasted_iota(jnp.int32, (N_EXPERTS, ek_ref.shape[1]), 0)
    cs = cs_ref[:, 0:1]
    lb = lb_ref[:, 0:1]
    for k in range(TOP_K):
        hit = ioe == ek_ref[k:k + 1, :]
        rk = rk_ref[k:k + 1, :]
        dest_ref[k:k + 1, :] = jnp.sum(jnp.where(hit, cs, 0.0), axis=0, keepdims=True).astype(jnp.int32) + rk
        ldest_ref[k:k + 1, :] = jnp.sum(jnp.where(hit, lb, 0.0), axis=0, keepdims=True).astype(jnp.int32) + rk


def _moe_dest(ek, rk, cstart, lbase):
    n = ek.shape[1]
    pair_spec = pl.BlockSpec((TOP_K, MOE_TSB), lambda s: (0, s))
    table_spec = pl.BlockSpec((N_EXPERTS, LANES), lambda s: (0, s))
    out = jax.ShapeDtypeStruct((TOP_K, n), jnp.int32)
    spread = lambda tab: jnp.repeat(tab.astype(F32), LANES, axis=1)
    return pl.pallas_call(
        _moe_dest_kernel,
        out_shape=(out, out),
        grid=(n // MOE_TSB,),
        in_specs=[pair_spec, pair_spec, table_spec, table_spec],
        out_specs=(pair_spec, pair_spec),
        compiler_params=_params(("arbitrary",)),
        name="moe_dest",
    )(ek, rk, spread(cstart), spread(lbase))


def _moe_invert_kernel(dest_hbm, zeros_hbm, rt_ref, dsm, sem):
    s = pl.program_id(0)
    pairs = TOP_K * MOE_TSB

    @pl.when(s == 0)
    def _():
        fill = pltpu.make_async_copy(zeros_hbm, rt_ref, sem)
        fill.start()
        fill.wait()

    copy = pltpu.make_async_copy(dest_hbm.at[pl.ds(pl.multiple_of(s * pairs, pairs), pairs)], dsm, sem)
    copy.start()
    copy.wait()

    def per_token(t, carry):
        for k in range(TOP_K):
            rt_ref[dsm[k * MOE_TSB + t]] = s * MOE_TSB + t
        return carry

    lax.fori_loop(0, MOE_TSB, per_token, 0, unroll=4)


def _moe_invert(dest_blocks, n_rows):
    pairs = TOP_K * MOE_TSB
    return pl.pallas_call(
        _moe_invert_kernel,
        out_shape=jax.ShapeDtypeStruct((n_rows,), jnp.int32),
        grid=(dest_blocks.shape[0] // pairs,),
        in_specs=[pl.BlockSpec(memory_space=pl.ANY), pl.BlockSpec(memory_space=pl.ANY)],
        out_specs=pl.BlockSpec(memory_space=pltpu.SMEM),
        scratch_shapes=[pltpu.SMEM((pairs,), jnp.int32), pltpu.SemaphoreType.DMA],
        compiler_params=_params(("arbitrary",)),
        name="moe_invert",
    )(dest_blocks, jnp.zeros((n_rows,), jnp.int32))


def _moe_expert_kernel(texp_ref, nused_ref, ids_ref, xp_ref, wg_ref, wu_ref, wd_ref, ys_ref,
                       xs, tmp, wgb, wub, wdb):
    i = pl.program_id(0)

    def gather(tile, slot):
        for r in range(MOE_TM):
            xs[slot, r * TOK_SUB:(r + 1) * TOK_SUB, :] = _load_token(xp_ref, ids_ref[tile * MOE_TM + r])

    @pl.when(i == 0)
    def _():
        gather(0, 0)

    @pl.when(i < nused_ref[0])
    def _():
        @pl.when(jnp.logical_or(i == 0, texp_ref[i] != texp_ref[jnp.maximum(i - 1, 0)]))
        def _():
            wgb[...] = wg_ref[...].astype(BF16)
            wub[...] = wu_ref[...].astype(BF16)
            wdb[...] = wd_ref[...].astype(BF16)

        slot = i % 2
        lhs = _tiles_to_rows(xs.at[slot], MOE_TM).astype(BF16)
        gather(i + 1, 1 - slot)
        act = jax.nn.silu(_dot(lhs, wgb[...])) * _dot(lhs, wub[...])
        ys_ref[...] = _rows_to_tiles(_dot(act.astype(BF16), wdb[...]), tmp)

    @pl.when(i >= nused_ref[0])
    def _():
        ys_ref[...] = jnp.zeros(ys_ref.shape, BF16)


def _moe_experts(plan, xp, w_gate, w_up, w_down, layer):
    nt = plan["tile_expert"].shape[0]
    grid_spec = pltpu.PrefetchScalarGridSpec(
        num_scalar_prefetch=3,
        grid=(nt,),
        in_specs=[
            pl.BlockSpec(memory_space=pltpu.VMEM),
            pl.BlockSpec((None, None, D_MODEL, D_EXPERT), lambda i, te, nu, ids: (layer, te[i], 0, 0)),
            pl.BlockSpec((None, None, D_MODEL, D_EXPERT), lambda i, te, nu, ids: (layer, te[i], 0, 0)),
            pl.BlockSpec((None, None, D_EXPERT, D_MODEL), lambda i, te, nu, ids: (layer, te[i], 0, 0)),
        ],
        out_specs=pl.BlockSpec((MOE_TM // 2, 2 * TOK_SUB, LANES), lambda i, te, nu, ids: (i, 0, 0)),
        scratch_shapes=[pltpu.VMEM((2, MOE_TM * TOK_SUB, LANES), F32),
                        pltpu.VMEM((MOE_TM * TOK_SUB, LANES), F32),
                        pltpu.VMEM((D_MODEL, D_EXPERT), BF16),
                        pltpu.VMEM((D_MODEL, D_EXPERT), BF16),
                        pltpu.VMEM((D_EXPERT, D_MODEL), BF16)],
    )
    return pl.pallas_call(
        _moe_expert_kernel,
        out_shape=jax.ShapeDtypeStruct((nt * MOE_TM // 2, 2 * TOK_SUB, LANES), BF16),
        grid_spec=grid_spec,
        compiler_params=_params(("arbitrary",)),
        name="moe_experts",
    )(plan["tile_expert"], plan["n_used"], plan["row_token"], xp, w_gate, w_up, w_down)


def _moe_combine_kernel(cstart_ref, npiece_ref, lbase_ref, ldest_ref, wk_ref, ys_hbm, xp_ref, sg_ref, su_ref,
                        sd_ref, x_ref, mod_ref, g_ref, o_ref, ysb, ybuf, sem, *, block0):
    blk = pl.program_id(0) + block0

    def piece_copy(src_row, dst_row):
        return pltpu.make_async_copy(ys_hbm.at[pl.ds(src_row // 2, MOE_PIECE // 2)],
                                     ysb.at[pl.ds(dst_row // 2, MOE_PIECE // 2)], sem)

    def per_expert(e, total):
        j = blk * N_EXPERTS + e
        src, dst, npc = cstart_ref[j], lbase_ref[j], npiece_ref[j]

        def per_piece(p, carry):
            piece_copy(pl.multiple_of(src + p * MOE_PIECE, MOE_CHUNK_ALIGN),
                       pl.multiple_of(dst + p * MOE_PIECE, MOE_PIECE)).start()
            return carry

        lax.fori_loop(0, npc, per_piece, 0)
        return total + npc

    total = lax.fori_loop(0, N_EXPERTS, per_expert, 0)

    ybuf[...] = xp_ref[...].astype(F32).reshape(MOE_TSB * TOK_SUB, LANES)
    lhs = _tiles_to_rows(ybuf, MOE_TSB).astype(BF16)
    act = jax.nn.silu(_dot(lhs, sg_ref[...].astype(BF16))) * _dot(lhs, su_ref[...].astype(BF16))
    o_ref[...] = _dot(act.astype(BF16), sd_ref[...].astype(BF16))

    def wait_one(p, carry):
        piece_copy(0, 0).wait()
        return carry

    lax.fori_loop(0, total, wait_one, 0)

    t0 = pl.program_id(0) * MOE_TSB

    def per_token(t, carry):
        p = (t0 + t) * TOP_K
        acc = wk_ref[p] * _load_token(ysb, ldest_ref[p])
        for k in range(1, TOP_K):
            acc = acc + wk_ref[p + k] * _load_token(ysb, ldest_ref[p + k])
        ybuf[pl.ds(pl.multiple_of(t * TOK_SUB, TOK_SUB), TOK_SUB), :] = acc
        return carry

    lax.fori_loop(0, MOE_TSB, per_token, 0, unroll=4)
    routed = _tiles_to_rows(ybuf, MOE_TSB)
    o_ref[...] = x_ref[...] + mod_ref[5:6, :] * _rms(routed + o_ref[...], g_ref[3:4, :])


def _moe_combine(plan, ldest, wk, block0, ys, xp, s_gate, s_up, s_down, layer, x1, mod, g, rows_per_seq,
                 first_row):
    n = x1.shape[0]
    cap = TOP_K * MOE_TSB + N_EXPERTS * MOE_PIECE
    grid_spec = pltpu.PrefetchScalarGridSpec(
        num_scalar_prefetch=5,
        grid=(n // MOE_TSB,),
        in_specs=[
            pl.BlockSpec(memory_space=pl.ANY),
            pl.BlockSpec((MOE_TSB // 2, 2 * TOK_SUB, LANES), lambda s, *_: (s, 0, 0)),
            pl.BlockSpec((None, D_MODEL, D_SHARED), lambda s, *_: (layer, 0, 0)),
            pl.BlockSpec((None, D_MODEL, D_SHARED), lambda s, *_: (layer, 0, 0)),
            pl.BlockSpec((None, D_SHARED, D_MODEL), lambda s, *_: (layer, 0, 0)),
            pl.BlockSpec((MOE_TSB, D_MODEL), lambda s, *_: (s, 0)),
            _mod_spec(MOE_TSB, rows_per_seq, first_row),
            pl.BlockSpec((4, D_MODEL), lambda s, *_: (0, 0)),
        ],
        out_specs=pl.BlockSpec((MOE_TSB, D_MODEL), lambda s, *_: (s, 0)),
        scratch_shapes=[pltpu.VMEM((cap // 2, 2 * TOK_SUB, LANES), BF16),
                        pltpu.VMEM((MOE_TSB * TOK_SUB, LANES), F32),
                        pltpu.SemaphoreType.DMA],
    )
    return pl.pallas_call(
        functools.partial(_moe_combine_kernel, block0=block0),
        out_shape=jax.ShapeDtypeStruct((n, D_MODEL), F32),
        grid_spec=grid_spec,
        compiler_params=_params(("arbitrary",)),
        name="moe_combine",
    )(plan["cstart"], plan["npiece"], plan["lbase"], ldest, wk, ys, xp, s_gate, s_up, s_down, x1, mod, g)


def _rope(x, cos, sin_up, sin_dn):
    out = []
    for h in range(N_HEADS):
        xs = x[:, h * V_DIM:(h + 1) * V_DIM]
        up = pltpu.roll(xs, V_DIM - ROPE_PAIRS, 1)
        dn = pltpu.roll(xs, ROPE_PAIRS, 1)
        out.append(xs * cos + up * sin_up + dn * sin_dn)
    return jnp.concatenate(out, axis=1)


def _qkv_kernel(x_ref, mod_ref, g_ref, w_ref, *rest, rope):
    if rope:
        cos_ref, sup_ref, sdn_ref, q_ref, k_ref, v_ref = rest
    else:
        q_ref, k_ref, v_ref = rest
    h = _mod_in(x_ref[...], g_ref[...], mod_ref[0:1, :], mod_ref[1:2, :])
    qkv = _dot(h.astype(BF16), w_ref[...])
    q, k, v = qkv[:, :QK_W], qkv[:, QK_W:2 * QK_W], qkv[:, 2 * QK_W:]
    if rope:
        q = _rope(q, cos_ref[...], sup_ref[...], sdn_ref[...])
        k = _rope(k, cos_ref[...], sup_ref[...], sdn_ref[...])
    q_ref[...] = (q * HEAD_DIM ** -0.5).astype(q_ref.dtype)
    k_ref[...] = k.astype(k_ref.dtype)
    v_ref[...] = v.astype(v_ref.dtype)


def _qkv(x, mod, g, w_qkv, rows_per_seq, first_row, rope_tables, kv_dtype):
    n = x.shape[0]
    tm = 512
    rope = rope_tables is not None
    in_specs = [
        pl.BlockSpec((tm, D_MODEL), lambda i: (i, 0)),
        _mod_spec(tm, rows_per_seq, first_row),
        pl.BlockSpec((1, D_MODEL), lambda i: (0, 0)),
        pl.BlockSpec((D_MODEL, 3 * QK_W), lambda i: (0, 0)),
    ]
    args = [x, mod, g, w_qkv]
    if rope:
        tiles_per_seq = rows_per_seq // tm
        in_specs += [pl.BlockSpec((tm, V_DIM), lambda i: (i % tiles_per_seq, 0))] * 3
        args += list(rope_tables)
    return pl.pallas_call(
        functools.partial(_qkv_kernel, rope=rope),
        out_shape=(jax.ShapeDtypeStruct((n, QK_W), BF16),
                   jax.ShapeDtypeStruct((n, QK_W), kv_dtype),
                   jax.ShapeDtypeStruct((n, N_HEADS * V_DIM), kv_dtype)),
        grid=(n // tm,),
        in_specs=in_specs,
        out_specs=(pl.BlockSpec((tm, QK_W), lambda i: (i, 0)),) * 3,
        compiler_params=_params(("arbitrary",)),
        name="attn_qkv_rope" if rope else "attn_qkv",
    )(*args)


def _softmax_parts(s):
    e = jnp.exp(s - jnp.max(s, axis=-1, keepdims=True))
    return e, 1.0 / jnp.sum(e, axis=-1, keepdims=True)


def _attn_kernel(lp_ref, sub_ref, q_ref, k_ref, v_ref, o_ref, *, lam_init):
    lp = lp_ref[...]
    lam = (jnp.exp(jnp.sum(lp[0:1, :] * lp[1:2, :], axis=1, keepdims=True))
           - jnp.exp(jnp.sum(lp[2:3, :] * lp[3:4, :], axis=1, keepdims=True)) + lam_init)
    for h in range(N_HEADS):
        c0 = h * V_DIM
        q = q_ref[:, c0:c0 + V_DIM]
        k = k_ref[:, c0:c0 + V_DIM].astype(BF16)
        e1, r1 = _softmax_parts(_dot_nt(q[:, :HEAD_DIM], k[:, :HEAD_DIM]))
        e2, r2 = _softmax_parts(_dot_nt(q[:, HEAD_DIM:], k[:, HEAD_DIM:]))
        w = e1 * r1 - lam * (e2 * r2)
        o = _dot(w.astype(BF16), v_ref[:, c0:c0 + V_DIM].astype(BF16))
        o_ref[:, c0:c0 + V_DIM] = (_rms(o, sub_ref[...]) * (1.0 - lam_init)).astype(BF16)


def _attention(lp, subln, q, k, v, lam_init, tq):
    bsz, t = q.shape[:2]
    tk = k.shape[1]
    return pl.pallas_call(
        functools.partial(_attn_kernel, lam_init=lam_init),
        out_shape=jax.ShapeDtypeStruct((bsz, t, N_HEADS * V_DIM), BF16),
        grid=(bsz, t // tq),
        in_specs=[
            pl.BlockSpec((4, HEAD_DIM), lambda b, i: (0, 0)),
            pl.BlockSpec((1, V_DIM), lambda b, i: (0, 0)),
            pl.BlockSpec((None, tq, QK_W), lambda b, i: (b, i, 0)),
            pl.BlockSpec((None, tk, QK_W), lambda b, i: (b, 0, 0)),
            pl.BlockSpec((None, tk, N_HEADS * V_DIM), lambda b, i: (b, 0, 0)),
        ],
        out_specs=pl.BlockSpec((None, tq, N_HEADS * V_DIM), lambda b, i: (b, i, 0)),
        compiler_params=_params(("arbitrary", "arbitrary")),
        name="diff_attention",
    )(lp, subln, q, k, v)


def _lru_gate_layout(gate_w, gate_b):
    nb = LRU_BLOCKS // 2
    w = gate_w.reshape(2, 2, 2, nb, LRU_BLOCK, LRU_BLOCK)
    bd = jnp.einsum("dgsbio,bc->sbidgco", w, jnp.eye(nb, dtype=w.dtype))
    bd = bd.reshape(2, LRU_HALF, 4 * LRU_HALF).astype(BF16)
    b = gate_b.reshape(2, 2, 2, LRU_HALF).transpose(2, 0, 1, 3).reshape(2, 1, 4 * LRU_HALF)
    return bd, b


def _rope_tables(n):
    rows = n // GRID_W
    row = jnp.repeat(jnp.arange(rows, dtype=F32), GRID_W)
    col = jnp.tile(jnp.arange(GRID_W, dtype=F32), rows)
    freqs = ROPE_THETA ** (-jnp.arange(ROPE_PAIRS, dtype=F32) / ROPE_PAIRS)
    ar = row[:, None] * freqs
    ac = col[:, None] * freqs
    zero = jnp.zeros_like(ar)
    cos = jnp.concatenate([jnp.cos(ar), jnp.cos(ar), jnp.cos(ac), jnp.cos(ac)], axis=-1)
    sin_up = jnp.concatenate([-jnp.sin(ar), zero, -jnp.sin(ac), zero], axis=-1)
    sin_dn = jnp.concatenate([zero, jnp.sin(ar), zero, jnp.sin(ac)], axis=-1)
    return tuple(jnp.tile(t, (1, 2)) for t in (cos, sin_up, sin_dn))


def kernel(x_prompt, x_sample, c, state_lru, cache_k, cache_v, c_ctx, mod_w, mod_b, norm_g,
           lru_w_in, lru_conv_w, lru_conv_b, lru_gate_w, lru_gate_b, lru_lambda, lru_w_out,
           attn_w_qkv, attn_lambda, attn_subln, attn_w_o,
           moe_router, moe_router_bias, moe_w_gate, moe_w_up, moe_w_down,
           shared_w_gate, shared_w_up, shared_w_down):
    bp, tp = x_prompt.shape[:2]
    bs, ts = x_sample.shape[:2]
    past = cache_k.shape[2]
    cond = jnp.concatenate([c_ctx[None], c, jnp.zeros((MOD_ROWS - 1 - bs, D_MODEL), F32)], axis=0)
    mod = _modulation(cond, mod_w, mod_b)
    streams = [dict(x=x_prompt.reshape(bp * tp, D_MODEL), rps=0, row=0, b=bp, t=tp),
               dict(x=x_sample.reshape(bs * ts, D_MODEL), rps=ts, row=1, b=bs, t=ts)]
    new_lru, new_k, new_v = [], [], []
    for i in range(DEPTH):
        j = i // N_MIXERS
        g = norm_g[i]
        router_t = moe_router[i].T
        router_b = moe_router_bias[i].reshape(N_EXPERTS, 1)
        if i % N_MIXERS == 0:
            w_in = lru_w_in[j].astype(BF16)
            w_mix = lru_w_out[j].astype(BF16)
            gate_w, gate_b = _lru_gate_layout(lru_gate_w[j], lru_gate_b[j])
        else:
            lam_init = 0.8 - 0.6 * math.exp(-0.3 * i)
            w_qkv = attn_w_qkv[j].astype(BF16)
            w_mix = attn_w_o[j].astype(BF16)
            tables = _rope_tables(ts)
        for si, s in enumerate(streams):
            if i % N_MIXERS == 0:
                gate, xr = _lru_in(s["x"], mod[i], g[0:1], w_in, s["rps"], s["row"])
                h0 = jnp.zeros((bp, 2, D_RNN), F32) if si == 0 else state_lru[:, j]
                m, h_last = _lru_core(xr, gate, h0, lru_conv_w[j], lru_conv_b[j][None], gate_w, gate_b,
                                      lru_lambda[j], s["b"], s["t"])
                if si == 0:
                    new_lru.append(h_last)
            else:
                if si == 0:
                    q, k, v = _qkv(s["x"], mod[i], g[0:1], w_qkv, s["rps"], s["row"], None, F32)
                    new_k.append(k.reshape(bp, tp, N_HEADS, 2 * HEAD_DIM))
                    new_v.append(v.reshape(bp, tp, N_HEADS, V_DIM))
                    k3 = k.reshape(bp, tp, QK_W)
                    v3 = v.reshape(bp, tp, N_HEADS * V_DIM)
                    tq = tp
                else:
                    q, k, v = _qkv(s["x"], mod[i], g[0:1], w_qkv, s["rps"], s["row"], tables, BF16)
                    k3 = jnp.concatenate([cache_k[:, j].reshape(bs, past, QK_W).astype(BF16),
                                          k.reshape(bs, ts, QK_W)], axis=1)
                    v3 = jnp.concatenate([cache_v[:, j].reshape(bs, past, N_HEADS * V_DIM).astype(BF16),
                                          v.reshape(bs, ts, N_HEADS * V_DIM)], axis=1)
                    tq = 512
                o = _attention(attn_lambda[j], attn_subln[j][None], q.reshape(s["b"], s["t"], QK_W),
                               k3, v3, lam_init, tq)
                m = o.reshape(s["b"] * s["t"], N_HEADS * V_DIM)
            s["x1"], s["xp"], s["comb"] = _mix_out(m, w_mix, s["x"], mod[i], g, router_t, router_b,
                                                   s["rps"], s["row"])
        xp = jnp.concatenate([s["xp"] for s in streams], axis=0)
        ek, rk, wk, cnt = _moe_plan(jnp.concatenate([s["comb"] for s in streams], axis=1))
        plan = _moe_layout(ek, rk, wk, cnt)
        ys = _moe_experts(plan, xp, moe_w_gate, moe_w_up, moe_w_down, i)
        tok0 = 0
        for s in streams:
            n = s["x1"].shape[0]
            pairs = slice(tok0 * TOP_K, (tok0 + n) * TOP_K)
            s["x"] = _moe_combine(plan, plan["ldest"][pairs], plan["wk"][pairs], tok0 // MOE_TSB, ys, s["xp"],
                                  shared_w_gate, shared_w_up,
                                  shared_w_down, i, s["x1"], mod[i], g, s["rps"], s["row"])
            tok0 += n
    return (streams[0]["x"].reshape(bp, tp, D_MODEL),
            streams[1]["x"].reshape(bs, ts, D_MODEL),
            jnp.stack(new_lru, axis=1),
            jnp.stack(new_k, axis=1),
            jnp.stack(new_v, axis=1))
```

```python
import functools
import math

import jax
import jax.numpy as jnp
from jax import lax
from jax.experimental import pallas as pl
from jax.experimental.pallas import tpu as pltpu

D_MODEL = 1024
DEPTH = 2
N_MIXERS = 2
GRID_W = 64
EPS = 1e-6
D_RNN = 1280
LRU_BLOCKS = 16
LRU_BLOCK = D_RNN // LRU_BLOCKS
CONV_W = 4
CONV_LEFT = 2
LRU_C = 8.0
N_HEADS = 8
HEAD_DIM = 64
V_DIM = 2 * HEAD_DIM
QK_W = N_HEADS * 2 * HEAD_DIM
ROPE_PAIRS = HEAD_DIM // 4
ROPE_THETA = 10000.0
N_EXPERTS = 64
N_GROUPS = 8
GROUP_SIZE = N_EXPERTS // N_GROUPS
TOPK_GROUPS = 4
TOP_K = 8
D_EXPERT = 256
D_SHARED = 256
ROUTED_SCALE = 2.5

MOD_ROWS = 8
LRU_HALF = D_RNN // 2
CONV_PAD = 8
MOE_TM = 512
MOE_TSB = 512
MOE_CHUNK_ALIGN = 8
MOE_PIECE = 32
LANES = 128
TOK_SUB = D_MODEL // LANES
VMEM_LIMIT = 56 * 1024 * 1024
BF16 = jnp.bfloat16
F32 = jnp.float32
NEG_INF = float("-inf")


def _params(sem):
    return pltpu.CompilerParams(dimension_semantics=sem, vmem_limit_bytes=VMEM_LIMIT)


def _rms(x, g):
    return x * lax.rsqrt(jnp.mean(x * x, axis=-1, keepdims=True) + EPS) * g


def _mod_in(x, g, shift, scale):
    return _rms(x, g) * (1.0 + scale) + shift


def _dot(a, b):
    return jnp.dot(a, b, preferred_element_type=F32)


def _dot_nt(a, b, precision=None):
    return lax.dot_general(a, b, (((1,), (1,)), ((), ())), precision=precision,
                           preferred_element_type=F32)


def _mod_kernel(cond_ref, w_ref, b_ref, o_ref):
    cnd = cond_ref[...]
    s = cnd * jax.nn.sigmoid(cnd)
    o_ref[...] = jnp.dot(s, w_ref[...], precision=lax.Precision.HIGHEST,
                         preferred_element_type=F32) + b_ref[...]


def _modulation(cond, mod_w, mod_b):
    tn = 1536
    out = pl.pallas_call(
        _mod_kernel,
        out_shape=jax.ShapeDtypeStruct((DEPTH, MOD_ROWS, 6 * D_MODEL), F32),
        grid=(DEPTH, 6 * D_MODEL // tn),
        in_specs=[
            pl.BlockSpec((MOD_ROWS, D_MODEL), lambda l, n: (0, 0)),
            pl.BlockSpec((None, D_MODEL, tn), lambda l, n: (l, 0, n)),
            pl.BlockSpec((None, 1, tn), lambda l, n: (l, 0, n)),
        ],
        out_specs=pl.BlockSpec((None, MOD_ROWS, tn), lambda l, n: (l, 0, n)),
        compiler_params=_params(("arbitrary", "arbitrary")),
        name="modulation",
    )(cond, mod_w, mod_b.reshape(DEPTH, 1, 6 * D_MODEL))
    return out.reshape(DEPTH, MOD_ROWS, 6, D_MODEL)


def _mod_spec(tm, rows_per_seq, first_row):
    def index(i, *_):
        return (first_row + (i * tm) // rows_per_seq if rows_per_seq else first_row, 0, 0)
    return pl.BlockSpec((None, 6, D_MODEL), index)


def _lru_in_kernel(x_ref, mod_ref, g_ref, w_ref, gate_ref, xr_ref):
    h = _mod_in(x_ref[...], g_ref[...], mod_ref[0:1, :], mod_ref[1:2, :])
    xb = _dot(h.astype(BF16), w_ref[...])
    gate_ref[...] = jax.nn.gelu(xb[:, :D_RNN]).astype(BF16)
    xr_ref[...] = xb[:, D_RNN:]


def _lru_in(x, mod, g, w_in, rows_per_seq, first_row):
    n = x.shape[0]
    tm = 512
    return pl.pallas_call(
        _lru_in_kernel,
        out_shape=(jax.ShapeDtypeStruct((n, D_RNN), BF16), jax.ShapeDtypeStruct((n, D_RNN), F32)),
        grid=(n // tm,),
        in_specs=[
            pl.BlockSpec((tm, D_MODEL), lambda i: (i, 0)),
            _mod_spec(tm, rows_per_seq, first_row),
            pl.BlockSpec((1, D_MODEL), lambda i: (0, 0)),
            pl.BlockSpec((D_MODEL, 2 * D_RNN), lambda i: (0, 0)),
        ],
        out_specs=(pl.BlockSpec((tm, D_RNN), lambda i: (i, 0)),
                   pl.BlockSpec((tm, D_RNN), lambda i: (i, 0))),
        compiler_params=_params(("arbitrary",)),
        name="lru_in",
    )(x, mod, g, w_in)


def _lru_core_kernel(xr_ref, gate_ref, h0_ref, cw_ref, cb_ref, gw_ref, gb_ref, lam_ref,
                     m_ref, hl_ref, xpad, a_f, u_f, a_b, u_b, *, seq, chunk):
    c = LRU_HALF
    xpad[0:CONV_PAD, :] = jnp.zeros((CONV_PAD, c), F32)
    xpad[CONV_PAD + seq:, :] = jnp.zeros((CONV_PAD, c), F32)
    xpad[CONV_PAD:CONV_PAD + seq, :] = xr_ref[...]
    lam = lam_ref[...]
    coef = -LRU_C * jax.nn.softplus(-lam)
    for r0 in range(0, seq, chunk):
        xr = cb_ref[...]
        for j in range(CONV_W):
            off = CONV_PAD - CONV_LEFT + j + r0
            xr = xr + xpad[off:off + chunk, :] * cw_ref[j:j + 1, :]
        g = _dot(xr.astype(BF16), gw_ref[...]) + gb_ref[...]
        for d, (a_s, u_s) in enumerate(((a_f, u_f), (a_b, u_b))):
            r = jax.nn.sigmoid(g[:, 2 * d * c:(2 * d + 1) * c])
            i = jax.nn.sigmoid(g[:, (2 * d + 1) * c:(2 * d + 2) * c])
            log_a = coef[d:d + 1, :] * r
            a = jnp.exp(log_a)
            a_s[r0:r0 + chunk, :] = a
            u_s[r0:r0 + chunk, :] = jnp.sqrt(-jnp.tanh(log_a) * (a * a + 1.0)) * (i * xr)

    def step(t, carry):
        hf, hb = carry
        tb = seq - 1 - t
        hf = a_f[pl.ds(t, 1), :] * hf + u_f[pl.ds(t, 1), :]
        u_f[pl.ds(t, 1), :] = hf
        hb = a_b[pl.ds(tb, 1), :] * hb + u_b[pl.ds(tb, 1), :]
        u_b[pl.ds(tb, 1), :] = hb
        return hf, hb

    hf, hb = lax.fori_loop(0, seq, step, (h0_ref[0:1, :], h0_ref[1:2, :]), unroll=8)
    hl_ref[0:1, :] = hf
    hl_ref[1:2, :] = hb
    m_ref[...] = ((u_f[...] + u_b[...]) * gate_ref[...].astype(F32)).astype(BF16)


def _lru_core(xr, gate, h0, conv_w, conv_b, gate_w, gate_b, lam, n_seq, seq):
    c = LRU_HALF
    chunk = min(seq, 256)
    kernel = functools.partial(_lru_core_kernel, seq=seq, chunk=chunk)
    return pl.pallas_call(
        kernel,
        out_shape=(jax.ShapeDtypeStruct((n_seq * seq, D_RNN), BF16),
                   jax.ShapeDtypeStruct((n_seq, 2, D_RNN), F32)),
        grid=(n_seq, 2),
        in_specs=[
            pl.BlockSpec((seq, c), lambda b, j: (b, j)),
            pl.BlockSpec((seq, c), lambda b, j: (b, j)),
            pl.BlockSpec((None, 2, c), lambda b, j: (b, 0, j)),
            pl.BlockSpec((CONV_W, c), lambda b, j: (0, j)),
            pl.BlockSpec((1, c), lambda b, j: (0, j)),
            pl.BlockSpec((None, c, 4 * c), lambda b, j: (j, 0, 0)),
            pl.BlockSpec((None, 1, 4 * c), lambda b, j: (j, 0, 0)),
            pl.BlockSpec((2, c), lambda b, j: (0, j)),
        ],
        out_specs=(pl.BlockSpec((seq, c), lambda b, j: (b, j)),
                   pl.BlockSpec((None, 2, c), lambda b, j: (b, 0, j))),
        scratch_shapes=[pltpu.VMEM((seq + 2 * CONV_PAD, c), F32)] + [pltpu.VMEM((seq, c), F32)] * 4,
        compiler_params=_params(("arbitrary", "arbitrary")),
        name="lru_core",
    )(xr, gate, h0, conv_w, conv_b, gate_w, gate_b, lam)


def _route(sel, scores):
    tm = sel.shape[1]
    io8 = lax.broadcasted_iota(jnp.int32, (GROUP_SIZE, tm), 0)
    blocks, gscore = [], []
    for g in range(N_GROUPS):
        blk = sel[g * GROUP_SIZE:(g + 1) * GROUP_SIZE, :]
        m1 = jnp.max(blk, axis=0, keepdims=True)
        first = jnp.min(jnp.where(blk == m1, io8, GROUP_SIZE), axis=0, keepdims=True)
        m2 = jnp.max(jnp.where(io8 == first, NEG_INF, blk), axis=0, keepdims=True)
        blocks.append(blk)
        gscore.append(m1 + m2)
    masked = []
    for g in range(N_GROUPS):
        rank = jnp.zeros((1, tm), jnp.int32)
        for o in range(N_GROUPS):
            if o == g:
                continue
            beats = (gscore[o] >= gscore[g]) if o < g else (gscore[o] > gscore[g])
            rank = rank + beats.astype(jnp.int32)
        masked.append(jnp.where(rank < TOPK_GROUPS, blocks[g], NEG_INF))
    v = jnp.concatenate(masked, axis=0)
    ioe = lax.broadcasted_iota(jnp.int32, (N_EXPERTS, tm), 0)
    chosen = jnp.zeros((N_EXPERTS, tm), F32)
    for _ in range(TOP_K):
        mx = jnp.max(v, axis=0, keepdims=True)
        first = jnp.min(jnp.where(v == mx, ioe, N_EXPERTS), axis=0, keepdims=True)
        pick = ioe == first
        chosen = jnp.where(pick, 1.0, chosen)
        v = jnp.where(pick, NEG_INF, v)
    wsel = chosen * scores
    comb = wsel / jnp.sum(wsel, axis=0, keepdims=True) * ROUTED_SCALE
    return jnp.where(chosen > 0.0, comb, -1.0)


def _rows_to_tiles(x, tmp):
    tm = x.shape[0]
    for c in range(TOK_SUB):
        tmp[pl.ds(c, tm, stride=TOK_SUB), :] = x[:, c * LANES:(c + 1) * LANES]
    return tmp[...].reshape(tm // 2, 2 * TOK_SUB, LANES).astype(BF16)


def _tiles_to_rows(tmp, tm):
    return jnp.concatenate([tmp[pl.ds(c, tm, stride=TOK_SUB), :] for c in range(TOK_SUB)], axis=1)


def _load_token(pairs_ref, t):
    pair = pairs_ref[t >> 1].astype(F32)
    return jnp.where((t & 1) == 1, pair[TOK_SUB:], pair[:TOK_SUB])


def _mix_out_kernel(m_ref, w_ref, x_ref, mod_ref, g_ref, rt_ref, rb_ref, x1_ref, h2_ref, comb_ref, tmp):
    y = _dot(m_ref[...], w_ref[...])
    x1 = x_ref[...] + mod_ref[2:3, :] * _rms(y, g_ref[1:2, :])
    x1_ref[...] = x1
    h2 = _mod_in(x1, g_ref[2:3, :], mod_ref[3:4, :], mod_ref[4:5, :])
    h2_ref[...] = _rows_to_tiles(h2, tmp)
    logits = _dot_nt(rt_ref[...], h2, precision=lax.Precision.HIGHEST)
    scores = jax.nn.sigmoid(logits)
    comb_ref[...] = _route(scores + rb_ref[...], scores)


def _mix_out(m, w, x, mod, g, router_t, router_b, rows_per_seq, first_row):
    n, k = m.shape
    tm = 512
    return pl.pallas_call(
        _mix_out_kernel,
        out_shape=(jax.ShapeDtypeStruct((n, D_MODEL), F32),
                   jax.ShapeDtypeStruct((n // 2, 2 * TOK_SUB, LANES), BF16),
                   jax.ShapeDtypeStruct((N_EXPERTS, n), F32)),
        grid=(n // tm,),
        in_specs=[
            pl.BlockSpec((tm, k), lambda i: (i, 0)),
            pl.BlockSpec((k, D_MODEL), lambda i: (0, 0)),
            pl.BlockSpec((tm, D_MODEL), lambda i: (i, 0)),
            _mod_spec(tm, rows_per_seq, first_row),
            pl.BlockSpec((4, D_MODEL), lambda i: (0, 0)),
            pl.BlockSpec((N_EXPERTS, D_MODEL), lambda i: (0, 0)),
            pl.BlockSpec((N_EXPERTS, 1), lambda i: (0, 0)),
        ],
        out_specs=(pl.BlockSpec((tm, D_MODEL), lambda i: (i, 0)),
                   pl.BlockSpec((tm // 2, 2 * TOK_SUB, LANES), lambda i: (i, 0, 0)),
                   pl.BlockSpec((N_EXPERTS, tm), lambda i: (0, i))),
        scratch_shapes=[pltpu.VMEM((tm * TOK_SUB, LANES), F32)],
        compiler_params=_params(("arbitrary",)),
        name="mix_out",
    )(m, w, x, mod, g, router_t, router_b)


def _moe_plan_kernel(comb_ref, ek_ref, rk_ref, wk_ref, cnt_ref):
    c = comb_ref[...]
    n = c.shape[1]
    chosen = c >= 0.0
    ch = chosen.astype(BF16)
    tri = (lax.broadcasted_iota(jnp.int32, (n, n), 0) <= lax.broadcasted_iota(jnp.int32, (n, n), 1))
    incl = _dot(ch, tri.astype(BF16))
    rank = incl - ch.astype(F32)
    cnt_ref[...] = jnp.broadcast_to(incl[:, n - 1:n], cnt_ref.shape)
    low = (lax.broadcasted_iota(jnp.int32, (N_EXPERTS, N_EXPERTS), 1)
           < lax.broadcasted_iota(jnp.int32, (N_EXPERTS, N_EXPERTS), 0))
    slot = _dot(low.astype(BF16), ch)
    ioe = lax.broadcasted_iota(jnp.int32, c.shape, 0).astype(F32)
    for k in range(TOP_K):
        sel = jnp.logical_and(chosen, slot == float(k))
        ek_ref[k:k + 1, :] = jnp.sum(jnp.where(sel, ioe, 0.0), axis=0, keepdims=True).astype(jnp.int32)
        rk_ref[k:k + 1, :] = jnp.sum(jnp.where(sel, rank, 0.0), axis=0, keepdims=True).astype(jnp.int32)
        wk_ref[k:k + 1, :] = jnp.sum(jnp.where(sel, c, 0.0), axis=0, keepdims=True)


def _moe_plan(comb):
    n = comb.shape[1]
    nsb = n // MOE_TSB
    out_i = jax.ShapeDtypeStruct((TOP_K, n), jnp.int32)
    pair_spec = pl.BlockSpec((TOP_K, MOE_TSB), lambda s: (0, s))
    return pl.pallas_call(
        _moe_plan_kernel,
        out_shape=(out_i, out_i, jax.ShapeDtypeStruct((TOP_K, n), F32),
                   jax.ShapeDtypeStruct((N_EXPERTS, nsb * 128), F32)),
        grid=(nsb,),
        in_specs=[pl.BlockSpec((N_EXPERTS, MOE_TSB), lambda s: (0, s))],
        out_specs=(pair_spec, pair_spec, pair_spec, pl.BlockSpec((N_EXPERTS, 128), lambda s: (0, s))),
        compiler_params=_params(("arbitrary",)),
        name="moe_plan",
    )(comb)


def _moe_num_tiles(n_tok):
    rows = n_tok * TOP_K + N_EXPERTS * (n_tok // MOE_TSB) * (MOE_CHUNK_ALIGN - 1)
    return rows // MOE_TM + N_EXPERTS + 1


def _ceil_to(x, m):
    return (x + m - 1) // m * m


def _moe_layout(ek, rk, wk, cnt):
    n = ek.shape[1]
    nsb = n // MOE_TSB
    nt = _moe_num_tiles(n)
    n_es = cnt[:, ::128].astype(jnp.int32)
    c_al = _ceil_to(n_es, MOE_CHUNK_ALIGN)
    tiles_e = _ceil_to(jnp.sum(c_al, axis=1), MOE_TM) // MOE_TM
    tile0 = jnp.cumsum(tiles_e) - tiles_e
    cstart = tile0[:, None] * MOE_TM + jnp.cumsum(c_al, axis=1) - c_al
    npiece = _ceil_to(n_es, MOE_PIECE) // MOE_PIECE
    lbase = (jnp.cumsum(npiece, axis=0) - npiece) * MOE_PIECE
    dest, ldest = _moe_dest(ek, rk, cstart, lbase)
    dest_blocks = dest.reshape(TOP_K, nsb, MOE_TSB).transpose(1, 0, 2).reshape(-1)
    row_token = _moe_invert(dest_blocks, nt * MOE_TM)
    tile_expert = jnp.sum(jnp.arange(nt, dtype=jnp.int32)[:, None] >= tile0[None, :], axis=1) - 1
    return dict(row_token=row_token,
                tile_expert=tile_expert.astype(jnp.int32), n_used=jnp.sum(tiles_e).reshape(1).astype(jnp.int32),
                cstart=cstart.T.reshape(-1), npiece=npiece.T.reshape(-1), lbase=lbase.T.reshape(-1),
                ldest=ldest.T.reshape(-1), wk=wk.T.reshape(-1))


def _moe_dest_kernel(ek_ref, rk_ref, cs_ref, lb_ref, dest_ref, ldest_ref):
    ioe = lax.broadcasted_iota(jnp.int32, (N_EXPERTS, ek_ref.shape[1]), 0)
    cs = cs_ref[:, 0:1]
    lb = lb_ref[:, 0:1]
    for k in range(TOP_K):
        hit = ioe == ek_ref[k:k + 1, :]
        rk = rk_ref[k:k + 1, :]
        dest_ref[k:k + 1, :] = jnp.sum(jnp.where(hit, cs, 0.0), axis=0, keepdims=True).astype(jnp.int32) + rk
        ldest_ref[k:k + 1, :] = jnp.sum(jnp.where(hit, lb, 0.0), axis=0, keepdims=True).astype(jnp.int32) + rk


def _moe_dest(ek, rk, cstart, lbase):
    n = ek.shape[1]
    pair_spec = pl.BlockSpec((TOP_K, MOE_TSB), lambda s: (0, s))
    table_spec = pl.BlockSpec((N_EXPERTS, LANES), lambda s: (0, s))
    out = jax.ShapeDtypeStruct((TOP_K, n), jnp.int32)
    spread = lambda tab: jnp.repeat(tab.astype(F32), LANES, axis=1)
    return pl.pallas_call(
        _moe_dest_kernel,
        out_shape=(out, out),
        grid=(n // MOE_TSB,),
        in_specs=[pair_spec, pair_spec, table_spec, table_spec],
        out_specs=(pair_spec, pair_spec),
        compiler_params=_params(("arbitrary",)),
        name="moe_dest",
    )(ek, rk, spread(cstart), spread(lbase))


def _moe_invert_kernel(dest_hbm, zeros_hbm, rt_ref, dsm, sem):
    s = pl.program_id(0)
    pairs = TOP_K * MOE_TSB

    @pl.when(s == 0)
    def _():
        fill = pltpu.make_async_copy(zeros_hbm, rt_ref, sem)
        fill.start()
        fill.wait()

    copy = pltpu.make_async_copy(dest_hbm.at[pl.ds(pl.multiple_of(s * pairs, pairs), pairs)], dsm, sem)
    copy.start()
    copy.wait()

    def per_token(t, carry):
        for k in range(TOP_K):
            rt_ref[dsm[k * MOE_TSB + t]] = s * MOE_TSB + t
        return carry

    lax.fori_loop(0, MOE_TSB, per_token, 0, unroll=4)


def _moe_invert(dest_blocks, n_rows):
    pairs = TOP_K * MOE_TSB
    return pl.pallas_call(
        _moe_invert_kernel,
        out_shape=jax.ShapeDtypeStruct((n_rows,), jnp.int32),
        grid=(dest_blocks.shape[0] // pairs,),
        in_specs=[pl.BlockSpec(memory_space=pl.ANY), pl.BlockSpec(memory_space=pl.ANY)],
        out_specs=pl.BlockSpec(memory_space=pltpu.SMEM),
        scratch_shapes=[pltpu.SMEM((pairs,), jnp.int32), pltpu.SemaphoreType.DMA],
        compiler_params=_params(("arbitrary",)),
        name="moe_invert",
    )(dest_blocks, jnp.zeros((n_rows,), jnp.int32))


def _moe_expert_kernel(texp_ref, nused_ref, ids_ref, xp_ref, wg_ref, wu_ref, wd_ref, ys_ref,
                       xs, tmp, wgb, wub, wdb):
    i = pl.program_id(0)

    def gather(tile, slot):
        for r in range(MOE_TM):
            xs[slot, r * TOK_SUB:(r + 1) * TOK_SUB, :] = _load_token(xp_ref, ids_ref[tile * MOE_TM + r])

    @pl.when(i == 0)
    def _():
        gather(0, 0)

    @pl.when(i < nused_ref[0])
    def _():
        @pl.when(jnp.logical_or(i == 0, texp_ref[i] != texp_ref[jnp.maximum(i - 1, 0)]))
        def _():
            wgb[...] = wg_ref[...].astype(BF16)
            wub[...] = wu_ref[...].astype(BF16)
            wdb[...] = wd_ref[...].astype(BF16)

        slot = i % 2
        lhs = _tiles_to_rows(xs.at[slot], MOE_TM).astype(BF16)
        gather(i + 1, 1 - slot)
        act = jax.nn.silu(_dot(lhs, wgb[...])) * _dot(lhs, wub[...])
        ys_ref[...] = _rows_to_tiles(_dot(act.astype(BF16), wdb[...]), tmp)

    @pl.when(i >= nused_ref[0])
    def _():
        ys_ref[...] = jnp.zeros(ys_ref.shape, BF16)


def _moe_experts(plan, xp, w_gate, w_up, w_down, layer):
    nt = plan["tile_expert"].shape[0]
    grid_spec = pltpu.PrefetchScalarGridSpec(
        num_scalar_prefetch=3,
        grid=(nt,),
        in_specs=[
            pl.BlockSpec(memory_space=pltpu.VMEM),
            pl.BlockSpec((None, None, D_MODEL, D_EXPERT), lambda i, te, nu, ids: (layer, te[i], 0, 0)),
            pl.BlockSpec((None, None, D_MODEL, D_EXPERT), lambda i, te, nu, ids: (layer, te[i], 0, 0)),
            pl.BlockSpec((None, None, D_EXPERT, D_MODEL), lambda i, te, nu, ids: (layer, te[i], 0, 0)),
        ],
        out_specs=pl.BlockSpec((MOE_TM // 2, 2 * TOK_SUB, LANES), lambda i, te, nu, ids: (i, 0, 0)),
        scratch_shapes=[pltpu.VMEM((2, MOE_TM * TOK_SUB, LANES), F32),
                        pltpu.VMEM((MOE_TM * TOK_SUB, LANES), F32),
                        pltpu.VMEM((D_MODEL, D_EXPERT), BF16),
                        pltpu.VMEM((D_MODEL, D_EXPERT), BF16),
                        pltpu.VMEM((D_EXPERT, D_MODEL), BF16)],
    )
    return pl.pallas_call(
        _moe_expert_kernel,
        out_shape=jax.ShapeDtypeStruct((nt * MOE_TM // 2, 2 * TOK_SUB, LANES), BF16),
        grid_spec=grid_spec,
        compiler_params=_params(("arbitrary",)),
        name="moe_experts",
    )(plan["tile_expert"], plan["n_used"], plan["row_token"], xp, w_gate, w_up, w_down)


def _moe_combine_kernel(cstart_ref, npiece_ref, lbase_ref, ldest_ref, wk_ref, ys_hbm, xp_ref, sg_ref, su_ref,
                        sd_ref, x_ref, mod_ref, g_ref, o_ref, ysb, ybuf, npending, sems, *, block0):
    step = pl.program_id(0)
    slot = step % 2

    def piece_copy(src_row, dst_row, buf):
        return pltpu.make_async_copy(ys_hbm.at[pl.ds(src_row // 2, MOE_PIECE // 2)],
                                     ysb.at[buf, pl.ds(dst_row // 2, MOE_PIECE // 2)], sems.at[buf])

    def fetch(blk, buf):
        def per_expert(e, total):
            j = blk * N_EXPERTS + e
            src, dst, npc = cstart_ref[j], lbase_ref[j], npiece_ref[j]

            def per_piece(p, carry):
                piece_copy(pl.multiple_of(src + p * MOE_PIECE, MOE_CHUNK_ALIGN),
                           pl.multiple_of(dst + p * MOE_PIECE, MOE_PIECE), buf).start()
                return carry

            lax.fori_loop(0, npc, per_piece, 0)
            return total + npc

        npending[buf] = lax.fori_loop(0, N_EXPERTS, per_expert, 0)

    @pl.when(step == 0)
    def _():
        fetch(block0, 0)

    @pl.when(step + 1 < pl.num_programs(0))
    def _():
        fetch(block0 + step + 1, 1 - slot)

    ybuf[...] = xp_ref[...].astype(F32).reshape(MOE_TSB * TOK_SUB, LANES)
    lhs = _tiles_to_rows(ybuf, MOE_TSB).astype(BF16)
    act = jax.nn.silu(_dot(lhs, sg_ref[...].astype(BF16))) * _dot(lhs, su_ref[...].astype(BF16))
    o_ref[...] = _dot(act.astype(BF16), sd_ref[...].astype(BF16))

    def wait_one(p, carry):
        piece_copy(0, 0, slot).wait()
        return carry

    lax.fori_loop(0, npending[slot], wait_one, 0)

    t0 = step * MOE_TSB
    yblk = ysb.at[slot]

    def per_token(t, carry):
        p = (t0 + t) * TOP_K
        acc = wk_ref[p] * _load_token(yblk, ldest_ref[p])
        for k in range(1, TOP_K):
            acc = acc + wk_ref[p + k] * _load_token(yblk, ldest_ref[p + k])
        ybuf[pl.ds(pl.multiple_of(t * TOK_SUB, TOK_SUB), TOK_SUB), :] = acc
        return carry

    lax.fori_loop(0, MOE_TSB, per_token, 0, unroll=4)
    routed = _tiles_to_rows(ybuf, MOE_TSB)
    o_ref[...] = x_ref[...] + mod_ref[5:6, :] * _rms(routed + o_ref[...], g_ref[3:4, :])


def _moe_combine(plan, ldest, wk, block0, ys, xp, s_gate, s_up, s_down, layer, x1, mod, g, rows_per_seq,
                 first_row):
    n = x1.shape[0]
    cap = TOP_K * MOE_TSB + N_EXPERTS * MOE_PIECE
    grid_spec = pltpu.PrefetchScalarGridSpec(
        num_scalar_prefetch=5,
        grid=(n // MOE_TSB,),
        in_specs=[
            pl.BlockSpec(memory_space=pl.ANY),
            pl.BlockSpec((MOE_TSB // 2, 2 * TOK_SUB, LANES), lambda s, *_: (s, 0, 0)),
            pl.BlockSpec((None, D_MODEL, D_SHARED), lambda s, *_: (layer, 0, 0)),
            pl.BlockSpec((None, D_MODEL, D_SHARED), lambda s, *_: (layer, 0, 0)),
            pl.BlockSpec((None, D_SHARED, D_MODEL), lambda s, *_: (layer, 0, 0)),
            pl.BlockSpec((MOE_TSB, D_MODEL), lambda s, *_: (s, 0)),
            _mod_spec(MOE_TSB, rows_per_seq, first_row),
            pl.BlockSpec((4, D_MODEL), lambda s, *_: (0, 0)),
        ],
        out_specs=pl.BlockSpec((MOE_TSB, D_MODEL), lambda s, *_: (s, 0)),
        scratch_shapes=[pltpu.VMEM((2, cap // 2, 2 * TOK_SUB, LANES), BF16),
                        pltpu.VMEM((MOE_TSB * TOK_SUB, LANES), F32),
                        pltpu.SMEM((2,), jnp.int32),
                        pltpu.SemaphoreType.DMA((2,))],
    )
    return pl.pallas_call(
        functools.partial(_moe_combine_kernel, block0=block0),
        out_shape=jax.ShapeDtypeStruct((n, D_MODEL), F32),
        grid_spec=grid_spec,
        compiler_params=_params(("arbitrary",)),
        name="moe_combine",
    )(plan["cstart"], plan["npiece"], plan["lbase"], ldest, wk, ys, xp, s_gate, s_up, s_down, x1, mod, g)


def _rope(x, cos, sin_up, sin_dn):
    out = []
    for h in range(N_HEADS):
        xs = x[:, h * V_DIM:(h + 1) * V_DIM]
        up = pltpu.roll(xs, V_DIM - ROPE_PAIRS, 1)
        dn = pltpu.roll(xs, ROPE_PAIRS, 1)
        out.append(xs * cos + up * sin_up + dn * sin_dn)
    return jnp.concatenate(out, axis=1)


def _qkv_kernel(x_ref, mod_ref, g_ref, w_ref, *rest, rope):
    if rope:
        cos_ref, sup_ref, sdn_ref, q_ref, k_ref, v_ref = rest
    else:
        q_ref, k_ref, v_ref = rest
    h = _mod_in(x_ref[...], g_ref[...], mod_ref[0:1, :], mod_ref[1:2, :])
    qkv = _dot(h.astype(BF16), w_ref[...])
    q, k, v = qkv[:, :QK_W], qkv[:, QK_W:2 * QK_W], qkv[:, 2 * QK_W:]
    if rope:
        q = _rope(q, cos_ref[...], sup_ref[...], sdn_ref[...])
        k = _rope(k, cos_ref[...], sup_ref[...], sdn_ref[...])
    q_ref[...] = (q * HEAD_DIM ** -0.5).astype(q_ref.dtype)
    k_ref[...] = k.astype(k_ref.dtype)
    v_ref[...] = v.astype(v_ref.dtype)


def _qkv(x, mod, g, w_qkv, rows_per_seq, first_row, rope_tables, kv_dtype):
    n = x.shape[0]
    tm = 512
    rope = rope_tables is not None
    in_specs = [
        pl.BlockSpec((tm, D_MODEL), lambda i: (i, 0)),
        _mod_spec(tm, rows_per_seq, first_row),
        pl.BlockSpec((1, D_MODEL), lambda i: (0, 0)),
        pl.BlockSpec((D_MODEL, 3 * QK_W), lambda i: (0, 0)),
    ]
    args = [x, mod, g, w_qkv]
    if rope:
        tiles_per_seq = rows_per_seq // tm
        in_specs += [pl.BlockSpec((tm, V_DIM), lambda i: (i % tiles_per_seq, 0))] * 3
        args += list(rope_tables)
    return pl.pallas_call(
        functools.partial(_qkv_kernel, rope=rope),
        out_shape=(jax.ShapeDtypeStruct((n, QK_W), BF16),
                   jax.ShapeDtypeStruct((n, QK_W), kv_dtype),
                   jax.ShapeDtypeStruct((n, N_HEADS * V_DIM), kv_dtype)),
        grid=(n // tm,),
        in_specs=in_specs,
        out_specs=(pl.BlockSpec((tm, QK_W), lambda i: (i, 0)),) * 3,
        compiler_params=_params(("arbitrary",)),
        name="attn_qkv_rope" if rope else "attn_qkv",
    )(*args)


def _softmax_parts(s):
    e = jnp.exp(s - jnp.max(s, axis=-1, keepdims=True))
    return e, 1.0 / jnp.sum(e, axis=-1, keepdims=True)


def _attn_kernel(lp_ref, sub_ref, q_ref, k_ref, v_ref, o_ref, *, lam_init):
    lp = lp_ref[...]
    lam = (jnp.exp(jnp.sum(lp[0:1, :] * lp[1:2, :], axis=1, keepdims=True))
           - jnp.exp(jnp.sum(lp[2:3, :] * lp[3:4, :], axis=1, keepdims=True)) + lam_init)
    for h in range(N_HEADS):
        c0 = h * V_DIM
        q = q_ref[:, c0:c0 + V_DIM]
        k = k_ref[:, c0:c0 + V_DIM].astype(BF16)
        e1, r1 = _softmax_parts(_dot_nt(q[:, :HEAD_DIM], k[:, :HEAD_DIM]))
        e2, r2 = _softmax_parts(_dot_nt(q[:, HEAD_DIM:], k[:, HEAD_DIM:]))
        w = e1 * r1 - lam * (e2 * r2)
        o = _dot(w.astype(BF16), v_ref[:, c0:c0 + V_DIM].astype(BF16))
        o_ref[:, c0:c0 + V_DIM] = (_rms(o, sub_ref[...]) * (1.0 - lam_init)).astype(BF16)


def _attention(lp, subln, q, k, v, lam_init, tq):
    bsz, t = q.shape[:2]
    tk = k.shape[1]
    return pl.pallas_call(
        functools.partial(_attn_kernel, lam_init=lam_init),
        out_shape=jax.ShapeDtypeStruct((bsz, t, N_HEADS * V_DIM), BF16),
        grid=(bsz, t // tq),
        in_specs=[
            pl.BlockSpec((4, HEAD_DIM), lambda b, i: (0, 0)),
            pl.BlockSpec((1, V_DIM), lambda b, i: (0, 0)),
            pl.BlockSpec((None, tq, QK_W), lambda b, i: (b, i, 0)),
            pl.BlockSpec((None, tk, QK_W), lambda b, i: (b, 0, 0)),
            pl.BlockSpec((None, tk, N_HEADS * V_DIM), lambda b, i: (b, 0, 0)),
        ],
        out_specs=pl.BlockSpec((None, tq, N_HEADS * V_DIM), lambda b, i: (b, i, 0)),
        compiler_params=_params(("arbitrary", "arbitrary")),
        name="diff_attention",
    )(lp, subln, q, k, v)


def _lru_gate_layout(gate_w, gate_b):
    nb = LRU_BLOCKS // 2
    w = gate_w.reshape(2, 2, 2, nb, LRU_BLOCK, LRU_BLOCK)
    bd = jnp.einsum("dgsbio,bc->sbidgco", w, jnp.eye(nb, dtype=w.dtype))
    bd = bd.reshape(2, LRU_HALF, 4 * LRU_HALF).astype(BF16)
    b = gate_b.reshape(2, 2, 2, LRU_HALF).transpose(2, 0, 1, 3).reshape(2, 1, 4 * LRU_HALF)
    return bd, b


def _rope_tables(n):
    rows = n // GRID_W
    row = jnp.repeat(jnp.arange(rows, dtype=F32), GRID_W)
    col = jnp.tile(jnp.arange(GRID_W, dtype=F32), rows)
    freqs = ROPE_THETA ** (-jnp.arange(ROPE_PAIRS, dtype=F32) / ROPE_PAIRS)
    ar = row[:, None] * freqs
    ac = col[:, None] * freqs
    zero = jnp.zeros_like(ar)
    cos = jnp.concatenate([jnp.cos(ar), jnp.cos(ar), jnp.cos(ac), jnp.cos(ac)], axis=-1)
    sin_up = jnp.concatenate([-jnp.sin(ar), zero, -jnp.sin(ac), zero], axis=-1)
    sin_dn = jnp.concatenate([zero, jnp.sin(ar), zero, jnp.sin(ac)], axis=-1)
    return tuple(jnp.tile(t, (1, 2)) for t in (cos, sin_up, sin_dn))


def kernel(x_prompt, x_sample, c, state_lru, cache_k, cache_v, c_ctx, mod_w, mod_b, norm_g,
           lru_w_in, lru_conv_w, lru_conv_b, lru_gate_w, lru_gate_b, lru_lambda, lru_w_out,
           attn_w_qkv, attn_lambda, attn_subln, attn_w_o,
           moe_router, moe_router_bias, moe_w_gate, moe_w_up, moe_w_down,
           shared_w_gate, shared_w_up, shared_w_down):
    bp, tp = x_prompt.shape[:2]
    bs, ts = x_sample.shape[:2]
    past = cache_k.shape[2]
    cond = jnp.concatenate([c_ctx[None], c, jnp.zeros((MOD_ROWS - 1 - bs, D_MODEL), F32)], axis=0)
    mod = _modulation(cond, mod_w, mod_b)
    streams = [dict(x=x_prompt.reshape(bp * tp, D_MODEL), rps=0, row=0, b=bp, t=tp),
               dict(x=x_sample.reshape(bs * ts, D_MODEL), rps=ts, row=1, b=bs, t=ts)]
    new_lru, new_k, new_v = [], [], []
    for i in range(DEPTH):
        j = i // N_MIXERS
        g = norm_g[i]
        router_t = moe_router[i].T
        router_b = moe_router_bias[i].reshape(N_EXPERTS, 1)
        if i % N_MIXERS == 0:
            w_in = lru_w_in[j].astype(BF16)
            w_mix = lru_w_out[j].astype(BF16)
            gate_w, gate_b = _lru_gate_layout(lru_gate_w[j], lru_gate_b[j])
        else:
            lam_init = 0.8 - 0.6 * math.exp(-0.3 * i)
            w_qkv = attn_w_qkv[j].astype(BF16)
            w_mix = attn_w_o[j].astype(BF16)
            tables = _rope_tables(ts)
        for si, s in enumerate(streams):
            if i % N_MIXERS == 0:
                gate, xr = _lru_in(s["x"], mod[i], g[0:1], w_in, s["rps"], s["row"])
                h0 = jnp.zeros((bp, 2, D_RNN), F32) if si == 0 else state_lru[:, j]
                m, h_last = _lru_core(xr, gate, h0, lru_conv_w[j], lru_conv_b[j][None], gate_w, gate_b,
                                      lru_lambda[j], s["b"], s["t"])
                if si == 0:
                    new_lru.append(h_last)
            else:
                if si == 0:
                    q, k, v = _qkv(s["x"], mod[i], g[0:1], w_qkv, s["rps"], s["row"], None, F32)
                    new_k.append(k.reshape(bp, tp, N_HEADS, 2 * HEAD_DIM))
                    new_v.append(v.reshape(bp, tp, N_HEADS, V_DIM))
                    k3 = k.reshape(bp, tp, QK_W)
                    v3 = v.reshape(bp, tp, N_HEADS * V_DIM)
                    tq = tp
                else:
                    q, k, v = _qkv(s["x"], mod[i], g[0:1], w_qkv, s["rps"], s["row"], tables, BF16)
                    k3 = jnp.concatenate([cache_k[:, j].reshape(bs, past, QK_W).astype(BF16),
                                          k.reshape(bs, ts, QK_W)], axis=1)
                    v3 = jnp.concatenate([cache_v[:, j].reshape(bs, past, N_HEADS * V_DIM).astype(BF16),
                                          v.reshape(bs, ts, N_HEADS * V_DIM)], axis=1)
                    tq = 512
                o = _attention(attn_lambda[j], attn_subln[j][None], q.reshape(s["b"], s["t"], QK_W),
                               k3, v3, lam_init, tq)
                m = o.reshape(s["b"] * s["t"], N_HEADS * V_DIM)
            s["x1"], s["xp"], s["comb"] = _mix_out(m, w_mix, s["x"], mod[i], g, router_t, router_b,
                                                   s["rps"], s["row"])
        xp = jnp.concatenate([s["xp"] for s in streams], axis=0)
        ek, rk, wk, cnt = _moe_plan(jnp.concatenate([s["comb"] for s in streams], axis=1))
        plan = _moe_layout(ek, rk, wk, cnt)
        ys = _moe_experts(plan, xp, moe_w_gate, moe_w_up, moe_w_down, i)
        tok0 = 0
        for s in streams:
            n = s["x1"].shape[0]
            pairs = slice(tok0 * TOP_K, (tok0 + n) * TOP_K)
            s["x"] = _moe_combine(plan, plan["ldest"][pairs], plan["wk"][pairs], tok0 // MOE_TSB, ys, s["xp"],
                                  shared_w_gate, shared_w_up,
                                  shared_w_down, i, s["x1"], mod[i], g, s["rps"], s["row"])
            tok0 += n
    return (streams[0]["x"].reshape(bp, tp, D_MODEL),
            streams[1]["x"].reshape(bs, ts, D_MODEL),
            jnp.stack(new_lru, axis=1),
            jnp.stack(new_k, axis=1),
            jnp.stack(new_v, axis=1))
```

```python
import functools
import math

import jax
import jax.numpy as jnp
from jax import lax
from jax.experimental import pallas as pl
from jax.experimental.pallas import tpu as pltpu

D_MODEL = 1024
DEPTH = 2
N_MIXERS = 2
GRID_W = 64
EPS = 1e-6
D_RNN = 1280
LRU_BLOCKS = 16
LRU_BLOCK = D_RNN // LRU_BLOCKS
CONV_W = 4
CONV_LEFT = 2
LRU_C = 8.0
N_HEADS = 8
HEAD_DIM = 64
V_DIM = 2 * HEAD_DIM
QK_W = N_HEADS * 2 * HEAD_DIM
ROPE_PAIRS = HEAD_DIM // 4
ROPE_THETA = 10000.0
N_EXPERTS = 64
N_GROUPS = 8
GROUP_SIZE = N_EXPERTS // N_GROUPS
TOPK_GROUPS = 4
TOP_K = 8
D_EXPERT = 256
D_SHARED = 256
ROUTED_SCALE = 2.5

MOD_ROWS = 8
LRU_HALF = D_RNN // 2
CONV_PAD = 8
MOE_TM = 512
MOE_TSB = 512
MOE_CHUNK_ALIGN = 8
MOE_PIECE = 32
LANES = 128
TOK_SUB = D_MODEL // LANES
VMEM_LIMIT = 56 * 1024 * 1024
BF16 = jnp.bfloat16
F32 = jnp.float32
NEG_INF = float("-inf")


def _params(sem):
    return pltpu.CompilerParams(dimension_semantics=sem, vmem_limit_bytes=VMEM_LIMIT)


def _rms(x, g):
    return x * lax.rsqrt(jnp.mean(x * x, axis=-1, keepdims=True) + EPS) * g


def _mod_in(x, g, shift, scale):
    return _rms(x, g) * (1.0 + scale) + shift


def _dot(a, b):
    return jnp.dot(a, b, preferred_element_type=F32)


def _dot_nt(a, b, precision=None):
    return lax.dot_general(a, b, (((1,), (1,)), ((), ())), precision=precision,
                           preferred_element_type=F32)


def _mod_kernel(cond_ref, w_ref, b_ref, o_ref):
    cnd = cond_ref[...]
    s = cnd * jax.nn.sigmoid(cnd)
    o_ref[...] = jnp.dot(s, w_ref[...], precision=lax.Precision.HIGHEST,
                         preferred_element_type=F32) + b_ref[...]


def _modulation(cond, mod_w, mod_b):
    tn = 1536
    out = pl.pallas_call(
        _mod_kernel,
        out_shape=jax.ShapeDtypeStruct((DEPTH, MOD_ROWS, 6 * D_MODEL), F32),
        grid=(DEPTH, 6 * D_MODEL // tn),
        in_specs=[
            pl.BlockSpec((MOD_ROWS, D_MODEL), lambda l, n: (0, 0)),
            pl.BlockSpec((None, D_MODEL, tn), lambda l, n: (l, 0, n)),
            pl.BlockSpec((None, 1, tn), lambda l, n: (l, 0, n)),
        ],
        out_specs=pl.BlockSpec((None, MOD_ROWS, tn), lambda l, n: (l, 0, n)),
        compiler_params=_params(("arbitrary", "arbitrary")),
        name="modulation",
    )(cond, mod_w, mod_b.reshape(DEPTH, 1, 6 * D_MODEL))
    return out.reshape(DEPTH, MOD_ROWS, 6, D_MODEL)


def _mod_spec(tm, rows_per_seq, first_row):
    def index(i, *_):
        return (first_row + (i * tm) // rows_per_seq if rows_per_seq else first_row, 0, 0)
    return pl.BlockSpec((None, 6, D_MODEL), index)


def _lru_in_kernel(x_ref, mod_ref, g_ref, w_ref, gate_ref, xr_ref):
    h = _mod_in(x_ref[...], g_ref[...], mod_ref[0:1, :], mod_ref[1:2, :])
    xb = _dot(h.astype(BF16), w_ref[...])
    gate_ref[...] = jax.nn.gelu(xb[:, :D_RNN]).astype(BF16)
    xr_ref[...] = xb[:, D_RNN:]


def _lru_in(x, mod, g, w_in, rows_per_seq, first_row):
    n = x.shape[0]
    tm = 512
    return pl.pallas_call(
        _lru_in_kernel,
        out_shape=(jax.ShapeDtypeStruct((n, D_RNN), BF16), jax.ShapeDtypeStruct((n, D_RNN), F32)),
        grid=(n // tm,),
        in_specs=[
            pl.BlockSpec((tm, D_MODEL), lambda i: (i, 0)),
            _mod_spec(tm, rows_per_seq, first_row),
            pl.BlockSpec((1, D_MODEL), lambda i: (0, 0)),
            pl.BlockSpec((D_MODEL, 2 * D_RNN), lambda i: (0, 0)),
        ],
        out_specs=(pl.BlockSpec((tm, D_RNN), lambda i: (i, 0)),
                   pl.BlockSpec((tm, D_RNN), lambda i: (i, 0))),
        compiler_params=_params(("arbitrary",)),
        name="lru_in",
    )(x, mod, g, w_in)


def _lru_core_kernel(xr_ref, gate_ref, h0_ref, cw_ref, cb_ref, gw_ref, gb_ref, lam_ref,
                     m_ref, hl_ref, xpad, a_f, u_f, a_b, u_b, *, seq, chunk):
    c = LRU_HALF
    xpad[0:CONV_PAD, :] = jnp.zeros((CONV_PAD, c), F32)
    xpad[CONV_PAD + seq:, :] = jnp.zeros((CONV_PAD, c), F32)
    xpad[CONV_PAD:CONV_PAD + seq, :] = xr_ref[...]
    lam = lam_ref[...]
    coef = -LRU_C * jax.nn.softplus(-lam)
    for r0 in range(0, seq, chunk):
        xr = cb_ref[...]
        for j in range(CONV_W):
            off = CONV_PAD - CONV_LEFT + j + r0
            xr = xr + xpad[off:off + chunk, :] * cw_ref[j:j + 1, :]
        g = _dot(xr.astype(BF16), gw_ref[...]) + gb_ref[...]
        for d, (a_s, u_s) in enumerate(((a_f, u_f), (a_b, u_b))):
            r = 0.5 * (jnp.tanh(0.5 * g[:, 2 * d * c:(2 * d + 1) * c]) + 1.0)
            i = 0.5 * (jnp.tanh(0.5 * g[:, (2 * d + 1) * c:(2 * d + 2) * c]) + 1.0)
            log_a = coef[d:d + 1, :] * r
            a = jnp.exp(log_a)
            a_s[r0:r0 + chunk, :] = a
            u_s[r0:r0 + chunk, :] = jnp.sqrt(-jnp.tanh(log_a) * (a * a + 1.0)) * (i * xr)

    def step(t, carry):
        hf, hb = carry
        tb = seq - 1 - t
        hf = a_f[pl.ds(t, 1), :] * hf + u_f[pl.ds(t, 1), :]
        u_f[pl.ds(t, 1), :] = hf
        hb = a_b[pl.ds(tb, 1), :] * hb + u_b[pl.ds(tb, 1), :]
        u_b[pl.ds(tb, 1), :] = hb
        return hf, hb

    hf, hb = lax.fori_loop(0, seq, step, (h0_ref[0:1, :], h0_ref[1:2, :]), unroll=8)
    hl_ref[0:1, :] = hf
    hl_ref[1:2, :] = hb
    m_ref[...] = ((u_f[...] + u_b[...]) * gate_ref[...].astype(F32)).astype(BF16)


def _lru_core(xr, gate, h0, conv_w, conv_b, gate_w, gate_b, lam, n_seq, seq):
    c = LRU_HALF
    chunk = min(seq, 256)
    kernel = functools.partial(_lru_core_kernel, seq=seq, chunk=chunk)
    return pl.pallas_call(
        kernel,
        out_shape=(jax.ShapeDtypeStruct((n_seq * seq, D_RNN), BF16),
                   jax.ShapeDtypeStruct((n_seq, 2, D_RNN), F32)),
        grid=(n_seq, 2),
        in_specs=[
            pl.BlockSpec((seq, c), lambda b, j: (b, j)),
            pl.BlockSpec((seq, c), lambda b, j: (b, j)),
            pl.BlockSpec((None, 2, c), lambda b, j: (b, 0, j)),
            pl.BlockSpec((CONV_W, c), lambda b, j: (0, j)),
            pl.BlockSpec((1, c), lambda b, j: (0, j)),
            pl.BlockSpec((None, c, 4 * c), lambda b, j: (j, 0, 0)),
            pl.BlockSpec((None, 1, 4 * c), lambda b, j: (j, 0, 0)),
            pl.BlockSpec((2, c), lambda b, j: (0, j)),
        ],
        out_specs=(pl.BlockSpec((seq, c), lambda b, j: (b, j)),
                   pl.BlockSpec((None, 2, c), lambda b, j: (b, 0, j))),
        scratch_shapes=[pltpu.VMEM((seq + 2 * CONV_PAD, c), F32)] + [pltpu.VMEM((seq, c), F32)] * 4,
        compiler_params=_params(("arbitrary", "arbitrary")),
        name="lru_core",
    )(xr, gate, h0, conv_w, conv_b, gate_w, gate_b, lam)


def _route(sel, scores):
    tm = sel.shape[1]
    io8 = lax.broadcasted_iota(jnp.int32, (GROUP_SIZE, tm), 0)
    blocks, gscore = [], []
    for g in range(N_GROUPS):
        blk = sel[g * GROUP_SIZE:(g + 1) * GROUP_SIZE, :]
        m1 = jnp.max(blk, axis=0, keepdims=True)
        first = jnp.min(jnp.where(blk == m1, io8, GROUP_SIZE), axis=0, keepdims=True)
        m2 = jnp.max(jnp.where(io8 == first, NEG_INF, blk), axis=0, keepdims=True)
        blocks.append(blk)
        gscore.append(m1 + m2)
    masked = []
    for g in range(N_GROUPS):
        rank = jnp.zeros((1, tm), jnp.int32)
        for o in range(N_GROUPS):
            if o == g:
                continue
            beats = (gscore[o] >= gscore[g]) if o < g else (gscore[o] > gscore[g])
            rank = rank + beats.astype(jnp.int32)
        masked.append(jnp.where(rank < TOPK_GROUPS, blocks[g], NEG_INF))
    v = jnp.concatenate(masked, axis=0)
    ioe = lax.broadcasted_iota(jnp.int32, (N_EXPERTS, tm), 0)
    chosen = jnp.zeros((N_EXPERTS, tm), F32)
    for _ in range(TOP_K):
        mx = jnp.max(v, axis=0, keepdims=True)
        first = jnp.min(jnp.where(v == mx, ioe, N_EXPERTS), axis=0, keepdims=True)
        pick = ioe == first
        chosen = jnp.where(pick, 1.0, chosen)
        v = jnp.where(pick, NEG_INF, v)
    wsel = chosen * scores
    comb = wsel / jnp.sum(wsel, axis=0, keepdims=True) * ROUTED_SCALE
    return jnp.where(chosen > 0.0, comb, -1.0)


def _rows_to_tiles(x, tmp):
    tm = x.shape[0]
    for c in range(TOK_SUB):
        tmp[pl.ds(c, tm, stride=TOK_SUB), :] = x[:, c * LANES:(c + 1) * LANES]
    return tmp[...].reshape(tm // 2, 2 * TOK_SUB, LANES).astype(BF16)


def _tiles_to_rows(tmp, tm):
    return jnp.concatenate([tmp[pl.ds(c, tm, stride=TOK_SUB), :] for c in range(TOK_SUB)], axis=1)


def _load_token(pairs_ref, t):
    pair = pairs_ref[t >> 1].astype(F32)
    return jnp.where((t & 1) == 1, pair[TOK_SUB:], pair[:TOK_SUB])


def _mix_out_kernel(m_ref, w_ref, x_ref, mod_ref, g_ref, rt_ref, rb_ref, x1_ref, h2_ref, comb_ref, tmp):
    y = _dot(m_ref[...], w_ref[...])
    x1 = x_ref[...] + mod_ref[2:3, :] * _rms(y, g_ref[1:2, :])
    x1_ref[...] = x1
    h2 = _mod_in(x1, g_ref[2:3, :], mod_ref[3:4, :], mod_ref[4:5, :])
    h2_ref[...] = _rows_to_tiles(h2, tmp)
    logits = _dot_nt(rt_ref[...], h2, precision=lax.Precision.HIGHEST)
    scores = jax.nn.sigmoid(logits)
    comb_ref[...] = _route(scores + rb_ref[...], scores)


def _mix_out(m, w, x, mod, g, router_t, router_b, rows_per_seq, first_row):
    n, k = m.shape
    tm = 512
    return pl.pallas_call(
        _mix_out_kernel,
        out_shape=(jax.ShapeDtypeStruct((n, D_MODEL), F32),
                   jax.ShapeDtypeStruct((n // 2, 2 * TOK_SUB, LANES), BF16),
                   jax.ShapeDtypeStruct((N_EXPERTS, n), F32)),
        grid=(n // tm,),
        in_specs=[
            pl.BlockSpec((tm, k), lambda i: (i, 0)),
            pl.BlockSpec((k, D_MODEL), lambda i: (0, 0)),
            pl.BlockSpec((tm, D_MODEL), lambda i: (i, 0)),
            _mod_spec(tm, rows_per_seq, first_row),
            pl.BlockSpec((4, D_MODEL), lambda i: (0, 0)),
            pl.BlockSpec((N_EXPERTS, D_MODEL), lambda i: (0, 0)),
            pl.BlockSpec((N_EXPERTS, 1), lambda i: (0, 0)),
        ],
        out_specs=(pl.BlockSpec((tm, D_MODEL), lambda i: (i, 0)),
                   pl.BlockSpec((tm // 2, 2 * TOK_SUB, LANES), lambda i: (i, 0, 0)),
                   pl.BlockSpec((N_EXPERTS, tm), lambda i: (0, i))),
        scratch_shapes=[pltpu.VMEM((tm * TOK_SUB, LANES), F32)],
        compiler_params=_params(("arbitrary",)),
        name="mix_out",
    )(m, w, x, mod, g, router_t, router_b)


def _moe_plan_kernel(comb_ref, ek_ref, rk_ref, wk_ref, cnt_ref):
    c = comb_ref[...]
    n = c.shape[1]
    chosen = c >= 0.0
    ch = chosen.astype(BF16)
    tri = (lax.broadcasted_iota(jnp.int32, (n, n), 0) <= lax.broadcasted_iota(jnp.int32, (n, n), 1))
    incl = _dot(ch, tri.astype(BF16))
    rank = incl - ch.astype(F32)
    cnt_ref[...] = jnp.broadcast_to(incl[:, n - 1:n], cnt_ref.shape)
    low = (lax.broadcasted_iota(jnp.int32, (N_EXPERTS, N_EXPERTS), 1)
           < lax.broadcasted_iota(jnp.int32, (N_EXPERTS, N_EXPERTS), 0))
    slot = _dot(low.astype(BF16), ch)
    ioe = lax.broadcasted_iota(jnp.int32, c.shape, 0).astype(F32)
    for k in range(TOP_K):
        sel = jnp.logical_and(chosen, slot == float(k))
        ek_ref[k:k + 1, :] = jnp.sum(jnp.where(sel, ioe, 0.0), axis=0, keepdims=True).astype(jnp.int32)
        rk_ref[k:k + 1, :] = jnp.sum(jnp.where(sel, rank, 0.0), axis=0, keepdims=True).astype(jnp.int32)
        wk_ref[k:k + 1, :] = jnp.sum(jnp.where(sel, c, 0.0), axis=0, keepdims=True)


def _moe_plan(comb):
    n = comb.shape[1]
    nsb = n // MOE_TSB
    out_i = jax.ShapeDtypeStruct((TOP_K, n), jnp.int32)
    pair_spec = pl.BlockSpec((TOP_K, MOE_TSB), lambda s: (0, s))
    return pl.pallas_call(
        _moe_plan_kernel,
        out_shape=(out_i, out_i, jax.ShapeDtypeStruct((TOP_K, n), F32),
                   jax.ShapeDtypeStruct((N_EXPERTS, nsb * 128), F32)),
        grid=(nsb,),
        in_specs=[pl.BlockSpec((N_EXPERTS, MOE_TSB), lambda s: (0, s))],
        out_specs=(pair_spec, pair_spec, pair_spec, pl.BlockSpec((N_EXPERTS, 128), lambda s: (0, s))),
        compiler_params=_params(("arbitrary",)),
        name="moe_plan",
    )(comb)


def _moe_num_tiles(n_tok):
    rows = n_tok * TOP_K + N_EXPERTS * (n_tok // MOE_TSB) * (MOE_CHUNK_ALIGN - 1)
    return rows // MOE_TM + N_EXPERTS + 1


def _ceil_to(x, m):
    return (x + m - 1) // m * m


def _moe_layout(ek, rk, wk, cnt):
    n = ek.shape[1]
    nsb = n // MOE_TSB
    nt = _moe_num_tiles(n)
    n_es = cnt[:, ::128].astype(jnp.int32)
    c_al = _ceil_to(n_es, MOE_CHUNK_ALIGN)
    tiles_e = _ceil_to(jnp.sum(c_al, axis=1), MOE_TM) // MOE_TM
    tile0 = jnp.cumsum(tiles_e) - tiles_e
    cstart = tile0[:, None] * MOE_TM + jnp.cumsum(c_al, axis=1) - c_al
    npiece = _ceil_to(n_es, MOE_PIECE) // MOE_PIECE
    lbase = (jnp.cumsum(npiece, axis=0) - npiece) * MOE_PIECE
    dest, ldest = _moe_dest(ek, rk, cstart, lbase)
    dest_blocks = dest.reshape(TOP_K, nsb, MOE_TSB).transpose(1, 0, 2).reshape(-1)
    row_token = _moe_invert(dest_blocks, nt * MOE_TM)
    tile_expert = jnp.sum(jnp.arange(nt, dtype=jnp.int32)[:, None] >= tile0[None, :], axis=1) - 1
    wsigned = jnp.where((ldest & 1) == 1, -wk, wk)
    return dict(row_token=row_token,
                tile_expert=tile_expert.astype(jnp.int32), n_used=jnp.sum(tiles_e).reshape(1).astype(jnp.int32),
                cstart=cstart.T.reshape(-1), npiece=npiece.T.reshape(-1), lbase=lbase.T.reshape(-1),
                ltile=(ldest >> 1).T.reshape(-1), wk=wsigned.T.reshape(-1))


def _moe_dest_kernel(ek_ref, rk_ref, cs_ref, lb_ref, dest_ref, ldest_ref):
    ioe = lax.broadcasted_iota(jnp.int32, (N_EXPERTS, ek_ref.shape[1]), 0)
    cs = cs_ref[:, 0:1]
    lb = lb_ref[:, 0:1]
    for k in range(TOP_K):
        hit = ioe == ek_ref[k:k + 1, :]
        rk = rk_ref[k:k + 1, :]
        dest_ref[k:k + 1, :] = jnp.sum(jnp.where(hit, cs, 0.0), axis=0, keepdims=True).astype(jnp.int32) + rk
        ldest_ref[k:k + 1, :] = jnp.sum(jnp.where(hit, lb, 0.0), axis=0, keepdims=True).astype(jnp.int32) + rk


def _moe_dest(ek, rk, cstart, lbase):
    n = ek.shape[1]
    pair_spec = pl.BlockSpec((TOP_K, MOE_TSB), lambda s: (0, s))
    table_spec = pl.BlockSpec((N_EXPERTS, LANES), lambda s: (0, s))
    out = jax.ShapeDtypeStruct((TOP_K, n), jnp.int32)
    spread = lambda tab: jnp.repeat(tab.astype(F32), LANES, axis=1)
    return pl.pallas_call(
        _moe_dest_kernel,
        out_shape=(out, out),
        grid=(n // MOE_TSB,),
        in_specs=[pair_spec, pair_spec, table_spec, table_spec],
        out_specs=(pair_spec, pair_spec),
        compiler_params=_params(("arbitrary",)),
        name="moe_dest",
    )(ek, rk, spread(cstart), spread(lbase))


def _moe_invert_kernel(dest_hbm, zeros_hbm, rt_ref, dsm, sem):
    s = pl.program_id(0)
    pairs = TOP_K * MOE_TSB

    @pl.when(s == 0)
    def _():
        fill = pltpu.make_async_copy(zeros_hbm, rt_ref, sem)
        fill.start()
        fill.wait()

    copy = pltpu.make_async_copy(dest_hbm.at[pl.ds(pl.multiple_of(s * pairs, pairs), pairs)], dsm, sem)
    copy.start()
    copy.wait()

    def per_token(t, carry):
        for k in range(TOP_K):
            rt_ref[dsm[k * MOE_TSB + t]] = s * MOE_TSB + t
        return carry

    lax.fori_loop(0, MOE_TSB, per_token, 0, unroll=4)


def _moe_invert(dest_blocks, n_rows):
    pairs = TOP_K * MOE_TSB
    return pl.pallas_call(
        _moe_invert_kernel,
        out_shape=jax.ShapeDtypeStruct((n_rows,), jnp.int32),
        grid=(dest_blocks.shape[0] // pairs,),
        in_specs=[pl.BlockSpec(memory_space=pl.ANY), pl.BlockSpec(memory_space=pl.ANY)],
        out_specs=pl.BlockSpec(memory_space=pltpu.SMEM),
        scratch_shapes=[pltpu.SMEM((pairs,), jnp.int32), pltpu.SemaphoreType.DMA],
        compiler_params=_params(("arbitrary",)),
        name="moe_invert",
    )(dest_blocks, jnp.zeros((n_rows,), jnp.int32))


def _moe_expert_kernel(texp_ref, nused_ref, ids_ref, xp_ref, wg_ref, wu_ref, wd_ref, ys_ref,
                       xs, tmp, wgb, wub, wdb):
    i = pl.program_id(0)

    def gather(tile, slot):
        for r in range(MOE_TM):
            xs[slot, r * TOK_SUB:(r + 1) * TOK_SUB, :] = _load_token(xp_ref, ids_ref[tile * MOE_TM + r])

    @pl.when(i == 0)
    def _():
        gather(0, 0)

    @pl.when(i < nused_ref[0])
    def _():
        @pl.when(jnp.logical_or(i == 0, texp_ref[i] != texp_ref[jnp.maximum(i - 1, 0)]))
        def _():
            wgb[...] = wg_ref[...].astype(BF16)
            wub[...] = wu_ref[...].astype(BF16)
            wdb[...] = wd_ref[...].astype(BF16)

        slot = i % 2
        lhs = _tiles_to_rows(xs.at[slot], MOE_TM).astype(BF16)
        gather(i + 1, 1 - slot)
        act = jax.nn.silu(_dot(lhs, wgb[...])) * _dot(lhs, wub[...])
        ys_ref[...] = _rows_to_tiles(_dot(act.astype(BF16), wdb[...]), tmp)

    @pl.when(i >= nused_ref[0])
    def _():
        ys_ref[...] = jnp.zeros(ys_ref.shape, BF16)


def _moe_experts(plan, xp, w_gate, w_up, w_down, layer):
    nt = plan["tile_expert"].shape[0]
    grid_spec = pltpu.PrefetchScalarGridSpec(
        num_scalar_prefetch=3,
        grid=(nt,),
        in_specs=[
            pl.BlockSpec(memory_space=pltpu.VMEM),
            pl.BlockSpec((None, None, D_MODEL, D_EXPERT), lambda i, te, nu, ids: (layer, te[i], 0, 0)),
            pl.BlockSpec((None, None, D_MODEL, D_EXPERT), lambda i, te, nu, ids: (layer, te[i], 0, 0)),
            pl.BlockSpec((None, None, D_EXPERT, D_MODEL), lambda i, te, nu, ids: (layer, te[i], 0, 0)),
        ],
        out_specs=pl.BlockSpec((MOE_TM // 2, 2 * TOK_SUB, LANES), lambda i, te, nu, ids: (i, 0, 0)),
        scratch_shapes=[pltpu.VMEM((2, MOE_TM * TOK_SUB, LANES), F32),
                        pltpu.VMEM((MOE_TM * TOK_SUB, LANES), F32),
                        pltpu.VMEM((D_MODEL, D_EXPERT), BF16),
                        pltpu.VMEM((D_MODEL, D_EXPERT), BF16),
                        pltpu.VMEM((D_EXPERT, D_MODEL), BF16)],
    )
    return pl.pallas_call(
        _moe_expert_kernel,
        out_shape=jax.ShapeDtypeStruct((nt * MOE_TM // 2, 2 * TOK_SUB, LANES), BF16),
        grid_spec=grid_spec,
        compiler_params=_params(("arbitrary",)),
        name="moe_experts",
    )(plan["tile_expert"], plan["n_used"], plan["row_token"], xp, w_gate, w_up, w_down)


def _moe_combine_kernel(cstart_ref, npiece_ref, lbase_ref, ltile_ref, wk_ref, ys_hbm, xp_ref, sg_ref, su_ref,
                        sd_ref, x_ref, mod_ref, g_ref, o_ref, ysb, ybuf, npending, sems, *, block0):
    step = pl.program_id(0)
    slot = step % 2

    def piece_copy(src_row, dst_row, buf):
        return pltpu.make_async_copy(ys_hbm.at[pl.ds(src_row // 2, MOE_PIECE // 2)],
                                     ysb.at[buf, pl.ds(dst_row // 2, MOE_PIECE // 2)], sems.at[buf])

    def fetch(blk, buf):
        def per_expert(e, total):
            j = blk * N_EXPERTS + e
            src, dst, npc = cstart_ref[j], lbase_ref[j], npiece_ref[j]

            def per_piece(p, carry):
                piece_copy(pl.multiple_of(src + p * MOE_PIECE, MOE_CHUNK_ALIGN),
                           pl.multiple_of(dst + p * MOE_PIECE, MOE_PIECE), buf).start()
                return carry

            lax.fori_loop(0, npc, per_piece, 0)
            return total + npc

        npending[buf] = lax.fori_loop(0, N_EXPERTS, per_expert, 0)

    @pl.when(step == 0)
    def _():
        fetch(block0, 0)

    @pl.when(step + 1 < pl.num_programs(0))
    def _():
        fetch(block0 + step + 1, 1 - slot)

    ybuf[...] = xp_ref[...].astype(F32).reshape(MOE_TSB * TOK_SUB, LANES)
    lhs = _tiles_to_rows(ybuf, MOE_TSB).astype(BF16)
    act = jax.nn.silu(_dot(lhs, sg_ref[...].astype(BF16))) * _dot(lhs, su_ref[...].astype(BF16))
    o_ref[...] = _dot(act.astype(BF16), sd_ref[...].astype(BF16))

    def wait_one(p, carry):
        piece_copy(0, 0, slot).wait()
        return carry

    lax.fori_loop(0, npending[slot], wait_one, 0)

    t0 = step * MOE_TSB
    yblk = ysb.at[slot]

    def weighted_row(p):
        w = jnp.broadcast_to(wk_ref[p], (TOK_SUB, LANES))
        pair = yblk[ltile_ref[p]].astype(F32)
        second = lax.bitcast_convert_type(w, jnp.int32) < 0
        return jnp.abs(w) * jnp.where(second, pair[TOK_SUB:], pair[:TOK_SUB])

    def per_token(t, carry):
        p = (t0 + t) * TOP_K
        acc = weighted_row(p)
        for k in range(1, TOP_K):
            acc = acc + weighted_row(p + k)
        ybuf[pl.ds(pl.multiple_of(t * TOK_SUB, TOK_SUB), TOK_SUB), :] = acc
        return carry

    lax.fori_loop(0, MOE_TSB, per_token, 0, unroll=4)
    routed = _tiles_to_rows(ybuf, MOE_TSB)
    o_ref[...] = x_ref[...] + mod_ref[5:6, :] * _rms(routed + o_ref[...], g_ref[3:4, :])


def _moe_combine(plan, ltile, wk, block0, ys, xp, s_gate, s_up, s_down, layer, x1, mod, g, rows_per_seq,
                 first_row):
    n = x1.shape[0]
    cap = TOP_K * MOE_TSB + N_EXPERTS * MOE_PIECE
    grid_spec = pltpu.PrefetchScalarGridSpec(
        num_scalar_prefetch=5,
        grid=(n // MOE_TSB,),
        in_specs=[
            pl.BlockSpec(memory_space=pl.ANY),
            pl.BlockSpec((MOE_TSB // 2, 2 * TOK_SUB, LANES), lambda s, *_: (s, 0, 0)),
            pl.BlockSpec((None, D_MODEL, D_SHARED), lambda s, *_: (layer, 0, 0)),
            pl.BlockSpec((None, D_MODEL, D_SHARED), lambda s, *_: (layer, 0, 0)),
            pl.BlockSpec((None, D_SHARED, D_MODEL), lambda s, *_: (layer, 0, 0)),
            pl.BlockSpec((MOE_TSB, D_MODEL), lambda s, *_: (s, 0)),
            _mod_spec(MOE_TSB, rows_per_seq, first_row),
            pl.BlockSpec((4, D_MODEL), lambda s, *_: (0, 0)),
        ],
        out_specs=pl.BlockSpec((MOE_TSB, D_MODEL), lambda s, *_: (s, 0)),
        scratch_shapes=[pltpu.VMEM((2, cap // 2, 2 * TOK_SUB, LANES), BF16),
                        pltpu.VMEM((MOE_TSB * TOK_SUB, LANES), F32),
                        pltpu.SMEM((2,), jnp.int32),
                        pltpu.SemaphoreType.DMA((2,))],
    )
    return pl.pallas_call(
        functools.partial(_moe_combine_kernel, block0=block0),
        out_shape=jax.ShapeDtypeStruct((n, D_MODEL), F32),
        grid_spec=grid_spec,
        compiler_params=_params(("arbitrary",)),
        name="moe_combine",
    )(plan["cstart"], plan["npiece"], plan["lbase"], ltile, wk, ys, xp, s_gate, s_up, s_down, x1, mod, g)


def _rope(x, cos, sin_up, sin_dn):
    out = []
    for h in range(N_HEADS):
        xs = x[:, h * V_DIM:(h + 1) * V_DIM]
        up = pltpu.roll(xs, V_DIM - ROPE_PAIRS, 1)
        dn = pltpu.roll(xs, ROPE_PAIRS, 1)
        out.append(xs * cos + up * sin_up + dn * sin_dn)
    return jnp.concatenate(out, axis=1)


def _qkv_kernel(x_ref, mod_ref, g_ref, w_ref, *rest, rope):
    if rope:
        cos_ref, sup_ref, sdn_ref, q_ref, k_ref, v_ref = rest
    else:
        q_ref, k_ref, v_ref = rest
    h = _mod_in(x_ref[...], g_ref[...], mod_ref[0:1, :], mod_ref[1:2, :])
    qkv = _dot(h.astype(BF16), w_ref[...])
    q, k, v = qkv[:, :QK_W], qkv[:, QK_W:2 * QK_W], qkv[:, 2 * QK_W:]
    if rope:
        q = _rope(q, cos_ref[...], sup_ref[...], sdn_ref[...])
        k = _rope(k, cos_ref[...], sup_ref[...], sdn_ref[...])
    q_ref[...] = (q * HEAD_DIM ** -0.5).astype(q_ref.dtype)
    k_ref[...] = k.astype(k_ref.dtype)
    v_ref[...] = v.astype(v_ref.dtype)


def _qkv(x, mod, g, w_qkv, rows_per_seq, first_row, rope_tables, kv_dtype):
    n = x.shape[0]
    tm = 512
    rope = rope_tables is not None
    in_specs = [
        pl.BlockSpec((tm, D_MODEL), lambda i: (i, 0)),
        _mod_spec(tm, rows_per_seq, first_row),
        pl.BlockSpec((1, D_MODEL), lambda i: (0, 0)),
        pl.BlockSpec((D_MODEL, 3 * QK_W), lambda i: (0, 0)),
    ]
    args = [x, mod, g, w_qkv]
    if rope:
        tiles_per_seq = rows_per_seq // tm
        in_specs += [pl.BlockSpec((tm, V_DIM), lambda i: (i % tiles_per_seq, 0))] * 3
        args += list(rope_tables)
    return pl.pallas_call(
        functools.partial(_qkv_kernel, rope=rope),
        out_shape=(jax.ShapeDtypeStruct((n, QK_W), BF16),
                   jax.ShapeDtypeStruct((n, QK_W), kv_dtype),
                   jax.ShapeDtypeStruct((n, N_HEADS * V_DIM), kv_dtype)),
        grid=(n // tm,),
        in_specs=in_specs,
        out_specs=(pl.BlockSpec((tm, QK_W), lambda i: (i, 0)),) * 3,
        compiler_params=_params(("arbitrary",)),
        name="attn_qkv_rope" if rope else "attn_qkv",
    )(*args)


def _softmax_parts(s):
    e = jnp.exp(s - jnp.max(s, axis=-1, keepdims=True))
    return e, 1.0 / jnp.sum(e, axis=-1, keepdims=True)


def _attn_kernel(lp_ref, sub_ref, q_ref, k_ref, v_ref, o_ref, *, lam_init):
    lp = lp_ref[...]
    lam = (jnp.exp(jnp.sum(lp[0:1, :] * lp[1:2, :], axis=1, keepdims=True))
           - jnp.exp(jnp.sum(lp[2:3, :] * lp[3:4, :], axis=1, keepdims=True)) + lam_init)
    for h in range(N_HEADS):
        c0 = h * V_DIM
        q = q_ref[:, c0:c0 + V_DIM]
        k = k_ref[:, c0:c0 + V_DIM].astype(BF16)
        e1, r1 = _softmax_parts(_dot_nt(q[:, :HEAD_DIM], k[:, :HEAD_DIM]))
        e2, r2 = _softmax_parts(_dot_nt(q[:, HEAD_DIM:], k[:, HEAD_DIM:]))
        w = e1 * r1 - lam * (e2 * r2)
        o = _dot(w.astype(BF16), v_ref[:, c0:c0 + V_DIM].astype(BF16))
        o_ref[:, c0:c0 + V_DIM] = (_rms(o, sub_ref[...]) * (1.0 - lam_init)).astype(BF16)


def _attention(lp, subln, q, k, v, lam_init, tq):
    bsz, t = q.shape[:2]
    tk = k.shape[1]
    return pl.pallas_call(
        functools.partial(_attn_kernel, lam_init=lam_init),
        out_shape=jax.ShapeDtypeStruct((bsz, t, N_HEADS * V_DIM), BF16),
        grid=(bsz, t // tq),
        in_specs=[
            pl.BlockSpec((4, HEAD_DIM), lambda b, i: (0, 0)),
            pl.BlockSpec((1, V_DIM), lambda b, i: (0, 0)),
            pl.BlockSpec((None, tq, QK_W), lambda b, i: (b, i, 0)),
            pl.BlockSpec((None, tk, QK_W), lambda b, i: (b, 0, 0)),
            pl.BlockSpec((None, tk, N_HEADS * V_DIM), lambda b, i: (b, 0, 0)),
        ],
        out_specs=pl.BlockSpec((None, tq, N_HEADS * V_DIM), lambda b, i: (b, i, 0)),
        compiler_params=_params(("arbitrary", "arbitrary")),
        name="diff_attention",
    )(lp, subln, q, k, v)


def _lru_gate_layout(gate_w, gate_b):
    nb = LRU_BLOCKS // 2
    w = gate_w.reshape(2, 2, 2, nb, LRU_BLOCK, LRU_BLOCK)
    bd = jnp.einsum("dgsbio,bc->sbidgco", w, jnp.eye(nb, dtype=w.dtype))
    bd = bd.reshape(2, LRU_HALF, 4 * LRU_HALF).astype(BF16)
    b = gate_b.reshape(2, 2, 2, LRU_HALF).transpose(2, 0, 1, 3).reshape(2, 1, 4 * LRU_HALF)
    return bd, b


def _rope_tables(n):
    rows = n // GRID_W
    row = jnp.repeat(jnp.arange(rows, dtype=F32), GRID_W)
    col = jnp.tile(jnp.arange(GRID_W, dtype=F32), rows)
    freqs = ROPE_THETA ** (-jnp.arange(ROPE_PAIRS, dtype=F32) / ROPE_PAIRS)
    ar = row[:, None] * freqs
    ac = col[:, None] * freqs
    zero = jnp.zeros_like(ar)
    cos = jnp.concatenate([jnp.cos(ar), jnp.cos(ar), jnp.cos(ac), jnp.cos(ac)], axis=-1)
    sin_up = jnp.concatenate([-jnp.sin(ar), zero, -jnp.sin(ac), zero], axis=-1)
    sin_dn = jnp.concatenate([zero, jnp.sin(ar), zero, jnp.sin(ac)], axis=-1)
    return tuple(jnp.tile(t, (1, 2)) for t in (cos, sin_up, sin_dn))


def kernel(x_prompt, x_sample, c, state_lru, cache_k, cache_v, c_ctx, mod_w, mod_b, norm_g,
           lru_w_in, lru_conv_w, lru_conv_b, lru_gate_w, lru_gate_b, lru_lambda, lru_w_out,
           attn_w_qkv, attn_lambda, attn_subln, attn_w_o,
           moe_router, moe_router_bias, moe_w_gate, moe_w_up, moe_w_down,
           shared_w_gate, shared_w_up, shared_w_down):
    bp, tp = x_prompt.shape[:2]
    bs, ts = x_sample.shape[:2]
    past = cache_k.shape[2]
    cond = jnp.concatenate([c_ctx[None], c, jnp.zeros((MOD_ROWS - 1 - bs, D_MODEL), F32)], axis=0)
    mod = _modulation(cond, mod_w, mod_b)
    streams = [dict(x=x_prompt.reshape(bp * tp, D_MODEL), rps=0, row=0, b=bp, t=tp),
               dict(x=x_sample.reshape(bs * ts, D_MODEL), rps=ts, row=1, b=bs, t=ts)]
    new_lru, new_k, new_v = [], [], []
    for i in range(DEPTH):
        j = i // N_MIXERS
        g = norm_g[i]
        router_t = moe_router[i].T
        router_b = moe_router_bias[i].reshape(N_EXPERTS, 1)
        if i % N_MIXERS == 0:
            w_in = lru_w_in[j].astype(BF16)
            w_mix = lru_w_out[j].astype(BF16)
            gate_w, gate_b = _lru_gate_layout(lru_gate_w[j], lru_gate_b[j])
        else:
            lam_init = 0.8 - 0.6 * math.exp(-0.3 * i)
            w_qkv = attn_w_qkv[j].astype(BF16)
            w_mix = attn_w_o[j].astype(BF16)
            tables = _rope_tables(ts)
        for si, s in enumerate(streams):
            if i % N_MIXERS == 0:
                gate, xr = _lru_in(s["x"], mod[i], g[0:1], w_in, s["rps"], s["row"])
                h0 = jnp.zeros((bp, 2, D_RNN), F32) if si == 0 else state_lru[:, j]
                m, h_last = _lru_core(xr, gate, h0, lru_conv_w[j], lru_conv_b[j][None], gate_w, gate_b,
                                      lru_lambda[j], s["b"], s["t"])
                if si == 0:
                    new_lru.append(h_last)
            else:
                if si == 0:
                    q, k, v = _qkv(s["x"], mod[i], g[0:1], w_qkv, s["rps"], s["row"], None, F32)
                    new_k.append(k.reshape(bp, tp, N_HEADS, 2 * HEAD_DIM))
                    new_v.append(v.reshape(bp, tp, N_HEADS, V_DIM))
                    k3 = k.reshape(bp, tp, QK_W)
                    v3 = v.reshape(bp, tp, N_HEADS * V_DIM)
                    tq = tp
                else:
                    q, k, v = _qkv(s["x"], mod[i], g[0:1], w_qkv, s["rps"], s["row"], tables, BF16)
                    k3 = jnp.concatenate([cache_k[:, j].reshape(bs, past, QK_W).astype(BF16),
                                          k.reshape(bs, ts, QK_W)], axis=1)
                    v3 = jnp.concatenate([cache_v[:, j].reshape(bs, past, N_HEADS * V_DIM).astype(BF16),
                                          v.reshape(bs, ts, N_HEADS * V_DIM)], axis=1)
                    tq = 512
                o = _attention(attn_lambda[j], attn_subln[j][None], q.reshape(s["b"], s["t"], QK_W),
                               k3, v3, lam_init, tq)
                m = o.reshape(s["b"] * s["t"], N_HEADS * V_DIM)
            s["x1"], s["xp"], s["comb"] = _mix_out(m, w_mix, s["x"], mod[i], g, router_t, router_b,
                                                   s["rps"], s["row"])
        xp = jnp.concatenate([s["xp"] for s in streams], axis=0)
        ek, rk, wk, cnt = _moe_plan(jnp.concatenate([s["comb"] for s in streams], axis=1))
        plan = _moe_layout(ek, rk, wk, cnt)
        ys = _moe_experts(plan, xp, moe_w_gate, moe_w_up, moe_w_down, i)
        tok0 = 0
        for s in streams:
            n = s["x1"].shape[0]
            pairs = slice(tok0 * TOP_K, (tok0 + n) * TOP_K)
            s["x"] = _moe_combine(plan, plan["ltile"][pairs], plan["wk"][pairs], tok0 // MOE_TSB, ys, s["xp"],
                                  shared_w_gate, shared_w_up,
                                  shared_w_down, i, s["x1"], mod[i], g, s["rps"], s["row"])
            tok0 += n
    return (streams[0]["x"].reshape(bp, tp, D_MODEL),
            streams[1]["x"].reshape(bs, ts, D_MODEL),
            jnp.stack(new_lru, axis=1),
            jnp.stack(new_k, axis=1),
            jnp.stack(new_v, axis=1))
```

```python
import functools
import math

import jax
import jax.numpy as jnp
from jax import lax
from jax.experimental import pallas as pl
from jax.experimental.pallas import tpu as pltpu

D_MODEL = 1024
DEPTH = 2
N_MIXERS = 2
GRID_W = 64
EPS = 1e-6
D_RNN = 1280
LRU_BLOCKS = 16
LRU_BLOCK = D_RNN // LRU_BLOCKS
CONV_W = 4
CONV_LEFT = 2
LRU_C = 8.0
N_HEADS = 8
HEAD_DIM = 64
V_DIM = 2 * HEAD_DIM
QK_W = N_HEADS * 2 * HEAD_DIM
ROPE_PAIRS = HEAD_DIM // 4
ROPE_THETA = 10000.0
N_EXPERTS = 64
N_GROUPS = 8
GROUP_SIZE = N_EXPERTS // N_GROUPS
TOPK_GROUPS = 4
TOP_K = 8
D_EXPERT = 256
D_SHARED = 256
ROUTED_SCALE = 2.5

MOD_ROWS = 8
LRU_HALF = D_RNN // 2
CONV_PAD = 8
MOE_TM = 512
MOE_TSB = 512
MOE_CHUNK_ALIGN = 8
MOE_PIECE = 32
LANES = 128
TOK_SUB = D_MODEL // LANES
VMEM_LIMIT = 56 * 1024 * 1024
BF16 = jnp.bfloat16
F32 = jnp.float32
NEG_INF = float("-inf")


def _params(sem):
    return pltpu.CompilerParams(dimension_semantics=sem, vmem_limit_bytes=VMEM_LIMIT)


def _rms(x, g):
    return x * lax.rsqrt(jnp.mean(x * x, axis=-1, keepdims=True) + EPS) * g


def _mod_in(x, g, shift, scale):
    return _rms(x, g) * (1.0 + scale) + shift


def _dot(a, b):
    return jnp.dot(a, b, preferred_element_type=F32)


def _dot_nt(a, b, precision=None):
    return lax.dot_general(a, b, (((1,), (1,)), ((), ())), precision=precision,
                           preferred_element_type=F32)


def _mod_kernel(cond_ref, w_ref, b_ref, o_ref):
    cnd = cond_ref[...]
    s = cnd * jax.nn.sigmoid(cnd)
    o_ref[...] = jnp.dot(s, w_ref[...], precision=lax.Precision.HIGHEST,
                         preferred_element_type=F32) + b_ref[...]


def _modulation(cond, mod_w, mod_b):
    tn = 1536
    out = pl.pallas_call(
        _mod_kernel,
        out_shape=jax.ShapeDtypeStruct((DEPTH, MOD_ROWS, 6 * D_MODEL), F32),
        grid=(DEPTH, 6 * D_MODEL // tn),
        in_specs=[
            pl.BlockSpec((MOD_ROWS, D_MODEL), lambda l, n: (0, 0)),
            pl.BlockSpec((None, D_MODEL, tn), lambda l, n: (l, 0, n)),
            pl.BlockSpec((None, 1, tn), lambda l, n: (l, 0, n)),
        ],
        out_specs=pl.BlockSpec((None, MOD_ROWS, tn), lambda l, n: (l, 0, n)),
        compiler_params=_params(("arbitrary", "arbitrary")),
        name="modulation",
    )(cond, mod_w, mod_b.reshape(DEPTH, 1, 6 * D_MODEL))
    return out.reshape(DEPTH, MOD_ROWS, 6, D_MODEL)


def _mod_spec(tm, rows_per_seq, first_row):
    def index(i, *_):
        return (first_row + (i * tm) // rows_per_seq if rows_per_seq else first_row, 0, 0)
    return pl.BlockSpec((None, 6, D_MODEL), index)


def _lru_in_kernel(x_ref, mod_ref, g_ref, w_ref, gate_ref, xr_ref):
    h = _mod_in(x_ref[...], g_ref[...], mod_ref[0:1, :], mod_ref[1:2, :])
    xb = _dot(h.astype(BF16), w_ref[...])
    gate_ref[...] = jax.nn.gelu(xb[:, :D_RNN]).astype(BF16)
    xr_ref[...] = xb[:, D_RNN:]


def _lru_in(x, mod, g, w_in, rows_per_seq, first_row):
    n = x.shape[0]
    tm = 512
    return pl.pallas_call(
        _lru_in_kernel,
        out_shape=(jax.ShapeDtypeStruct((n, D_RNN), BF16), jax.ShapeDtypeStruct((n, D_RNN), F32)),
        grid=(n // tm,),
        in_specs=[
            pl.BlockSpec((tm, D_MODEL), lambda i: (i, 0)),
            _mod_spec(tm, rows_per_seq, first_row),
            pl.BlockSpec((1, D_MODEL), lambda i: (0, 0)),
            pl.BlockSpec((D_MODEL, 2 * D_RNN), lambda i: (0, 0)),
        ],
        out_specs=(pl.BlockSpec((tm, D_RNN), lambda i: (i, 0)),
                   pl.BlockSpec((tm, D_RNN), lambda i: (i, 0))),
        compiler_params=_params(("arbitrary",)),
        name="lru_in",
    )(x, mod, g, w_in)


def _lru_core_kernel(xr_ref, gate_ref, h0_ref, cw_ref, cb_ref, gw_ref, gb_ref, lam_ref,
                     m_ref, hl_ref, xpad, a_f, u_f, a_b, u_b, *, seq, chunk):
    c = LRU_HALF
    xpad[0:CONV_PAD, :] = jnp.zeros((CONV_PAD, c), F32)
    xpad[CONV_PAD + seq:, :] = jnp.zeros((CONV_PAD, c), F32)
    xpad[CONV_PAD:CONV_PAD + seq, :] = xr_ref[...]
    lam = lam_ref[...]
    coef = -LRU_C * jax.nn.softplus(-lam)
    for r0 in range(0, seq, chunk):
        xr = cb_ref[...]
        for j in range(CONV_W):
            off = CONV_PAD - CONV_LEFT + j + r0
            xr = xr + xpad[off:off + chunk, :] * cw_ref[j:j + 1, :]
        g = _dot(xr.astype(BF16), gw_ref[...]) + gb_ref[...]
        for d, (a_s, u_s) in enumerate(((a_f, u_f), (a_b, u_b))):
            r = 0.5 * (jnp.tanh(0.5 * g[:, 2 * d * c:(2 * d + 1) * c]) + 1.0)
            i = 0.5 * (jnp.tanh(0.5 * g[:, (2 * d + 1) * c:(2 * d + 2) * c]) + 1.0)
            log_a = coef[d:d + 1, :] * r
            a = jnp.exp(log_a)
            a_s[r0:r0 + chunk, :] = a
            u_s[r0:r0 + chunk, :] = jnp.sqrt(-jnp.tanh(log_a) * (a * a + 1.0)) * (i * xr)

    def step(t, carry):
        hf, hb = carry
        tb = seq - 1 - t
        hf = a_f[pl.ds(t, 1), :] * hf + u_f[pl.ds(t, 1), :]
        u_f[pl.ds(t, 1), :] = hf
        hb = a_b[pl.ds(tb, 1), :] * hb + u_b[pl.ds(tb, 1), :]
        u_b[pl.ds(tb, 1), :] = hb
        return hf, hb

    hf, hb = lax.fori_loop(0, seq, step, (h0_ref[0:1, :], h0_ref[1:2, :]), unroll=8)
    hl_ref[0:1, :] = hf
    hl_ref[1:2, :] = hb
    m_ref[...] = ((u_f[...] + u_b[...]) * gate_ref[...].astype(F32)).astype(BF16)


def _lru_core(xr, gate, h0, conv_w, conv_b, gate_w, gate_b, lam, n_seq, seq):
    c = LRU_HALF
    chunk = min(seq, 256)
    kernel = functools.partial(_lru_core_kernel, seq=seq, chunk=chunk)
    return pl.pallas_call(
        kernel,
        out_shape=(jax.ShapeDtypeStruct((n_seq * seq, D_RNN), BF16),
                   jax.ShapeDtypeStruct((n_seq, 2, D_RNN), F32)),
        grid=(n_seq, 2),
        in_specs=[
            pl.BlockSpec((seq, c), lambda b, j: (b, j)),
            pl.BlockSpec((seq, c), lambda b, j: (b, j)),
            pl.BlockSpec((None, 2, c), lambda b, j: (b, 0, j)),
            pl.BlockSpec((CONV_W, c), lambda b, j: (0, j)),
            pl.BlockSpec((1, c), lambda b, j: (0, j)),
            pl.BlockSpec((None, c, 4 * c), lambda b, j: (j, 0, 0)),
            pl.BlockSpec((None, 1, 4 * c), lambda b, j: (j, 0, 0)),
            pl.BlockSpec((2, c), lambda b, j: (0, j)),
        ],
        out_specs=(pl.BlockSpec((seq, c), lambda b, j: (b, j)),
                   pl.BlockSpec((None, 2, c), lambda b, j: (b, 0, j))),
        scratch_shapes=[pltpu.VMEM((seq + 2 * CONV_PAD, c), F32)] + [pltpu.VMEM((seq, c), F32)] * 4,
        compiler_params=_params(("arbitrary", "arbitrary")),
        name="lru_core",
    )(xr, gate, h0, conv_w, conv_b, gate_w, gate_b, lam)


def _route(sel, scores):
    tm = sel.shape[1]
    io8 = lax.broadcasted_iota(jnp.int32, (GROUP_SIZE, tm), 0)
    blocks, gscore = [], []
    for g in range(N_GROUPS):
        blk = sel[g * GROUP_SIZE:(g + 1) * GROUP_SIZE, :]
        m1 = jnp.max(blk, axis=0, keepdims=True)
        first = jnp.min(jnp.where(blk == m1, io8, GROUP_SIZE), axis=0, keepdims=True)
        m2 = jnp.max(jnp.where(io8 == first, NEG_INF, blk), axis=0, keepdims=True)
        blocks.append(blk)
        gscore.append(m1 + m2)
    masked = []
    for g in range(N_GROUPS):
        rank = jnp.zeros((1, tm), jnp.int32)
        for o in range(N_GROUPS):
            if o == g:
                continue
            beats = (gscore[o] >= gscore[g]) if o < g else (gscore[o] > gscore[g])
            rank = rank + beats.astype(jnp.int32)
        masked.append(jnp.where(rank < TOPK_GROUPS, blocks[g], NEG_INF))
    v = jnp.concatenate(masked, axis=0)
    ioe = lax.broadcasted_iota(jnp.int32, (N_EXPERTS, tm), 0)
    chosen = jnp.zeros((N_EXPERTS, tm), F32)
    for _ in range(TOP_K):
        mx = jnp.max(v, axis=0, keepdims=True)
        first = jnp.min(jnp.where(v == mx, ioe, N_EXPERTS), axis=0, keepdims=True)
        pick = ioe == first
        chosen = jnp.where(pick, 1.0, chosen)
        v = jnp.where(pick, NEG_INF, v)
    wsel = chosen * scores
    comb = wsel / jnp.sum(wsel, axis=0, keepdims=True) * ROUTED_SCALE
    return jnp.where(chosen > 0.0, comb, -1.0)


def _rows_to_tiles(x, tmp):
    tm = x.shape[0]
    for c in range(TOK_SUB):
        tmp[pl.ds(c, tm, stride=TOK_SUB), :] = x[:, c * LANES:(c + 1) * LANES]
    return tmp[...].reshape(tm // 2, 2 * TOK_SUB, LANES).astype(BF16)


def _tiles_to_rows(tmp, tm):
    return jnp.concatenate([tmp[pl.ds(c, tm, stride=TOK_SUB), :] for c in range(TOK_SUB)], axis=1)


SECOND_HALF = -2 ** 31


def _token_code(t):
    return jnp.where((t & 1) == 1, (t >> 1) | SECOND_HALF, t >> 1)


def _load_token(pairs_ref, code):
    pair = pairs_ref[code & (2 ** 31 - 1)].astype(F32)
    second = jnp.broadcast_to(code, (TOK_SUB, LANES)) < 0
    return jnp.where(second, pair[TOK_SUB:], pair[:TOK_SUB])


def _mix_out_kernel(m_ref, w_ref, x_ref, mod_ref, g_ref, rt_ref, rb_ref, x1_ref, h2_ref, comb_ref, tmp):
    y = _dot(m_ref[...], w_ref[...])
    x1 = x_ref[...] + mod_ref[2:3, :] * _rms(y, g_ref[1:2, :])
    x1_ref[...] = x1
    h2 = _mod_in(x1, g_ref[2:3, :], mod_ref[3:4, :], mod_ref[4:5, :])
    h2_ref[...] = _rows_to_tiles(h2, tmp)
    logits = _dot_nt(rt_ref[...], h2, precision=lax.Precision.HIGHEST)
    scores = jax.nn.sigmoid(logits)
    comb_ref[...] = _route(scores + rb_ref[...], scores)


def _mix_out(m, w, x, mod, g, router_t, router_b, rows_per_seq, first_row):
    n, k = m.shape
    tm = 512
    return pl.pallas_call(
        _mix_out_kernel,
        out_shape=(jax.ShapeDtypeStruct((n, D_MODEL), F32),
                   jax.ShapeDtypeStruct((n // 2, 2 * TOK_SUB, LANES), BF16),
                   jax.ShapeDtypeStruct((N_EXPERTS, n), F32)),
        grid=(n // tm,),
        in_specs=[
            pl.BlockSpec((tm, k), lambda i: (i, 0)),
            pl.BlockSpec((k, D_MODEL), lambda i: (0, 0)),
            pl.BlockSpec((tm, D_MODEL), lambda i: (i, 0)),
            _mod_spec(tm, rows_per_seq, first_row),
            pl.BlockSpec((4, D_MODEL), lambda i: (0, 0)),
            pl.BlockSpec((N_EXPERTS, D_MODEL), lambda i: (0, 0)),
            pl.BlockSpec((N_EXPERTS, 1), lambda i: (0, 0)),
        ],
        out_specs=(pl.BlockSpec((tm, D_MODEL), lambda i: (i, 0)),
                   pl.BlockSpec((tm // 2, 2 * TOK_SUB, LANES), lambda i: (i, 0, 0)),
                   pl.BlockSpec((N_EXPERTS, tm), lambda i: (0, i))),
        scratch_shapes=[pltpu.VMEM((tm * TOK_SUB, LANES), F32)],
        compiler_params=_params(("arbitrary",)),
        name="mix_out",
    )(m, w, x, mod, g, router_t, router_b)


def _moe_plan_kernel(comb_ref, ek_ref, rk_ref, wk_ref, cnt_ref):
    c = comb_ref[...]
    n = c.shape[1]
    chosen = c >= 0.0
    ch = chosen.astype(BF16)
    tri = (lax.broadcasted_iota(jnp.int32, (n, n), 0) <= lax.broadcasted_iota(jnp.int32, (n, n), 1))
    incl = _dot(ch, tri.astype(BF16))
    rank = incl - ch.astype(F32)
    cnt_ref[...] = jnp.broadcast_to(incl[:, n - 1:n], cnt_ref.shape)
    low = (lax.broadcasted_iota(jnp.int32, (N_EXPERTS, N_EXPERTS), 1)
           < lax.broadcasted_iota(jnp.int32, (N_EXPERTS, N_EXPERTS), 0))
    slot = _dot(low.astype(BF16), ch)
    ioe = lax.broadcasted_iota(jnp.int32, c.shape, 0).astype(F32)
    for k in range(TOP_K):
        sel = jnp.logical_and(chosen, slot == float(k))
        ek_ref[k:k + 1, :] = jnp.sum(jnp.where(sel, ioe, 0.0), axis=0, keepdims=True).astype(jnp.int32)
        rk_ref[k:k + 1, :] = jnp.sum(jnp.where(sel, rank, 0.0), axis=0, keepdims=True).astype(jnp.int32)
        wk_ref[k:k + 1, :] = jnp.sum(jnp.where(sel, c, 0.0), axis=0, keepdims=True)


def _moe_plan(comb):
    n = comb.shape[1]
    nsb = n // MOE_TSB
    out_i = jax.ShapeDtypeStruct((TOP_K, n), jnp.int32)
    pair_spec = pl.BlockSpec((TOP_K, MOE_TSB), lambda s: (0, s))
    return pl.pallas_call(
        _moe_plan_kernel,
        out_shape=(out_i, out_i, jax.ShapeDtypeStruct((TOP_K, n), F32),
                   jax.ShapeDtypeStruct((N_EXPERTS, nsb * 128), F32)),
        grid=(nsb,),
        in_specs=[pl.BlockSpec((N_EXPERTS, MOE_TSB), lambda s: (0, s))],
        out_specs=(pair_spec, pair_spec, pair_spec, pl.BlockSpec((N_EXPERTS, 128), lambda s: (0, s))),
        compiler_params=_params(("arbitrary",)),
        name="moe_plan",
    )(comb)


def _moe_num_tiles(n_tok):
    rows = n_tok * TOP_K + N_EXPERTS * (n_tok // MOE_TSB) * (MOE_CHUNK_ALIGN - 1)
    return rows // MOE_TM + N_EXPERTS + 1


def _ceil_to(x, m):
    return (x + m - 1) // m * m


def _moe_layout(ek, rk, wk, cnt):
    n = ek.shape[1]
    nsb = n // MOE_TSB
    nt = _moe_num_tiles(n)
    n_es = cnt[:, ::128].astype(jnp.int32)
    c_al = _ceil_to(n_es, MOE_CHUNK_ALIGN)
    tiles_e = _ceil_to(jnp.sum(c_al, axis=1), MOE_TM) // MOE_TM
    tile0 = jnp.cumsum(tiles_e) - tiles_e
    cstart = tile0[:, None] * MOE_TM + jnp.cumsum(c_al, axis=1) - c_al
    npiece = _ceil_to(n_es, MOE_PIECE) // MOE_PIECE
    lbase = (jnp.cumsum(npiece, axis=0) - npiece) * MOE_PIECE
    dest, ldest = _moe_dest(ek, rk, cstart, lbase)
    dest_blocks = dest.reshape(TOP_K, nsb, MOE_TSB).transpose(1, 0, 2).reshape(-1)
    row_token = _moe_invert(dest_blocks, nt * MOE_TM)
    tile_expert = jnp.sum(jnp.arange(nt, dtype=jnp.int32)[:, None] >= tile0[None, :], axis=1) - 1
    wsigned = jnp.where((ldest & 1) == 1, -wk, wk)
    return dict(row_token=row_token,
                tile_expert=tile_expert.astype(jnp.int32), n_used=jnp.sum(tiles_e).reshape(1).astype(jnp.int32),
                cstart=cstart.T.reshape(-1), npiece=npiece.T.reshape(-1), lbase=lbase.T.reshape(-1),
                ltile=(ldest >> 1).T.reshape(-1), wk=wsigned.T.reshape(-1))


def _moe_dest_kernel(ek_ref, rk_ref, cs_ref, lb_ref, dest_ref, ldest_ref):
    ioe = lax.broadcasted_iota(jnp.int32, (N_EXPERTS, ek_ref.shape[1]), 0)
    cs = cs_ref[:, 0:1]
    lb = lb_ref[:, 0:1]
    for k in range(TOP_K):
        hit = ioe == ek_ref[k:k + 1, :]
        rk = rk_ref[k:k + 1, :]
        dest_ref[k:k + 1, :] = jnp.sum(jnp.where(hit, cs, 0.0), axis=0, keepdims=True).astype(jnp.int32) + rk
        ldest_ref[k:k + 1, :] = jnp.sum(jnp.where(hit, lb, 0.0), axis=0, keepdims=True).astype(jnp.int32) + rk


def _moe_dest(ek, rk, cstart, lbase):
    n = ek.shape[1]
    pair_spec = pl.BlockSpec((TOP_K, MOE_TSB), lambda s: (0, s))
    table_spec = pl.BlockSpec((N_EXPERTS, LANES), lambda s: (0, s))
    out = jax.ShapeDtypeStruct((TOP_K, n), jnp.int32)
    spread = lambda tab: jnp.repeat(tab.astype(F32), LANES, axis=1)
    return pl.pallas_call(
        _moe_dest_kernel,
        out_shape=(out, out),
        grid=(n // MOE_TSB,),
        in_specs=[pair_spec, pair_spec, table_spec, table_spec],
        out_specs=(pair_spec, pair_spec),
        compiler_params=_params(("arbitrary",)),
        name="moe_dest",
    )(ek, rk, spread(cstart), spread(lbase))


def _moe_invert_kernel(dest_hbm, zeros_hbm, rt_ref, dsm, sem):
    s = pl.program_id(0)
    pairs = TOP_K * MOE_TSB

    @pl.when(s == 0)
    def _():
        fill = pltpu.make_async_copy(zeros_hbm, rt_ref, sem)
        fill.start()
        fill.wait()

    copy = pltpu.make_async_copy(dest_hbm.at[pl.ds(pl.multiple_of(s * pairs, pairs), pairs)], dsm, sem)
    copy.start()
    copy.wait()

    def per_token(t, carry):
        code = _token_code(s * MOE_TSB + t)
        for k in range(TOP_K):
            rt_ref[dsm[k * MOE_TSB + t]] = code
        return carry

    lax.fori_loop(0, MOE_TSB, per_token, 0, unroll=4)


def _moe_invert(dest_blocks, n_rows):
    pairs = TOP_K * MOE_TSB
    return pl.pallas_call(
        _moe_invert_kernel,
        out_shape=jax.ShapeDtypeStruct((n_rows,), jnp.int32),
        grid=(dest_blocks.shape[0] // pairs,),
        in_specs=[pl.BlockSpec(memory_space=pl.ANY), pl.BlockSpec(memory_space=pl.ANY)],
        out_specs=pl.BlockSpec(memory_space=pltpu.SMEM),
        scratch_shapes=[pltpu.SMEM((pairs,), jnp.int32), pltpu.SemaphoreType.DMA],
        compiler_params=_params(("arbitrary",)),
        name="moe_invert",
    )(dest_blocks, jnp.zeros((n_rows,), jnp.int32))


def _moe_expert_kernel(texp_ref, nused_ref, ids_ref, xp_ref, wg_ref, wu_ref, wd_ref, ys_ref,
                       xs_even, xs_odd, tmp, wgb, wub, wdb):
    i = pl.program_id(0)

    def gather(tile, xs):
        for r in range(MOE_TM):
            xs[r * TOK_SUB:(r + 1) * TOK_SUB, :] = _load_token(xp_ref, ids_ref[tile * MOE_TM + r])

    @pl.when(i == 0)
    def _():
        gather(0, xs_even)

    def tile_body(cur, nxt):
        gather(i + 1, nxt)
        lhs = _tiles_to_rows(cur, MOE_TM).astype(BF16)
        act = jax.nn.silu(_dot(lhs, wgb[...])) * _dot(lhs, wub[...])
        ys_ref[...] = _rows_to_tiles(_dot(act.astype(BF16), wdb[...]), tmp)

    @pl.when(i < nused_ref[0])
    def _():
        @pl.when(jnp.logical_or(i == 0, texp_ref[i] != texp_ref[jnp.maximum(i - 1, 0)]))
        def _():
            wgb[...] = wg_ref[...].astype(BF16)
            wub[...] = wu_ref[...].astype(BF16)
            wdb[...] = wd_ref[...].astype(BF16)

        @pl.when(i % 2 == 0)
        def _():
            tile_body(xs_even, xs_odd)

        @pl.when(i % 2 == 1)
        def _():
            tile_body(xs_odd, xs_even)

    @pl.when(i >= nused_ref[0])
    def _():
        ys_ref[...] = jnp.zeros(ys_ref.shape, BF16)


def _moe_experts(plan, xp, w_gate, w_up, w_down, layer):
    nt = plan["tile_expert"].shape[0]
    grid_spec = pltpu.PrefetchScalarGridSpec(
        num_scalar_prefetch=3,
        grid=(nt,),
        in_specs=[
            pl.BlockSpec(memory_space=pltpu.VMEM),
            pl.BlockSpec((None, None, D_MODEL, D_EXPERT), lambda i, te, nu, ids: (layer, te[i], 0, 0)),
            pl.BlockSpec((None, None, D_MODEL, D_EXPERT), lambda i, te, nu, ids: (layer, te[i], 0, 0)),
            pl.BlockSpec((None, None, D_EXPERT, D_MODEL), lambda i, te, nu, ids: (layer, te[i], 0, 0)),
        ],
        out_specs=pl.BlockSpec((MOE_TM // 2, 2 * TOK_SUB, LANES), lambda i, te, nu, ids: (i, 0, 0)),
        scratch_shapes=[pltpu.VMEM((MOE_TM * TOK_SUB, LANES), F32),
                        pltpu.VMEM((MOE_TM * TOK_SUB, LANES), F32),
                        pltpu.VMEM((MOE_TM * TOK_SUB, LANES), F32),
                        pltpu.VMEM((D_MODEL, D_EXPERT), BF16),
                        pltpu.VMEM((D_MODEL, D_EXPERT), BF16),
                        pltpu.VMEM((D_EXPERT, D_MODEL), BF16)],
    )
    return pl.pallas_call(
        _moe_expert_kernel,
        out_shape=jax.ShapeDtypeStruct((nt * MOE_TM // 2, 2 * TOK_SUB, LANES), BF16),
        grid_spec=grid_spec,
        compiler_params=_params(("arbitrary",)),
        name="moe_experts",
    )(plan["tile_expert"], plan["n_used"], plan["row_token"], xp, w_gate, w_up, w_down)


def _moe_combine_kernel(cstart_ref, npiece_ref, lbase_ref, ltile_ref, wk_ref, ys_hbm, xp_ref, sg_ref, su_ref,
                        sd_ref, x_ref, mod_ref, g_ref, o_ref, ysb, ybuf, npending, sems, *, block0):
    step = pl.program_id(0)
    slot = step % 2

    def piece_copy(src_row, dst_row, buf):
        return pltpu.make_async_copy(ys_hbm.at[pl.ds(src_row // 2, MOE_PIECE // 2)],
                                     ysb.at[buf, pl.ds(dst_row // 2, MOE_PIECE // 2)], sems.at[buf])

    def fetch(blk, buf):
        def per_expert(e, total):
            j = blk * N_EXPERTS + e
            src, dst, npc = cstart_ref[j], lbase_ref[j], npiece_ref[j]

            def per_piece(p, carry):
                piece_copy(pl.multiple_of(src + p * MOE_PIECE, MOE_CHUNK_ALIGN),
                           pl.multiple_of(dst + p * MOE_PIECE, MOE_PIECE), buf).start()
                return carry

            lax.fori_loop(0, npc, per_piece, 0)
            return total + npc

        npending[buf] = lax.fori_loop(0, N_EXPERTS, per_expert, 0)

    @pl.when(step == 0)
    def _():
        fetch(block0, 0)

    @pl.when(step + 1 < pl.num_programs(0))
    def _():
        fetch(block0 + step + 1, 1 - slot)

    ybuf[...] = xp_ref[...].astype(F32).reshape(MOE_TSB * TOK_SUB, LANES)
    lhs = _tiles_to_rows(ybuf, MOE_TSB).astype(BF16)
    act = jax.nn.silu(_dot(lhs, sg_ref[...].astype(BF16))) * _dot(lhs, su_ref[...].astype(BF16))
    o_ref[...] = _dot(act.astype(BF16), sd_ref[...].astype(BF16))

    def wait_one(p, carry):
        piece_copy(0, 0, slot).wait()
        return carry

    lax.fori_loop(0, npending[slot], wait_one, 0)

    t0 = step * MOE_TSB
    yblk = ysb.at[slot]

    def weighted_row(p):
        w = jnp.broadcast_to(wk_ref[p], (TOK_SUB, LANES))
        pair = yblk[ltile_ref[p]].astype(F32)
        second = lax.bitcast_convert_type(w, jnp.int32) < 0
        return jnp.abs(w) * jnp.where(second, pair[TOK_SUB:], pair[:TOK_SUB])

    def per_token(t, carry):
        p = (t0 + t) * TOP_K
        acc = weighted_row(p)
        for k in range(1, TOP_K):
            acc = acc + weighted_row(p + k)
        ybuf[pl.ds(pl.multiple_of(t * TOK_SUB, TOK_SUB), TOK_SUB), :] = acc
        return carry

    lax.fori_loop(0, MOE_TSB, per_token, 0, unroll=4)
    routed = _tiles_to_rows(ybuf, MOE_TSB)
    o_ref[...] = x_ref[...] + mod_ref[5:6, :] * _rms(routed + o_ref[...], g_ref[3:4, :])


def _moe_combine(plan, ltile, wk, block0, ys, xp, s_gate, s_up, s_down, layer, x1, mod, g, rows_per_seq,
                 first_row):
    n = x1.shape[0]
    cap = TOP_K * MOE_TSB + N_EXPERTS * MOE_PIECE
    grid_spec = pltpu.PrefetchScalarGridSpec(
        num_scalar_prefetch=5,
        grid=(n // MOE_TSB,),
        in_specs=[
            pl.BlockSpec(memory_space=pl.ANY),
            pl.BlockSpec((MOE_TSB // 2, 2 * TOK_SUB, LANES), lambda s, *_: (s, 0, 0)),
            pl.BlockSpec((None, D_MODEL, D_SHARED), lambda s, *_: (layer, 0, 0)),
            pl.BlockSpec((None, D_MODEL, D_SHARED), lambda s, *_: (layer, 0, 0)),
            pl.BlockSpec((None, D_SHARED, D_MODEL), lambda s, *_: (layer, 0, 0)),
            pl.BlockSpec((MOE_TSB, D_MODEL), lambda s, *_: (s, 0)),
            _mod_spec(MOE_TSB, rows_per_seq, first_row),
            pl.BlockSpec((4, D_MODEL), lambda s, *_: (0, 0)),
        ],
        out_specs=pl.BlockSpec((MOE_TSB, D_MODEL), lambda s, *_: (s, 0)),
        scratch_shapes=[pltpu.VMEM((2, cap // 2, 2 * TOK_SUB, LANES), BF16),
                        pltpu.VMEM((MOE_TSB * TOK_SUB, LANES), F32),
                        pltpu.SMEM((2,), jnp.int32),
                        pltpu.SemaphoreType.DMA((2,))],
    )
    return pl.pallas_call(
        functools.partial(_moe_combine_kernel, block0=block0),
        out_shape=jax.ShapeDtypeStruct((n, D_MODEL), F32),
        grid_spec=grid_spec,
        compiler_params=_params(("arbitrary",)),
        name="moe_combine",
    )(plan["cstart"], plan["npiece"], plan["lbase"], ltile, wk, ys, xp, s_gate, s_up, s_down, x1, mod, g)


def _rope(x, cos, sin_up, sin_dn):
    out = []
    for h in range(N_HEADS):
        xs = x[:, h * V_DIM:(h + 1) * V_DIM]
        up = pltpu.roll(xs, V_DIM - ROPE_PAIRS, 1)
        dn = pltpu.roll(xs, ROPE_PAIRS, 1)
        out.append(xs * cos + up * sin_up + dn * sin_dn)
    return jnp.concatenate(out, axis=1)


def _qkv_kernel(x_ref, mod_ref, g_ref, w_ref, *rest, rope):
    if rope:
        cos_ref, sup_ref, sdn_ref, q_ref, k_ref, v_ref = rest
    else:
        q_ref, k_ref, v_ref = rest
    h = _mod_in(x_ref[...], g_ref[...], mod_ref[0:1, :], mod_ref[1:2, :])
    qkv = _dot(h.astype(BF16), w_ref[...])
    q, k, v = qkv[:, :QK_W], qkv[:, QK_W:2 * QK_W], qkv[:, 2 * QK_W:]
    if rope:
        q = _rope(q, cos_ref[...], sup_ref[...], sdn_ref[...])
        k = _rope(k, cos_ref[...], sup_ref[...], sdn_ref[...])
    q_ref[...] = (q * HEAD_DIM ** -0.5).astype(q_ref.dtype)
    k_ref[...] = k.astype(k_ref.dtype)
    v_ref[...] = v.astype(v_ref.dtype)


def _qkv(x, mod, g, w_qkv, rows_per_seq, first_row, rope_tables, kv_dtype):
    n = x.shape[0]
    tm = 512
    rope = rope_tables is not None
    in_specs = [
        pl.BlockSpec((tm, D_MODEL), lambda i: (i, 0)),
        _mod_spec(tm, rows_per_seq, first_row),
        pl.BlockSpec((1, D_MODEL), lambda i: (0, 0)),
        pl.BlockSpec((D_MODEL, 3 * QK_W), lambda i: (0, 0)),
    ]
    args = [x, mod, g, w_qkv]
    if rope:
        tiles_per_seq = rows_per_seq // tm
        in_specs += [pl.BlockSpec((tm, V_DIM), lambda i: (i % tiles_per_seq, 0))] * 3
        args += list(rope_tables)
    return pl.pallas_call(
        functools.partial(_qkv_kernel, rope=rope),
        out_shape=(jax.ShapeDtypeStruct((n, QK_W), BF16),
                   jax.ShapeDtypeStruct((n, QK_W), kv_dtype),
                   jax.ShapeDtypeStruct((n, N_HEADS * V_DIM), kv_dtype)),
        grid=(n // tm,),
        in_specs=in_specs,
        out_specs=(pl.BlockSpec((tm, QK_W), lambda i: (i, 0)),) * 3,
        compiler_params=_params(("arbitrary",)),
        name="attn_qkv_rope" if rope else "attn_qkv",
    )(*args)


def _softmax_parts(s):
    e = jnp.exp(s - jnp.max(s, axis=-1, keepdims=True))
    return e, 1.0 / jnp.sum(e, axis=-1, keepdims=True)


def _attn_kernel(lp_ref, sub_ref, q_ref, k_ref, v_ref, o_ref, *, lam_init):
    lp = lp_ref[...]
    lam = (jnp.exp(jnp.sum(lp[0:1, :] * lp[1:2, :], axis=1, keepdims=True))
           - jnp.exp(jnp.sum(lp[2:3, :] * lp[3:4, :], axis=1, keepdims=True)) + lam_init)
    for h in range(N_HEADS):
        c0 = h * V_DIM
        q = q_ref[:, c0:c0 + V_DIM]
        k = k_ref[:, c0:c0 + V_DIM].astype(BF16)
        e1, r1 = _softmax_parts(_dot_nt(q[:, :HEAD_DIM], k[:, :HEAD_DIM]))
        e2, r2 = _softmax_parts(_dot_nt(q[:, HEAD_DIM:], k[:, HEAD_DIM:]))
        w = e1 * r1 - lam * (e2 * r2)
        o = _dot(w.astype(BF16), v_ref[:, c0:c0 + V_DIM].astype(BF16))
        o_ref[:, c0:c0 + V_DIM] = (_rms(o, sub_ref[...]) * (1.0 - lam_init)).astype(BF16)


def _attention(lp, subln, q, k, v, lam_init, tq):
    bsz, t = q.shape[:2]
    tk = k.shape[1]
    return pl.pallas_call(
        functools.partial(_attn_kernel, lam_init=lam_init),
        out_shape=jax.ShapeDtypeStruct((bsz, t, N_HEADS * V_DIM), BF16),
        grid=(bsz, t // tq),
        in_specs=[
            pl.BlockSpec((4, HEAD_DIM), lambda b, i: (0, 0)),
            pl.BlockSpec((1, V_DIM), lambda b, i: (0, 0)),
            pl.BlockSpec((None, tq, QK_W), lambda b, i: (b, i, 0)),
            pl.BlockSpec((None, tk, QK_W), lambda b, i: (b, 0, 0)),
            pl.BlockSpec((None, tk, N_HEADS * V_DIM), lambda b, i: (b, 0, 0)),
        ],
        out_specs=pl.BlockSpec((None, tq, N_HEADS * V_DIM), lambda b, i: (b, i, 0)),
        compiler_params=_params(("arbitrary", "arbitrary")),
        name="diff_attention",
    )(lp, subln, q, k, v)


def _lru_gate_layout(gate_w, gate_b):
    nb = LRU_BLOCKS // 2
    w = gate_w.reshape(2, 2, 2, nb, LRU_BLOCK, LRU_BLOCK)
    bd = jnp.einsum("dgsbio,bc->sbidgco", w, jnp.eye(nb, dtype=w.dtype))
    bd = bd.reshape(2, LRU_HALF, 4 * LRU_HALF).astype(BF16)
    b = gate_b.reshape(2, 2, 2, LRU_HALF).transpose(2, 0, 1, 3).reshape(2, 1, 4 * LRU_HALF)
    return bd, b


def _rope_tables(n):
    rows = n // GRID_W
    row = jnp.repeat(jnp.arange(rows, dtype=F32), GRID_W)
    col = jnp.tile(jnp.arange(GRID_W, dtype=F32), rows)
    freqs = ROPE_THETA ** (-jnp.arange(ROPE_PAIRS, dtype=F32) / ROPE_PAIRS)
    ar = row[:, None] * freqs
    ac = col[:, None] * freqs
    zero = jnp.zeros_like(ar)
    cos = jnp.concatenate([jnp.cos(ar), jnp.cos(ar), jnp.cos(ac), jnp.cos(ac)], axis=-1)
    sin_up = jnp.concatenate([-jnp.sin(ar), zero, -jnp.sin(ac), zero], axis=-1)
    sin_dn = jnp.concatenate([zero, jnp.sin(ar), zero, jnp.sin(ac)], axis=-1)
    return tuple(jnp.tile(t, (1, 2)) for t in (cos, sin_up, sin_dn))


def kernel(x_prompt, x_sample, c, state_lru, cache_k, cache_v, c_ctx, mod_w, mod_b, norm_g,
           lru_w_in, lru_conv_w, lru_conv_b, lru_gate_w, lru_gate_b, lru_lambda, lru_w_out,
           attn_w_qkv, attn_lambda, attn_subln, attn_w_o,
           moe_router, moe_router_bias, moe_w_gate, moe_w_up, moe_w_down,
           shared_w_gate, shared_w_up, shared_w_down):
    bp, tp = x_prompt.shape[:2]
    bs, ts = x_sample.shape[:2]
    past = cache_k.shape[2]
    cond = jnp.concatenate([c_ctx[None], c, jnp.zeros((MOD_ROWS - 1 - bs, D_MODEL), F32)], axis=0)
    mod = _modulation(cond, mod_w, mod_b)
    streams = [dict(x=x_prompt.reshape(bp * tp, D_MODEL), rps=0, row=0, b=bp, t=tp),
               dict(x=x_sample.reshape(bs * ts, D_MODEL), rps=ts, row=1, b=bs, t=ts)]
    new_lru, new_k, new_v = [], [], []
    for i in range(DEPTH):
        j = i // N_MIXERS
        g = norm_g[i]
        router_t = moe_router[i].T
        router_b = moe_router_bias[i].reshape(N_EXPERTS, 1)
        if i % N_MIXERS == 0:
            w_in = lru_w_in[j].astype(BF16)
            w_mix = lru_w_out[j].astype(BF16)
            gate_w, gate_b = _lru_gate_layout(lru_gate_w[j], lru_gate_b[j])
        else:
            lam_init = 0.8 - 0.6 * math.exp(-0.3 * i)
            w_qkv = attn_w_qkv[j].astype(BF16)
            w_mix = attn_w_o[j].astype(BF16)
            tables = _rope_tables(ts)
        for si, s in enumerate(streams):
            if i % N_MIXERS == 0:
                gate, xr = _lru_in(s["x"], mod[i], g[0:1], w_in, s["rps"], s["row"])
                h0 = jnp.zeros((bp, 2, D_RNN), F32) if si == 0 else state_lru[:, j]
                m, h_last = _lru_core(xr, gate, h0, lru_conv_w[j], lru_conv_b[j][None], gate_w, gate_b,
                                      lru_lambda[j], s["b"], s["t"])
                if si == 0:
                    new_lru.append(h_last)
            else:
                if si == 0:
                    q, k, v = _qkv(s["x"], mod[i], g[0:1], w_qkv, s["rps"], s["row"], None, F32)
                    new_k.append(k.reshape(bp, tp, N_HEADS, 2 * HEAD_DIM))
                    new_v.append(v.reshape(bp, tp, N_HEADS, V_DIM))
                    k3 = k.reshape(bp, tp, QK_W)
                    v3 = v.reshape(bp, tp, N_HEADS * V_DIM)
                    tq = tp
                else:
                    q, k, v = _qkv(s["x"], mod[i], g[0:1], w_qkv, s["rps"], s["row"], tables, BF16)
                    k3 = jnp.concatenate([cache_k[:, j].reshape(bs, past, QK_W).astype(BF16),
                                          k.reshape(bs, ts, QK_W)], axis=1)
                    v3 = jnp.concatenate([cache_v[:, j].reshape(bs, past, N_HEADS * V_DIM).astype(BF16),
                                          v.reshape(bs, ts, N_HEADS * V_DIM)], axis=1)
                    tq = 512
                o = _attention(attn_lambda[j], attn_subln[j][None], q.reshape(s["b"], s["t"], QK_W),
                               k3, v3, lam_init, tq)
                m = o.reshape(s["b"] * s["t"], N_HEADS * V_DIM)
            s["x1"], s["xp"], s["comb"] = _mix_out(m, w_mix, s["x"], mod[i], g, router_t, router_b,
                                                   s["rps"], s["row"])
        xp = jnp.concatenate([s["xp"] for s in streams], axis=0)
        ek, rk, wk, cnt = _moe_plan(jnp.concatenate([s["comb"] for s in streams], axis=1))
        plan = _moe_layout(ek, rk, wk, cnt)
        ys = _moe_experts(plan, xp, moe_w_gate, moe_w_up, moe_w_down, i)
        tok0 = 0
        for s in streams:
            n = s["x1"].shape[0]
            pairs = slice(tok0 * TOP_K, (tok0 + n) * TOP_K)
            s["x"] = _moe_combine(plan, plan["ltile"][pairs], plan["wk"][pairs], tok0 // MOE_TSB, ys, s["xp"],
                                  shared_w_gate, shared_w_up,
                                  shared_w_down, i, s["x1"], mod[i], g, s["rps"], s["row"])
            tok0 += n
    return (streams[0]["x"].reshape(bp, tp, D_MODEL),
            streams[1]["x"].reshape(bs, ts, D_MODEL),
            jnp.stack(new_lru, axis=1),
            jnp.stack(new_k, axis=1),
            jnp.stack(new_v, axis=1))
```

```python
import functools
import math

import jax
import jax.numpy as jnp
from jax import lax
from jax.experimental import pallas as pl
from jax.experimental.pallas import tpu as pltpu

D_MODEL = 1024
DEPTH = 2
N_MIXERS = 2
GRID_W = 64
EPS = 1e-6
D_RNN = 1280
LRU_BLOCKS = 16
LRU_BLOCK = D_RNN // LRU_BLOCKS
CONV_W = 4
CONV_LEFT = 2
LRU_C = 8.0
N_HEADS = 8
HEAD_DIM = 64
V_DIM = 2 * HEAD_DIM
QK_W = N_HEADS * 2 * HEAD_DIM
ROPE_PAIRS = HEAD_DIM // 4
ROPE_THETA = 10000.0
N_EXPERTS = 64
N_GROUPS = 8
GROUP_SIZE = N_EXPERTS // N_GROUPS
TOPK_GROUPS = 4
TOP_K = 8
D_EXPERT = 256
D_SHARED = 256
ROUTED_SCALE = 2.5

MOD_ROWS = 8
LRU_HALF = D_RNN // 2
CONV_PAD = 8
MOE_TM = 512
MOE_ROW_GROUPS = 1
MOE_TSB = 512
MOE_CHUNK_ALIGN = 8
MOE_PIECE = 64
LANES = 128
TOK_SUB = D_MODEL // LANES
VMEM_LIMIT = 56 * 1024 * 1024
BF16 = jnp.bfloat16
F32 = jnp.float32
NEG_INF = float("-inf")


def _params(sem):
    return pltpu.CompilerParams(dimension_semantics=sem, vmem_limit_bytes=VMEM_LIMIT)


def _rms(x, g):
    return x * lax.rsqrt(jnp.mean(x * x, axis=-1, keepdims=True) + EPS) * g


def _mod_in(x, g, shift, scale):
    return _rms(x, g) * (1.0 + scale) + shift


def _dot(a, b):
    return jnp.dot(a, b, preferred_element_type=F32)


def _dot_nt(a, b, precision=None):
    return lax.dot_general(a, b, (((1,), (1,)), ((), ())), precision=precision,
                           preferred_element_type=F32)


def _mod_kernel(cond_ref, w_ref, b_ref, o_ref):
    cnd = cond_ref[...]
    s = cnd * jax.nn.sigmoid(cnd)
    o_ref[...] = jnp.dot(s, w_ref[...], precision=lax.Precision.HIGHEST,
                         preferred_element_type=F32) + b_ref[...]


def _modulation(cond, mod_w, mod_b):
    tn = 1536
    out = pl.pallas_call(
        _mod_kernel,
        out_shape=jax.ShapeDtypeStruct((DEPTH, MOD_ROWS, 6 * D_MODEL), F32),
        grid=(DEPTH, 6 * D_MODEL // tn),
        in_specs=[
            pl.BlockSpec((MOD_ROWS, D_MODEL), lambda l, n: (0, 0)),
            pl.BlockSpec((None, D_MODEL, tn), lambda l, n: (l, 0, n)),
            pl.BlockSpec((None, 1, tn), lambda l, n: (l, 0, n)),
        ],
        out_specs=pl.BlockSpec((None, MOD_ROWS, tn), lambda l, n: (l, 0, n)),
        compiler_params=_params(("arbitrary", "arbitrary")),
        name="modulation",
    )(cond, mod_w, mod_b.reshape(DEPTH, 1, 6 * D_MODEL))
    return out.reshape(DEPTH, MOD_ROWS, 6, D_MODEL)


def _mod_spec(tm, rows_per_seq, first_row):
    def index(i, *_):
        return (first_row + (i * tm) // rows_per_seq if rows_per_seq else first_row, 0, 0)
    return pl.BlockSpec((None, 6, D_MODEL), index)


def _lru_in_kernel(x_ref, mod_ref, g_ref, w_ref, gate_ref, xr_ref):
    h = _mod_in(x_ref[...], g_ref[...], mod_ref[0:1, :], mod_ref[1:2, :])
    xb = _dot(h.astype(BF16), w_ref[...])
    gate_ref[...] = jax.nn.gelu(xb[:, :D_RNN]).astype(BF16)
    xr_ref[...] = xb[:, D_RNN:]


def _lru_in(x, mod, g, w_in, rows_per_seq, first_row):
    n = x.shape[0]
    tm = 512
    return pl.pallas_call(
        _lru_in_kernel,
        out_shape=(jax.ShapeDtypeStruct((n, D_RNN), BF16), jax.ShapeDtypeStruct((n, D_RNN), F32)),
        grid=(n // tm,),
        in_specs=[
            pl.BlockSpec((tm, D_MODEL), lambda i: (i, 0)),
            _mod_spec(tm, rows_per_seq, first_row),
            pl.BlockSpec((1, D_MODEL), lambda i: (0, 0)),
            pl.BlockSpec((D_MODEL, 2 * D_RNN), lambda i: (0, 0)),
        ],
        out_specs=(pl.BlockSpec((tm, D_RNN), lambda i: (i, 0)),
                   pl.BlockSpec((tm, D_RNN), lambda i: (i, 0))),
        compiler_params=_params(("arbitrary",)),
        name="lru_in",
    )(x, mod, g, w_in)


def _lru_core_kernel(xr_ref, gate_ref, h0_ref, cw_ref, cb_ref, gw_ref, gb_ref, lam_ref,
                     m_ref, hl_ref, xpad, a_f, u_f, a_b, u_b, *, seq, chunk):
    c = LRU_HALF
    xpad[0:CONV_PAD, :] = jnp.zeros((CONV_PAD, c), F32)
    xpad[CONV_PAD + seq:, :] = jnp.zeros((CONV_PAD, c), F32)
    xpad[CONV_PAD:CONV_PAD + seq, :] = xr_ref[...]
    lam = lam_ref[...]
    coef = -LRU_C * jax.nn.softplus(-lam)
    for r0 in range(0, seq, chunk):
        xr = cb_ref[...]
        for j in range(CONV_W):
            off = CONV_PAD - CONV_LEFT + j + r0
            xr = xr + xpad[off:off + chunk, :] * cw_ref[j:j + 1, :]
        g = _dot(xr.astype(BF16), gw_ref[...]) + gb_ref[...]
        for d, (a_s, u_s) in enumerate(((a_f, u_f), (a_b, u_b))):
            r = 0.5 * (jnp.tanh(0.5 * g[:, 2 * d * c:(2 * d + 1) * c]) + 1.0)
            i = 0.5 * (jnp.tanh(0.5 * g[:, (2 * d + 1) * c:(2 * d + 2) * c]) + 1.0)
            log_a = coef[d:d + 1, :] * r
            a = jnp.exp(log_a)
            a_s[r0:r0 + chunk, :] = a
            u_s[r0:r0 + chunk, :] = jnp.sqrt(-jnp.tanh(log_a) * (a * a + 1.0)) * (i * xr)

    def step(t, carry):
        hf, hb = carry
        tb = seq - 1 - t
        hf = a_f[pl.ds(t, 1), :] * hf + u_f[pl.ds(t, 1), :]
        u_f[pl.ds(t, 1), :] = hf
        hb = a_b[pl.ds(tb, 1), :] * hb + u_b[pl.ds(tb, 1), :]
        u_b[pl.ds(tb, 1), :] = hb
        return hf, hb

    hf, hb = lax.fori_loop(0, seq, step, (h0_ref[0:1, :], h0_ref[1:2, :]), unroll=8)
    hl_ref[0:1, :] = hf
    hl_ref[1:2, :] = hb
    m_ref[...] = ((u_f[...] + u_b[...]) * gate_ref[...].astype(F32)).astype(BF16)


def _lru_core(xr, gate, h0, conv_w, conv_b, gate_w, gate_b, lam, n_seq, seq):
    c = LRU_HALF
    chunk = min(seq, 256)
    kernel = functools.partial(_lru_core_kernel, seq=seq, chunk=chunk)
    return pl.pallas_call(
        kernel,
        out_shape=(jax.ShapeDtypeStruct((n_seq * seq, D_RNN), BF16),
                   jax.ShapeDtypeStruct((n_seq, 2, D_RNN), F32)),
        grid=(n_seq, 2),
        in_specs=[
            pl.BlockSpec((seq, c), lambda b, j: (b, j)),
            pl.BlockSpec((seq, c), lambda b, j: (b, j)),
            pl.BlockSpec((None, 2, c), lambda b, j: (b, 0, j)),
            pl.BlockSpec((CONV_W, c), lambda b, j: (0, j)),
            pl.BlockSpec((1, c), lambda b, j: (0, j)),
            pl.BlockSpec((None, c, 4 * c), lambda b, j: (j, 0, 0)),
            pl.BlockSpec((None, 1, 4 * c), lambda b, j: (j, 0, 0)),
            pl.BlockSpec((2, c), lambda b, j: (0, j)),
        ],
        out_specs=(pl.BlockSpec((seq, c), lambda b, j: (b, j)),
                   pl.BlockSpec((None, 2, c), lambda b, j: (b, 0, j))),
        scratch_shapes=[pltpu.VMEM((seq + 2 * CONV_PAD, c), F32)] + [pltpu.VMEM((seq, c), F32)] * 4,
        compiler_params=_params(("arbitrary", "arbitrary")),
        name="lru_core",
    )(xr, gate, h0, conv_w, conv_b, gate_w, gate_b, lam)


def _route(sel, scores):
    tm = sel.shape[1]
    io8 = lax.broadcasted_iota(jnp.int32, (GROUP_SIZE, tm), 0)
    blocks, gscore = [], []
    for g in range(N_GROUPS):
        blk = sel[g * GROUP_SIZE:(g + 1) * GROUP_SIZE, :]
        m1 = jnp.max(blk, axis=0, keepdims=True)
        first = jnp.min(jnp.where(blk == m1, io8, GROUP_SIZE), axis=0, keepdims=True)
        m2 = jnp.max(jnp.where(io8 == first, NEG_INF, blk), axis=0, keepdims=True)
        blocks.append(blk)
        gscore.append(m1 + m2)
    masked = []
    for g in range(N_GROUPS):
        rank = jnp.zeros((1, tm), jnp.int32)
        for o in range(N_GROUPS):
            if o == g:
                continue
            beats = (gscore[o] >= gscore[g]) if o < g else (gscore[o] > gscore[g])
            rank = rank + beats.astype(jnp.int32)
        masked.append(jnp.where(rank < TOPK_GROUPS, blocks[g], NEG_INF))
    v = jnp.concatenate(masked, axis=0)
    ioe = lax.broadcasted_iota(jnp.int32, (N_EXPERTS, tm), 0)
    chosen = jnp.zeros((N_EXPERTS, tm), F32)
    for _ in range(TOP_K):
        mx = jnp.max(v, axis=0, keepdims=True)
        first = jnp.min(jnp.where(v == mx, ioe, N_EXPERTS), axis=0, keepdims=True)
        pick = ioe == first
        chosen = jnp.where(pick, 1.0, chosen)
        v = jnp.where(pick, NEG_INF, v)
    wsel = chosen * scores
    comb = wsel / jnp.sum(wsel, axis=0, keepdims=True) * ROUTED_SCALE
    return jnp.where(chosen > 0.0, comb, -1.0)


def _rows_to_tiles(x, tmp):
    tm = x.shape[0]
    for c in range(TOK_SUB):
        tmp[pl.ds(c, tm, stride=TOK_SUB), :] = x[:, c * LANES:(c + 1) * LANES]
    return tmp[...].reshape(tm // 2, 2 * TOK_SUB, LANES).astype(BF16)


def _tiles_to_rows(tmp, tm):
    return jnp.concatenate([tmp[pl.ds(c, tm, stride=TOK_SUB), :] for c in range(TOK_SUB)], axis=1)


SECOND_HALF = -2 ** 31


def _token_code(t):
    return jnp.where((t & 1) == 1, (t >> 1) | SECOND_HALF, t >> 1)


def _load_token(pairs_ref, code):
    pair = pairs_ref[code & (2 ** 31 - 1)].astype(F32)
    second = jnp.broadcast_to(code, (TOK_SUB, LANES)) < 0
    return jnp.where(second, pair[TOK_SUB:], pair[:TOK_SUB])


def _mix_out_kernel(m_ref, w_ref, x_ref, mod_ref, g_ref, rt_ref, rb_ref, x1_ref, h2_ref, comb_ref, tmp):
    y = _dot(m_ref[...], w_ref[...])
    x1 = x_ref[...] + mod_ref[2:3, :] * _rms(y, g_ref[1:2, :])
    x1_ref[...] = x1
    h2 = _mod_in(x1, g_ref[2:3, :], mod_ref[3:4, :], mod_ref[4:5, :])
    h2_ref[...] = _rows_to_tiles(h2, tmp)
    logits = _dot_nt(rt_ref[...], h2, precision=lax.Precision.HIGHEST)
    scores = jax.nn.sigmoid(logits)
    comb_ref[...] = _route(scores + rb_ref[...], scores)


def _mix_out(m, w, x, mod, g, router_t, router_b, rows_per_seq, first_row):
    n, k = m.shape
    tm = 512
    return pl.pallas_call(
        _mix_out_kernel,
        out_shape=(jax.ShapeDtypeStruct((n, D_MODEL), F32),
                   jax.ShapeDtypeStruct((n // 2, 2 * TOK_SUB, LANES), BF16),
                   jax.ShapeDtypeStruct((N_EXPERTS, n), F32)),
        grid=(n // tm,),
        in_specs=[
            pl.BlockSpec((tm, k), lambda i: (i, 0)),
            pl.BlockSpec((k, D_MODEL), lambda i: (0, 0)),
            pl.BlockSpec((tm, D_MODEL), lambda i: (i, 0)),
            _mod_spec(tm, rows_per_seq, first_row),
            pl.BlockSpec((4, D_MODEL), lambda i: (0, 0)),
            pl.BlockSpec((N_EXPERTS, D_MODEL), lambda i: (0, 0)),
            pl.BlockSpec((N_EXPERTS, 1), lambda i: (0, 0)),
        ],
        out_specs=(pl.BlockSpec((tm, D_MODEL), lambda i: (i, 0)),
                   pl.BlockSpec((tm // 2, 2 * TOK_SUB, LANES), lambda i: (i, 0, 0)),
                   pl.BlockSpec((N_EXPERTS, tm), lambda i: (0, i))),
        scratch_shapes=[pltpu.VMEM((tm * TOK_SUB, LANES), F32)],
        compiler_params=_params(("arbitrary",)),
        name="mix_out",
    )(m, w, x, mod, g, router_t, router_b)


def _moe_plan_kernel(comb_ref, ek_ref, rk_ref, wk_ref, cnt_ref):
    c = comb_ref[...]
    n = c.shape[1]
    chosen = c >= 0.0
    ch = chosen.astype(BF16)
    tri = (lax.broadcasted_iota(jnp.int32, (n, n), 0) <= lax.broadcasted_iota(jnp.int32, (n, n), 1))
    incl = _dot(ch, tri.astype(BF16))
    rank = incl - ch.astype(F32)
    cnt_ref[...] = jnp.broadcast_to(incl[:, n - 1:n], cnt_ref.shape)
    low = (lax.broadcasted_iota(jnp.int32, (N_EXPERTS, N_EXPERTS), 1)
           < lax.broadcasted_iota(jnp.int32, (N_EXPERTS, N_EXPERTS), 0))
    slot = _dot(low.astype(BF16), ch)
    ioe = lax.broadcasted_iota(jnp.int32, c.shape, 0).astype(F32)
    for k in range(TOP_K):
        sel = jnp.logical_and(chosen, slot == float(k))
        ek_ref[k:k + 1, :] = jnp.sum(jnp.where(sel, ioe, 0.0), axis=0, keepdims=True).astype(jnp.int32)
        rk_ref[k:k + 1, :] = jnp.sum(jnp.where(sel, rank, 0.0), axis=0, keepdims=True).astype(jnp.int32)
        wk_ref[k:k + 1, :] = jnp.sum(jnp.where(sel, c, 0.0), axis=0, keepdims=True)


def _moe_plan(comb):
    n = comb.shape[1]
    nsb = n // MOE_TSB
    out_i = jax.ShapeDtypeStruct((TOP_K, n), jnp.int32)
    pair_spec = pl.BlockSpec((TOP_K, MOE_TSB), lambda s: (0, s))
    return pl.pallas_call(
        _moe_plan_kernel,
        out_shape=(out_i, out_i, jax.ShapeDtypeStruct((TOP_K, n), F32),
                   jax.ShapeDtypeStruct((N_EXPERTS, nsb * 128), F32)),
        grid=(nsb,),
        in_specs=[pl.BlockSpec((N_EXPERTS, MOE_TSB), lambda s: (0, s))],
        out_specs=(pair_spec, pair_spec, pair_spec, pl.BlockSpec((N_EXPERTS, 128), lambda s: (0, s))),
        compiler_params=_params(("arbitrary",)),
        name="moe_plan",
    )(comb)


def _moe_num_tiles(n_tok):
    rows = n_tok * TOP_K + N_EXPERTS * (n_tok // MOE_TSB) * (MOE_CHUNK_ALIGN - 1)
    return rows // MOE_TM + N_EXPERTS + 1


def _ceil_to(x, m):
    return (x + m - 1) // m * m


def _moe_layout(ek, rk, wk, cnt):
    n = ek.shape[1]
    nsb = n // MOE_TSB
    nt = _moe_num_tiles(n)
    n_es = cnt[:, ::128].astype(jnp.int32)
    c_al = _ceil_to(n_es, MOE_CHUNK_ALIGN)
    tiles_e = _ceil_to(jnp.sum(c_al, axis=1), MOE_TM) // MOE_TM
    tile0 = jnp.cumsum(tiles_e) - tiles_e
    cstart = tile0[:, None] * MOE_TM + jnp.cumsum(c_al, axis=1) - c_al
    npiece = _ceil_to(n_es, MOE_PIECE) // MOE_PIECE
    lbase = (jnp.cumsum(npiece, axis=0) - npiece) * MOE_PIECE
    dest, ldest = _moe_dest(ek, rk, cstart, lbase)
    dest_blocks = dest.reshape(TOP_K, nsb, MOE_TSB).transpose(1, 0, 2).reshape(-1)
    row_token = _moe_invert(dest_blocks, nt * MOE_TM)
    tile_expert = jnp.sum(jnp.arange(nt, dtype=jnp.int32)[:, None] >= tile0[None, :], axis=1) - 1
    wsigned = jnp.where((ldest & 1) == 1, -wk, wk)
    return dict(row_token=row_token,
                tile_expert=tile_expert.astype(jnp.int32), n_used=jnp.sum(tiles_e).reshape(1).astype(jnp.int32),
                cstart=cstart.T.reshape(-1), npiece=npiece.T.reshape(-1), lbase=lbase.T.reshape(-1),
                ltile=(ldest >> 1).T.reshape(-1), wk=wsigned.T.reshape(-1))


def _moe_dest_kernel(ek_ref, rk_ref, cs_ref, lb_ref, dest_ref, ldest_ref):
    ioe = lax.broadcasted_iota(jnp.int32, (N_EXPERTS, ek_ref.shape[1]), 0)
    cs = cs_ref[:, 0:1]
    lb = lb_ref[:, 0:1]
    for k in range(TOP_K):
        hit = ioe == ek_ref[k:k + 1, :]
        rk = rk_ref[k:k + 1, :]
        dest_ref[k:k + 1, :] = jnp.sum(jnp.where(hit, cs, 0.0), axis=0, keepdims=True).astype(jnp.int32) + rk
        ldest_ref[k:k + 1, :] = jnp.sum(jnp.where(hit, lb, 0.0), axis=0, keepdims=True).astype(jnp.int32) + rk


def _moe_dest(ek, rk, cstart, lbase):
    n = ek.shape[1]
    pair_spec = pl.BlockSpec((TOP_K, MOE_TSB), lambda s: (0, s))
    table_spec = pl.BlockSpec((N_EXPERTS, LANES), lambda s: (0, s))
    out = jax.ShapeDtypeStruct((TOP_K, n), jnp.int32)
    spread = lambda tab: jnp.repeat(tab.astype(F32), LANES, axis=1)
    return pl.pallas_call(
        _moe_dest_kernel,
        out_shape=(out, out),
        grid=(n // MOE_TSB,),
        in_specs=[pair_spec, pair_spec, table_spec, table_spec],
        out_specs=(pair_spec, pair_spec),
        compiler_params=_params(("arbitrary",)),
        name="moe_dest",
    )(ek, rk, spread(cstart), spread(lbase))


def _moe_invert_kernel(dest_hbm, zeros_hbm, code_ref, rt_ref, dsm, sem):
    s = pl.program_id(0)
    pairs = TOP_K * MOE_TSB

    @pl.when(s == 0)
    def _():
        fill = pltpu.make_async_copy(zeros_hbm, rt_ref, sem)
        fill.start()
        fill.wait()

    copy = pltpu.make_async_copy(dest_hbm.at[pl.ds(pl.multiple_of(s * pairs, pairs), pairs)], dsm, sem)
    copy.start()
    copy.wait()

    def per_token(t, carry):
        code = code_ref[s * MOE_TSB + t]
        for k in range(TOP_K):
            rt_ref[dsm[k * MOE_TSB + t]] = code
        return carry

    lax.fori_loop(0, MOE_TSB, per_token, 0, unroll=4)


def _moe_invert(dest_blocks, n_rows):
    pairs = TOP_K * MOE_TSB
    return pl.pallas_call(
        _moe_invert_kernel,
        out_shape=jax.ShapeDtypeStruct((n_rows,), jnp.int32),
        grid=(dest_blocks.shape[0] // pairs,),
        in_specs=[pl.BlockSpec(memory_space=pl.ANY), pl.BlockSpec(memory_space=pl.ANY),
                  pl.BlockSpec(memory_space=pltpu.SMEM)],
        out_specs=pl.BlockSpec(memory_space=pltpu.SMEM),
        scratch_shapes=[pltpu.SMEM((pairs,), jnp.int32), pltpu.SemaphoreType.DMA],
        compiler_params=_params(("arbitrary",)),
        name="moe_invert",
    )(dest_blocks, jnp.zeros((n_rows,), jnp.int32),
      _token_code(jnp.arange(dest_blocks.shape[0] // TOP_K, dtype=jnp.int32)))


def _moe_expert_kernel(texp_ref, nused_ref, ids_ref, xp_ref, wg_ref, wu_ref, wd_ref, ys_ref,
                       xs_even, xs_odd, tmp, wgb, wub, wdb):
    i = pl.program_id(0)

    def gather(tile, xs):
        for r in range(MOE_TM):
            xs[r * TOK_SUB:(r + 1) * TOK_SUB, :] = _load_token(xp_ref, ids_ref[tile * MOE_TM + r])

    @pl.when(i == 0)
    def _():
        gather(0, xs_even)

    def tile_body(cur, nxt):
        gather(i + 1, nxt)
        rows = MOE_TM // MOE_ROW_GROUPS
        for grp in range(MOE_ROW_GROUPS):
            tok = slice(grp * rows * TOK_SUB, (grp + 1) * rows * TOK_SUB)
            lhs = _tiles_to_rows(cur.at[tok], rows).astype(BF16)
            act = jax.nn.silu(_dot(lhs, wgb[...])) * _dot(lhs, wub[...])
            ys_ref[grp * rows // 2:(grp + 1) * rows // 2] = _rows_to_tiles(
                _dot(act.astype(BF16), wdb[...]), tmp.at[tok])

    @pl.when(i < nused_ref[0])
    def _():
        @pl.when(jnp.logical_or(i == 0, texp_ref[i] != texp_ref[jnp.maximum(i - 1, 0)]))
        def _():
            wgb[...] = wg_ref[...].astype(BF16)
            wub[...] = wu_ref[...].astype(BF16)
            wdb[...] = wd_ref[...].astype(BF16)

        @pl.when(i % 2 == 0)
        def _():
            tile_body(xs_even, xs_odd)

        @pl.when(i % 2 == 1)
        def _():
            tile_body(xs_odd, xs_even)

    @pl.when(i >= nused_ref[0])
    def _():
        ys_ref[...] = jnp.zeros(ys_ref.shape, BF16)


def _moe_experts(plan, xp, w_gate, w_up, w_down, layer):
    nt = plan["tile_expert"].shape[0]
    grid_spec = pltpu.PrefetchScalarGridSpec(
        num_scalar_prefetch=3,
        grid=(nt,),
        in_specs=[
            pl.BlockSpec(memory_space=pltpu.VMEM),
            pl.BlockSpec((None, None, D_MODEL, D_EXPERT), lambda i, te, nu, ids: (layer, te[i], 0, 0)),
            pl.BlockSpec((None, None, D_MODEL, D_EXPERT), lambda i, te, nu, ids: (layer, te[i], 0, 0)),
            pl.BlockSpec((None, None, D_EXPERT, D_MODEL), lambda i, te, nu, ids: (layer, te[i], 0, 0)),
        ],
        out_specs=pl.BlockSpec((MOE_TM // 2, 2 * TOK_SUB, LANES), lambda i, te, nu, ids: (i, 0, 0)),
        scratch_shapes=[pltpu.VMEM((MOE_TM * TOK_SUB, LANES), F32),
                        pltpu.VMEM((MOE_TM * TOK_SUB, LANES), F32),
                        pltpu.VMEM((MOE_TM * TOK_SUB, LANES), F32),
                        pltpu.VMEM((D_MODEL, D_EXPERT), BF16),
                        pltpu.VMEM((D_MODEL, D_EXPERT), BF16),
                        pltpu.VMEM((D_EXPERT, D_MODEL), BF16)],
    )
    return pl.pallas_call(
        _moe_expert_kernel,
        out_shape=jax.ShapeDtypeStruct((nt * MOE_TM // 2, 2 * TOK_SUB, LANES), BF16),
        grid_spec=grid_spec,
        compiler_params=_params(("arbitrary",)),
        name="moe_experts",
    )(plan["tile_expert"], plan["n_used"], plan["row_token"], xp, w_gate, w_up, w_down)


def _moe_combine_kernel(cstart_ref, npiece_ref, lbase_ref, ltile_ref, wk_ref, ys_hbm, xp_ref, sg_ref, su_ref,
                        sd_ref, x_ref, mod_ref, g_ref, o_ref, ysb, ybuf, npending, sems, *, block0):
    step = pl.program_id(0)
    slot = step % 2

    def piece_copy(src_row, dst_row, buf):
        return pltpu.make_async_copy(ys_hbm.at[pl.ds(src_row // 2, MOE_PIECE // 2)],
                                     ysb.at[buf, pl.ds(dst_row // 2, MOE_PIECE // 2)], sems.at[buf])

    def fetch(blk, buf):
        def per_expert(e, total):
            j = blk * N_EXPERTS + e
            src, dst, npc = cstart_ref[j], lbase_ref[j], npiece_ref[j]

            def per_piece(p, carry):
                piece_copy(pl.multiple_of(src + p * MOE_PIECE, MOE_CHUNK_ALIGN),
                           pl.multiple_of(dst + p * MOE_PIECE, MOE_PIECE), buf).start()
                return carry

            lax.fori_loop(0, npc, per_piece, 0)
            return total + npc

        npending[buf] = lax.fori_loop(0, N_EXPERTS, per_expert, 0)

    @pl.when(step == 0)
    def _():
        fetch(block0, 0)

    @pl.when(step + 1 < pl.num_programs(0))
    def _():
        fetch(block0 + step + 1, 1 - slot)

    ybuf[...] = xp_ref[...].astype(F32).reshape(MOE_TSB * TOK_SUB, LANES)
    lhs = _tiles_to_rows(ybuf, MOE_TSB).astype(BF16)
    act = jax.nn.silu(_dot(lhs, sg_ref[...].astype(BF16))) * _dot(lhs, su_ref[...].astype(BF16))
    o_ref[...] = _dot(act.astype(BF16), sd_ref[...].astype(BF16))

    def wait_one(p, carry):
        piece_copy(0, 0, slot).wait()
        return carry

    lax.fori_loop(0, npending[slot], wait_one, 0)

    t0 = step * MOE_TSB
    yblk = ysb.at[slot]

    def weighted_row(p):
        w = jnp.broadcast_to(wk_ref[p], (TOK_SUB, LANES))
        pair = yblk[ltile_ref[p]].astype(F32)
        second = lax.bitcast_convert_type(w, jnp.int32) < 0
        return jnp.abs(w) * jnp.where(second, pair[TOK_SUB:], pair[:TOK_SUB])

    def per_token(t, carry):
        p = (t0 + t) * TOP_K
        acc = weighted_row(p)
        for k in range(1, TOP_K):
            acc = acc + weighted_row(p + k)
        ybuf[pl.ds(pl.multiple_of(t * TOK_SUB, TOK_SUB), TOK_SUB), :] = acc
        return carry

    lax.fori_loop(0, MOE_TSB, per_token, 0, unroll=4)
    routed = _tiles_to_rows(ybuf, MOE_TSB)
    o_ref[...] = x_ref[...] + mod_ref[5:6, :] * _rms(routed + o_ref[...], g_ref[3:4, :])


def _moe_combine(plan, ltile, wk, block0, ys, xp, s_gate, s_up, s_down, layer, x1, mod, g, rows_per_seq,
                 first_row):
    n = x1.shape[0]
    cap = TOP_K * MOE_TSB + N_EXPERTS * MOE_PIECE
    grid_spec = pltpu.PrefetchScalarGridSpec(
        num_scalar_prefetch=5,
        grid=(n // MOE_TSB,),
        in_specs=[
            pl.BlockSpec(memory_space=pl.ANY),
            pl.BlockSpec((MOE_TSB // 2, 2 * TOK_SUB, LANES), lambda s, *_: (s, 0, 0)),
            pl.BlockSpec((None, D_MODEL, D_SHARED), lambda s, *_: (layer, 0, 0)),
            pl.BlockSpec((None, D_MODEL, D_SHARED), lambda s, *_: (layer, 0, 0)),
            pl.BlockSpec((None, D_SHARED, D_MODEL), lambda s, *_: (layer, 0, 0)),
            pl.BlockSpec((MOE_TSB, D_MODEL), lambda s, *_: (s, 0)),
            _mod_spec(MOE_TSB, rows_per_seq, first_row),
            pl.BlockSpec((4, D_MODEL), lambda s, *_: (0, 0)),
        ],
        out_specs=pl.BlockSpec((MOE_TSB, D_MODEL), lambda s, *_: (s, 0)),
        scratch_shapes=[pltpu.VMEM((2, cap // 2, 2 * TOK_SUB, LANES), BF16),
                        pltpu.VMEM((MOE_TSB * TOK_SUB, LANES), F32),
                        pltpu.SMEM((2,), jnp.int32),
                        pltpu.SemaphoreType.DMA((2,))],
    )
    return pl.pallas_call(
        functools.partial(_moe_combine_kernel, block0=block0),
        out_shape=jax.ShapeDtypeStruct((n, D_MODEL), F32),
        grid_spec=grid_spec,
        compiler_params=_params(("arbitrary",)),
        name="moe_combine",
    )(plan["cstart"], plan["npiece"], plan["lbase"], ltile, wk, ys, xp, s_gate, s_up, s_down, x1, mod, g)


def _rope(x, cos, sin_up, sin_dn):
    out = []
    for h in range(N_HEADS):
        xs = x[:, h * V_DIM:(h + 1) * V_DIM]
        up = pltpu.roll(xs, V_DIM - ROPE_PAIRS, 1)
        dn = pltpu.roll(xs, ROPE_PAIRS, 1)
        out.append(xs * cos + up * sin_up + dn * sin_dn)
    return jnp.concatenate(out, axis=1)


def _qkv_kernel(x_ref, mod_ref, g_ref, w_ref, *rest, rope):
    if rope:
        cos_ref, sup_ref, sdn_ref, q_ref, k_ref, v_ref = rest
    else:
        q_ref, k_ref, v_ref = rest
    h = _mod_in(x_ref[...], g_ref[...], mod_ref[0:1, :], mod_ref[1:2, :])
    qkv = _dot(h.astype(BF16), w_ref[...])
    q, k, v = qkv[:, :QK_W], qkv[:, QK_W:2 * QK_W], qkv[:, 2 * QK_W:]
    if rope:
        q = _rope(q, cos_ref[...], sup_ref[...], sdn_ref[...])
        k = _rope(k, cos_ref[...], sup_ref[...], sdn_ref[...])
    q_ref[...] = (q * HEAD_DIM ** -0.5).astype(q_ref.dtype)
    k_ref[...] = k.astype(k_ref.dtype)
    v_ref[...] = v.astype(v_ref.dtype)


def _qkv(x, mod, g, w_qkv, rows_per_seq, first_row, rope_tables, kv_dtype):
    n = x.shape[0]
    tm = 512
    rope = rope_tables is not None
    in_specs = [
        pl.BlockSpec((tm, D_MODEL), lambda i: (i, 0)),
        _mod_spec(tm, rows_per_seq, first_row),
        pl.BlockSpec((1, D_MODEL), lambda i: (0, 0)),
        pl.BlockSpec((D_MODEL, 3 * QK_W), lambda i: (0, 0)),
    ]
    args = [x, mod, g, w_qkv]
    if rope:
        tiles_per_seq = rows_per_seq // tm
        in_specs += [pl.BlockSpec((tm, V_DIM), lambda i: (i % tiles_per_seq, 0))] * 3
        args += list(rope_tables)
    return pl.pallas_call(
        functools.partial(_qkv_kernel, rope=rope),
        out_shape=(jax.ShapeDtypeStruct((n, QK_W), BF16),
                   jax.ShapeDtypeStruct((n, QK_W), kv_dtype),
                   jax.ShapeDtypeStruct((n, N_HEADS * V_DIM), kv_dtype)),
        grid=(n // tm,),
        in_specs=in_specs,
        out_specs=(pl.BlockSpec((tm, QK_W), lambda i: (i, 0)),) * 3,
        compiler_params=_params(("arbitrary",)),
        name="attn_qkv_rope" if rope else "attn_qkv",
    )(*args)


def _softmax_parts(s):
    e = jnp.exp(s - jnp.max(s, axis=-1, keepdims=True))
    return e, 1.0 / jnp.sum(e, axis=-1, keepdims=True)


def _attn_kernel(lp_ref, sub_ref, q_ref, k_ref, v_ref, o_ref, *, lam_init):
    lp = lp_ref[...]
    lam = (jnp.exp(jnp.sum(lp[0:1, :] * lp[1:2, :], axis=1, keepdims=True))
           - jnp.exp(jnp.sum(lp[2:3, :] * lp[3:4, :], axis=1, keepdims=True)) + lam_init)
    for h in range(N_HEADS):
        c0 = h * V_DIM
        q = q_ref[:, c0:c0 + V_DIM]
        k = k_ref[:, c0:c0 + V_DIM].astype(BF16)
        e1, r1 = _softmax_parts(_dot_nt(q[:, :HEAD_DIM], k[:, :HEAD_DIM]))
        e2, r2 = _softmax_parts(_dot_nt(q[:, HEAD_DIM:], k[:, HEAD_DIM:]))
        w = e1 * r1 - lam * (e2 * r2)
        o = _dot(w.astype(BF16), v_ref[:, c0:c0 + V_DIM].astype(BF16))
        o_ref[:, c0:c0 + V_DIM] = (_rms(o, sub_ref[...]) * (1.0 - lam_init)).astype(BF16)


def _attention(lp, subln, q, k, v, lam_init, tq):
    bsz, t = q.shape[:2]
    tk = k.shape[1]
    return pl.pallas_call(
        functools.partial(_attn_kernel, lam_init=lam_init),
        out_shape=jax.ShapeDtypeStruct((bsz, t, N_HEADS * V_DIM), BF16),
        grid=(bsz, t // tq),
        in_specs=[
            pl.BlockSpec((4, HEAD_DIM), lambda b, i: (0, 0)),
            pl.BlockSpec((1, V_DIM), lambda b, i: (0, 0)),
            pl.BlockSpec((None, tq, QK_W), lambda b, i: (b, i, 0)),
            pl.BlockSpec((None, tk, QK_W), lambda b, i: (b, 0, 0)),
            pl.BlockSpec((None, tk, N_HEADS * V_DIM), lambda b, i: (b, 0, 0)),
        ],
        out_specs=pl.BlockSpec((None, tq, N_HEADS * V_DIM), lambda b, i: (b, i, 0)),
        compiler_params=_params(("arbitrary", "arbitrary")),
        name="diff_attention",
    )(lp, subln, q, k, v)


def _lru_gate_layout(gate_w, gate_b):
    nb = LRU_BLOCKS // 2
    w = gate_w.reshape(2, 2, 2, nb, LRU_BLOCK, LRU_BLOCK)
    bd = jnp.einsum("dgsbio,bc->sbidgco", w, jnp.eye(nb, dtype=w.dtype))
    bd = bd.reshape(2, LRU_HALF, 4 * LRU_HALF).astype(BF16)
    b = gate_b.reshape(2, 2, 2, LRU_HALF).transpose(2, 0, 1, 3).reshape(2, 1, 4 * LRU_HALF)
    return bd, b


def _rope_tables(n):
    rows = n // GRID_W
    row = jnp.repeat(jnp.arange(rows, dtype=F32), GRID_W)
    col = jnp.tile(jnp.arange(GRID_W, dtype=F32), rows)
    freqs = ROPE_THETA ** (-jnp.arange(ROPE_PAIRS, dtype=F32) / ROPE_PAIRS)
    ar = row[:, None] * freqs
    ac = col[:, None] * freqs
    zero = jnp.zeros_like(ar)
    cos = jnp.concatenate([jnp.cos(ar), jnp.cos(ar), jnp.cos(ac), jnp.cos(ac)], axis=-1)
    sin_up = jnp.concatenate([-jnp.sin(ar), zero, -jnp.sin(ac), zero], axis=-1)
    sin_dn = jnp.concatenate([zero, jnp.sin(ar), zero, jnp.sin(ac)], axis=-1)
    return tuple(jnp.tile(t, (1, 2)) for t in (cos, sin_up, sin_dn))


def kernel(x_prompt, x_sample, c, state_lru, cache_k, cache_v, c_ctx, mod_w, mod_b, norm_g,
           lru_w_in, lru_conv_w, lru_conv_b, lru_gate_w, lru_gate_b, lru_lambda, lru_w_out,
           attn_w_qkv, attn_lambda, attn_subln, attn_w_o,
           moe_router, moe_router_bias, moe_w_gate, moe_w_up, moe_w_down,
           shared_w_gate, shared_w_up, shared_w_down):
    bp, tp = x_prompt.shape[:2]
    bs, ts = x_sample.shape[:2]
    past = cache_k.shape[2]
    cond = jnp.concatenate([c_ctx[None], c, jnp.zeros((MOD_ROWS - 1 - bs, D_MODEL), F32)], axis=0)
    mod = _modulation(cond, mod_w, mod_b)
    streams = [dict(x=x_prompt.reshape(bp * tp, D_MODEL), rps=0, row=0, b=bp, t=tp),
               dict(x=x_sample.reshape(bs * ts, D_MODEL), rps=ts, row=1, b=bs, t=ts)]
    new_lru, new_k, new_v = [], [], []
    for i in range(DEPTH):
        j = i // N_MIXERS
        g = norm_g[i]
        router_t = moe_router[i].T
        router_b = moe_router_bias[i].reshape(N_EXPERTS, 1)
        if i % N_MIXERS == 0:
            w_in = lru_w_in[j].astype(BF16)
            w_mix = lru_w_out[j].astype(BF16)
            gate_w, gate_b = _lru_gate_layout(lru_gate_w[j], lru_gate_b[j])
        else:
            lam_init = 0.8 - 0.6 * math.exp(-0.3 * i)
            w_qkv = attn_w_qkv[j].astype(BF16)
            w_mix = attn_w_o[j].astype(BF16)
            tables = _rope_tables(ts)
        for si, s in enumerate(streams):
            if i % N_MIXERS == 0:
                gate, xr = _lru_in(s["x"], mod[i], g[0:1], w_in, s["rps"], s["row"])
                h0 = jnp.zeros((bp, 2, D_RNN), F32) if si == 0 else state_lru[:, j]
                m, h_last = _lru_core(xr, gate, h0, lru_conv_w[j], lru_conv_b[j][None], gate_w, gate_b,
                                      lru_lambda[j], s["b"], s["t"])
                if si == 0:
                    new_lru.append(h_last)
            else:
                if si == 0:
                    q, k, v = _qkv(s["x"], mod[i], g[0:1], w_qkv, s["rps"], s["row"], None, F32)
                    new_k.append(k.reshape(bp, tp, N_HEADS, 2 * HEAD_DIM))
                    new_v.append(v.reshape(bp, tp, N_HEADS, V_DIM))
                    k3 = k.reshape(bp, tp, QK_W)
                    v3 = v.reshape(bp, tp, N_HEADS * V_DIM)
                    tq = tp
                else:
                    q, k, v = _qkv(s["x"], mod[i], g[0:1], w_qkv, s["rps"], s["row"], tables, BF16)
                    k3 = jnp.concatenate([cache_k[:, j].reshape(bs, past, QK_W).astype(BF16),
                                          k.reshape(bs, ts, QK_W)], axis=1)
                    v3 = jnp.concatenate([cache_v[:, j].reshape(bs, past, N_HEADS * V_DIM).astype(BF16),
                                          v.reshape(bs, ts, N_HEADS * V_DIM)], axis=1)
                    tq = 512
                o = _attention(attn_lambda[j], attn_subln[j][None], q.reshape(s["b"], s["t"], QK_W),
                               k3, v3, lam_init, tq)
                m = o.reshape(s["b"] * s["t"], N_HEADS * V_DIM)
            s["x1"], s["xp"], s["comb"] = _mix_out(m, w_mix, s["x"], mod[i], g, router_t, router_b,
                                                   s["rps"], s["row"])
        xp = jnp.concatenate([s["xp"] for s in streams], axis=0)
        ek, rk, wk, cnt = _moe_plan(jnp.concatenate([s["comb"] for s in streams], axis=1))
        plan = _moe_layout(ek, rk, wk, cnt)
        ys = _moe_experts(plan, xp, moe_w_gate, moe_w_up, moe_w_down, i)
        tok0 = 0
        for s in streams:
            n = s["x1"].shape[0]
            pairs = slice(tok0 * TOP_K, (tok0 + n) * TOP_K)
            s["x"] = _moe_combine(plan, plan["ltile"][pairs], plan["wk"][pairs], tok0 // MOE_TSB, ys, s["xp"],
                                  shared_w_gate, shared_w_up,
                                  shared_w_down, i, s["x1"], mod[i], g, s["rps"], s["row"])
            tok0 += n
    return (streams[0]["x"].reshape(bp, tp, D_MODEL),
            streams[1]["x"].reshape(bs, ts, D_MODEL),
            jnp.stack(new_lru, axis=1),
            jnp.stack(new_k, axis=1),
            jnp.stack(new_v, axis=1))
```

```python
import functools
import math

import jax
import jax.numpy as jnp
from jax import lax
from jax.experimental import pallas as pl
from jax.experimental.pallas import tpu as pltpu

D_MODEL = 1024
DEPTH = 2
N_MIXERS = 2
GRID_W = 64
EPS = 1e-6
D_RNN = 1280
LRU_BLOCKS = 16
LRU_BLOCK = D_RNN // LRU_BLOCKS
CONV_W = 4
CONV_LEFT = 2
LRU_C = 8.0
N_HEADS = 8
HEAD_DIM = 64
V_DIM = 2 * HEAD_DIM
QK_W = N_HEADS * 2 * HEAD_DIM
ROPE_PAIRS = HEAD_DIM // 4
ROPE_THETA = 10000.0
N_EXPERTS = 64
N_GROUPS = 8
GROUP_SIZE = N_EXPERTS // N_GROUPS
TOPK_GROUPS = 4
TOP_K = 8
D_EXPERT = 256
D_SHARED = 256
ROUTED_SCALE = 2.5

MOD_ROWS = 8
LRU_HALF = D_RNN // 2
CONV_PAD = 8
MOE_TM = 512
MOE_ROW_GROUPS = 1
MOE_TSB = 512
MOE_CHUNK_ALIGN = 8
MOE_PIECE = 64
LANES = 128
LRU_STEP_SUB = 8
TOK_SUB = D_MODEL // LANES
VMEM_LIMIT = 56 * 1024 * 1024
BF16 = jnp.bfloat16
F32 = jnp.float32
NEG_INF = float("-inf")


def _params(sem):
    return pltpu.CompilerParams(dimension_semantics=sem, vmem_limit_bytes=VMEM_LIMIT)


def _rms(x, g):
    return x * lax.rsqrt(jnp.mean(x * x, axis=-1, keepdims=True) + EPS) * g


def _mod_in(x, g, shift, scale):
    return _rms(x, g) * (1.0 + scale) + shift


def _dot(a, b):
    return jnp.dot(a, b, preferred_element_type=F32)


def _dot_nt(a, b, precision=None):
    return lax.dot_general(a, b, (((1,), (1,)), ((), ())), precision=precision,
                           preferred_element_type=F32)


def _mod_kernel(cond_ref, w_ref, b_ref, o_ref):
    cnd = cond_ref[...]
    s = cnd * jax.nn.sigmoid(cnd)
    o_ref[...] = jnp.dot(s, w_ref[...], precision=lax.Precision.HIGHEST,
                         preferred_element_type=F32) + b_ref[...]


def _modulation(cond, mod_w, mod_b):
    tn = 1536
    out = pl.pallas_call(
        _mod_kernel,
        out_shape=jax.ShapeDtypeStruct((DEPTH, MOD_ROWS, 6 * D_MODEL), F32),
        grid=(DEPTH, 6 * D_MODEL // tn),
        in_specs=[
            pl.BlockSpec((MOD_ROWS, D_MODEL), lambda l, n: (0, 0)),
            pl.BlockSpec((None, D_MODEL, tn), lambda l, n: (l, 0, n)),
            pl.BlockSpec((None, 1, tn), lambda l, n: (l, 0, n)),
        ],
        out_specs=pl.BlockSpec((None, MOD_ROWS, tn), lambda l, n: (l, 0, n)),
        compiler_params=_params(("arbitrary", "arbitrary")),
        name="modulation",
    )(cond, mod_w, mod_b.reshape(DEPTH, 1, 6 * D_MODEL))
    return out.reshape(DEPTH, MOD_ROWS, 6, D_MODEL)


def _mod_spec(tm, rows_per_seq, first_row):
    def index(i, *_):
        return (first_row + (i * tm) // rows_per_seq if rows_per_seq else first_row, 0, 0)
    return pl.BlockSpec((None, 6, D_MODEL), index)


def _lru_in_kernel(x_ref, mod_ref, g_ref, w_ref, gate_ref, xr_ref):
    h = _mod_in(x_ref[...], g_ref[...], mod_ref[0:1, :], mod_ref[1:2, :])
    xb = _dot(h.astype(BF16), w_ref[...])
    gate_ref[...] = jax.nn.gelu(xb[:, :D_RNN]).astype(BF16)
    xr_ref[...] = xb[:, D_RNN:]


def _lru_in(x, mod, g, w_in, rows_per_seq, first_row):
    n = x.shape[0]
    tm = 512
    return pl.pallas_call(
        _lru_in_kernel,
        out_shape=(jax.ShapeDtypeStruct((n, D_RNN), BF16), jax.ShapeDtypeStruct((n, D_RNN), F32)),
        grid=(n // tm,),
        in_specs=[
            pl.BlockSpec((tm, D_MODEL), lambda i: (i, 0)),
            _mod_spec(tm, rows_per_seq, first_row),
            pl.BlockSpec((1, D_MODEL), lambda i: (0, 0)),
            pl.BlockSpec((D_MODEL, 2 * D_RNN), lambda i: (0, 0)),
        ],
        out_specs=(pl.BlockSpec((tm, D_RNN), lambda i: (i, 0)),
                   pl.BlockSpec((tm, D_RNN), lambda i: (i, 0))),
        compiler_params=_params(("arbitrary",)),
        name="lru_in",
    )(x, mod, g, w_in)


def _lru_core_kernel(xr_ref, gate_ref, h0_ref, cw_ref, cb_ref, gw_ref, gb_ref, lam_ref,
                     m_ref, hl_ref, xpad, a_f, u_f, a_b, u_b, hs, *, seq, chunk):
    c = LRU_HALF
    nslab = c // LANES

    @pl.when(jnp.logical_and(pl.program_id(0) == 0, pl.program_id(1) == 0))
    def _():
        for buf in (a_f, u_f, a_b, u_b, hs):
            buf[...] = jnp.zeros(buf.shape, F32)

    xpad[0:CONV_PAD, :] = jnp.zeros((CONV_PAD, c), F32)
    xpad[CONV_PAD + seq:, :] = jnp.zeros((CONV_PAD, c), F32)
    xpad[CONV_PAD:CONV_PAD + seq, :] = xr_ref[...]
    lam = lam_ref[...]
    coef = -LRU_C * jax.nn.softplus(-lam)
    for r0 in range(0, seq, chunk):
        xr = cb_ref[...]
        for j in range(CONV_W):
            off = CONV_PAD - CONV_LEFT + j + r0
            xr = xr + xpad[off:off + chunk, :] * cw_ref[j:j + 1, :]
        g = _dot(xr.astype(BF16), gw_ref[...]) + gb_ref[...]
        for d, (a_s, u_s) in enumerate(((a_f, u_f), (a_b, u_b))):
            r = 0.5 * (jnp.tanh(0.5 * g[:, 2 * d * c:(2 * d + 1) * c]) + 1.0)
            i = 0.5 * (jnp.tanh(0.5 * g[:, (2 * d + 1) * c:(2 * d + 2) * c]) + 1.0)
            log_a = coef[d:d + 1, :] * r
            a = jnp.exp(log_a)
            u = jnp.sqrt(-jnp.tanh(log_a) * (a * a + 1.0)) * (i * xr)
            for s in range(nslab):
                rows = pl.ds(r0 * LRU_STEP_SUB + s, chunk, stride=LRU_STEP_SUB)
                a_s[rows, :] = a[:, s * LANES:(s + 1) * LANES]
                u_s[rows, :] = u[:, s * LANES:(s + 1) * LANES]

    for d in range(2):
        for s in range(nslab):
            hs[d * LRU_STEP_SUB + s:d * LRU_STEP_SUB + s + 1, :] = h0_ref[d:d + 1, s * LANES:(s + 1) * LANES]

    def step(t, carry):
        hf, hb = carry
        rf = pl.ds(pl.multiple_of(t * LRU_STEP_SUB, LRU_STEP_SUB), LRU_STEP_SUB)
        rb = pl.ds(pl.multiple_of((seq - 1 - t) * LRU_STEP_SUB, LRU_STEP_SUB), LRU_STEP_SUB)
        hf = a_f[rf, :] * hf + u_f[rf, :]
        u_f[rf, :] = hf
        hb = a_b[rb, :] * hb + u_b[rb, :]
        u_b[rb, :] = hb
        return hf, hb

    hf, hb = lax.fori_loop(0, seq, step, (hs[0:LRU_STEP_SUB, :], hs[LRU_STEP_SUB:, :]), unroll=8)
    hs[0:LRU_STEP_SUB, :] = hf
    hs[LRU_STEP_SUB:, :] = hb
    for d in range(2):
        for s in range(nslab):
            hl_ref[d:d + 1, s * LANES:(s + 1) * LANES] = hs[d * LRU_STEP_SUB + s:d * LRU_STEP_SUB + s + 1, :]
    y = jnp.concatenate([u_f[pl.ds(s, seq, stride=LRU_STEP_SUB), :] + u_b[pl.ds(s, seq, stride=LRU_STEP_SUB), :]
                         for s in range(nslab)], axis=1)
    m_ref[...] = (y * gate_ref[...].astype(F32)).astype(BF16)


def _lru_core(xr, gate, h0, conv_w, conv_b, gate_w, gate_b, lam, n_seq, seq):
    c = LRU_HALF
    chunk = min(seq, 256)
    kernel = functools.partial(_lru_core_kernel, seq=seq, chunk=chunk)
    return pl.pallas_call(
        kernel,
        out_shape=(jax.ShapeDtypeStruct((n_seq * seq, D_RNN), BF16),
                   jax.ShapeDtypeStruct((n_seq, 2, D_RNN), F32)),
        grid=(n_seq, 2),
        in_specs=[
            pl.BlockSpec((seq, c), lambda b, j: (b, j)),
            pl.BlockSpec((seq, c), lambda b, j: (b, j)),
            pl.BlockSpec((None, 2, c), lambda b, j: (b, 0, j)),
            pl.BlockSpec((CONV_W, c), lambda b, j: (0, j)),
            pl.BlockSpec((1, c), lambda b, j: (0, j)),
            pl.BlockSpec((None, c, 4 * c), lambda b, j: (j, 0, 0)),
            pl.BlockSpec((None, 1, 4 * c), lambda b, j: (j, 0, 0)),
            pl.BlockSpec((2, c), lambda b, j: (0, j)),
        ],
        out_specs=(pl.BlockSpec((seq, c), lambda b, j: (b, j)),
                   pl.BlockSpec((None, 2, c), lambda b, j: (b, 0, j))),
        scratch_shapes=([pltpu.VMEM((seq + 2 * CONV_PAD, c), F32)]
                        + [pltpu.VMEM((seq * LRU_STEP_SUB, LANES), F32)] * 4
                        + [pltpu.VMEM((2 * LRU_STEP_SUB, LANES), F32)]),
        compiler_params=_params(("arbitrary", "arbitrary")),
        name="lru_core",
    )(xr, gate, h0, conv_w, conv_b, gate_w, gate_b, lam)


def _route(sel, scores):
    tm = sel.shape[1]
    io8 = lax.broadcasted_iota(jnp.int32, (GROUP_SIZE, tm), 0)
    blocks, gscore = [], []
    for g in range(N_GROUPS):
        blk = sel[g * GROUP_SIZE:(g + 1) * GROUP_SIZE, :]
        m1 = jnp.max(blk, axis=0, keepdims=True)
        first = jnp.min(jnp.where(blk == m1, io8, GROUP_SIZE), axis=0, keepdims=True)
        m2 = jnp.max(jnp.where(io8 == first, NEG_INF, blk), axis=0, keepdims=True)
        blocks.append(blk)
        gscore.append(m1 + m2)
    masked = []
    for g in range(N_GROUPS):
        rank = jnp.zeros((1, tm), jnp.int32)
        for o in range(N_GROUPS):
            if o == g:
                continue
            beats = (gscore[o] >= gscore[g]) if o < g else (gscore[o] > gscore[g])
            rank = rank + beats.astype(jnp.int32)
        masked.append(jnp.where(rank < TOPK_GROUPS, blocks[g], NEG_INF))
    v = jnp.concatenate(masked, axis=0)
    ioe = lax.broadcasted_iota(jnp.int32, (N_EXPERTS, tm), 0)
    chosen = jnp.zeros((N_EXPERTS, tm), F32)
    for _ in range(TOP_K):
        mx = jnp.max(v, axis=0, keepdims=True)
        first = jnp.min(jnp.where(v == mx, ioe, N_EXPERTS), axis=0, keepdims=True)
        pick = ioe == first
        chosen = jnp.where(pick, 1.0, chosen)
        v = jnp.where(pick, NEG_INF, v)
    wsel = chosen * scores
    comb = wsel / jnp.sum(wsel, axis=0, keepdims=True) * ROUTED_SCALE
    return jnp.where(chosen > 0.0, comb, -1.0)


def _rows_to_tiles(x, tmp):
    tm = x.shape[0]
    for c in range(TOK_SUB):
        tmp[pl.ds(c, tm, stride=TOK_SUB), :] = x[:, c * LANES:(c + 1) * LANES]
    return tmp[...].reshape(tm // 2, 2 * TOK_SUB, LANES).astype(BF16)


def _tiles_to_rows(tmp, tm):
    return jnp.concatenate([tmp[pl.ds(c, tm, stride=TOK_SUB), :] for c in range(TOK_SUB)], axis=1)


SECOND_HALF = -2 ** 31


def _token_code(t):
    return jnp.where((t & 1) == 1, (t >> 1) | SECOND_HALF, t >> 1)


def _load_token(pairs_ref, code):
    pair = pairs_ref[code & (2 ** 31 - 1)].astype(F32)
    second = jnp.broadcast_to(code, (TOK_SUB, LANES)) < 0
    return jnp.where(second, pair[TOK_SUB:], pair[:TOK_SUB])


def _mix_out_kernel(m_ref, w_ref, x_ref, mod_ref, g_ref, rt_ref, rb_ref, x1_ref, h2_ref, comb_ref, tmp):
    y = _dot(m_ref[...], w_ref[...])
    x1 = x_ref[...] + mod_ref[2:3, :] * _rms(y, g_ref[1:2, :])
    x1_ref[...] = x1
    h2 = _mod_in(x1, g_ref[2:3, :], mod_ref[3:4, :], mod_ref[4:5, :])
    h2_ref[...] = _rows_to_tiles(h2, tmp)
    logits = _dot_nt(rt_ref[...], h2, precision=lax.Precision.HIGHEST)
    scores = jax.nn.sigmoid(logits)
    comb_ref[...] = _route(scores + rb_ref[...], scores)


def _mix_out(m, w, x, mod, g, router_t, router_b, rows_per_seq, first_row):
    n, k = m.shape
    tm = 512
    return pl.pallas_call(
        _mix_out_kernel,
        out_shape=(jax.ShapeDtypeStruct((n, D_MODEL), F32),
                   jax.ShapeDtypeStruct((n // 2, 2 * TOK_SUB, LANES), BF16),
                   jax.ShapeDtypeStruct((N_EXPERTS, n), F32)),
        grid=(n // tm,),
        in_specs=[
            pl.BlockSpec((tm, k), lambda i: (i, 0)),
            pl.BlockSpec((k, D_MODEL), lambda i: (0, 0)),
            pl.BlockSpec((tm, D_MODEL), lambda i: (i, 0)),
            _mod_spec(tm, rows_per_seq, first_row),
            pl.BlockSpec((4, D_MODEL), lambda i: (0, 0)),
            pl.BlockSpec((N_EXPERTS, D_MODEL), lambda i: (0, 0)),
            pl.BlockSpec((N_EXPERTS, 1), lambda i: (0, 0)),
        ],
        out_specs=(pl.BlockSpec((tm, D_MODEL), lambda i: (i, 0)),
                   pl.BlockSpec((tm // 2, 2 * TOK_SUB, LANES), lambda i: (i, 0, 0)),
                   pl.BlockSpec((N_EXPERTS, tm), lambda i: (0, i))),
        scratch_shapes=[pltpu.VMEM((tm * TOK_SUB, LANES), F32)],
        compiler_params=_params(("arbitrary",)),
        name="mix_out",
    )(m, w, x, mod, g, router_t, router_b)


def _moe_plan_kernel(comb_ref, ek_ref, rk_ref, wk_ref, cnt_ref):
    c = comb_ref[...]
    n = c.shape[1]
    chosen = c >= 0.0
    ch = chosen.astype(BF16)
    tri = (lax.broadcasted_iota(jnp.int32, (n, n), 0) <= lax.broadcasted_iota(jnp.int32, (n, n), 1))
    incl = _dot(ch, tri.astype(BF16))
    rank = incl - ch.astype(F32)
    cnt_ref[...] = jnp.broadcast_to(incl[:, n - 1:n], cnt_ref.shape)
    low = (lax.broadcasted_iota(jnp.int32, (N_EXPERTS, N_EXPERTS), 1)
           < lax.broadcasted_iota(jnp.int32, (N_EXPERTS, N_EXPERTS), 0))
    slot = _dot(low.astype(BF16), ch)
    ioe = lax.broadcasted_iota(jnp.int32, c.shape, 0).astype(F32)
    for k in range(TOP_K):
        sel = jnp.logical_and(chosen, slot == float(k))
        ek_ref[k:k + 1, :] = jnp.sum(jnp.where(sel, ioe, 0.0), axis=0, keepdims=True).astype(jnp.int32)
        rk_ref[k:k + 1, :] = jnp.sum(jnp.where(sel, rank, 0.0), axis=0, keepdims=True).astype(jnp.int32)
        wk_ref[k:k + 1, :] = jnp.sum(jnp.where(sel, c, 0.0), axis=0, keepdims=True)


def _moe_plan(comb):
    n = comb.shape[1]
    nsb = n // MOE_TSB
    out_i = jax.ShapeDtypeStruct((TOP_K, n), jnp.int32)
    pair_spec = pl.BlockSpec((TOP_K, MOE_TSB), lambda s: (0, s))
    return pl.pallas_call(
        _moe_plan_kernel,
        out_shape=(out_i, out_i, jax.ShapeDtypeStruct((TOP_K, n), F32),
                   jax.ShapeDtypeStruct((N_EXPERTS, nsb * 128), F32)),
        grid=(nsb,),
        in_specs=[pl.BlockSpec((N_EXPERTS, MOE_TSB), lambda s: (0, s))],
        out_specs=(pair_spec, pair_spec, pair_spec, pl.BlockSpec((N_EXPERTS, 128), lambda s: (0, s))),
        compiler_params=_params(("arbitrary",)),
        name="moe_plan",
    )(comb)


def _moe_num_tiles(n_tok):
    rows = n_tok * TOP_K + N_EXPERTS * (n_tok // MOE_TSB) * (MOE_CHUNK_ALIGN - 1)
    return rows // MOE_TM + N_EXPERTS + 1


def _ceil_to(x, m):
    return (x + m - 1) // m * m


def _moe_layout(ek, rk, wk, cnt):
    n = ek.shape[1]
    nsb = n // MOE_TSB
    nt = _moe_num_tiles(n)
    n_es = cnt[:, ::128].astype(jnp.int32)
    c_al = _ceil_to(n_es, MOE_CHUNK_ALIGN)
    tiles_e = _ceil_to(jnp.sum(c_al, axis=1), MOE_TM) // MOE_TM
    tile0 = jnp.cumsum(tiles_e) - tiles_e
    cstart = tile0[:, None] * MOE_TM + jnp.cumsum(c_al, axis=1) - c_al
    npiece = _ceil_to(n_es, MOE_PIECE) // MOE_PIECE
    lbase = (jnp.cumsum(npiece, axis=0) - npiece) * MOE_PIECE
    dest, ldest = _moe_dest(ek, rk, cstart, lbase)
    dest_blocks = dest.reshape(TOP_K, nsb, MOE_TSB).transpose(1, 0, 2).reshape(-1)
    row_token = _moe_invert(dest_blocks, nt * MOE_TM)
    tile_expert = jnp.sum(jnp.arange(nt, dtype=jnp.int32)[:, None] >= tile0[None, :], axis=1) - 1
    wsigned = jnp.where((ldest & 1) == 1, -wk, wk)
    return dict(row_token=row_token,
                tile_expert=tile_expert.astype(jnp.int32), n_used=jnp.sum(tiles_e).reshape(1).astype(jnp.int32),
                cstart=cstart.T.reshape(-1), npiece=npiece.T.reshape(-1), lbase=lbase.T.reshape(-1),
                ltile=(ldest >> 1).T.reshape(-1), wk=wsigned.T.reshape(-1))


def _moe_dest_kernel(ek_ref, rk_ref, cs_ref, lb_ref, dest_ref, ldest_ref):
    ioe = lax.broadcasted_iota(jnp.int32, (N_EXPERTS, ek_ref.shape[1]), 0)
    cs = cs_ref[:, 0:1]
    lb = lb_ref[:, 0:1]
    for k in range(TOP_K):
        hit = ioe == ek_ref[k:k + 1, :]
        rk = rk_ref[k:k + 1, :]
        dest_ref[k:k + 1, :] = jnp.sum(jnp.where(hit, cs, 0.0), axis=0, keepdims=True).astype(jnp.int32) + rk
        ldest_ref[k:k + 1, :] = jnp.sum(jnp.where(hit, lb, 0.0), axis=0, keepdims=True).astype(jnp.int32) + rk


def _moe_dest(ek, rk, cstart, lbase):
    n = ek.shape[1]
    pair_spec = pl.BlockSpec((TOP_K, MOE_TSB), lambda s: (0, s))
    table_spec = pl.BlockSpec((N_EXPERTS, LANES), lambda s: (0, s))
    out = jax.ShapeDtypeStruct((TOP_K, n), jnp.int32)
    spread = lambda tab: jnp.repeat(tab.astype(F32), LANES, axis=1)
    return pl.pallas_call(
        _moe_dest_kernel,
        out_shape=(out, out),
        grid=(n // MOE_TSB,),
        in_specs=[pair_spec, pair_spec, table_spec, table_spec],
        out_specs=(pair_spec, pair_spec),
        compiler_params=_params(("arbitrary",)),
        name="moe_dest",
    )(ek, rk, spread(cstart), spread(lbase))


def _moe_invert_kernel(dest_hbm, zeros_hbm, code_ref, rt_ref, dsm, sem):
    s = pl.program_id(0)
    pairs = TOP_K * MOE_TSB

    @pl.when(s == 0)
    def _():
        fill = pltpu.make_async_copy(zeros_hbm, rt_ref, sem)
        fill.start()
        fill.wait()

    copy = pltpu.make_async_copy(dest_hbm.at[pl.ds(pl.multiple_of(s * pairs, pairs), pairs)], dsm, sem)
    copy.start()
    copy.wait()

    def per_token(t, carry):
        code = code_ref[s * MOE_TSB + t]
        for k in range(TOP_K):
            rt_ref[dsm[k * MOE_TSB + t]] = code
        return carry

    lax.fori_loop(0, MOE_TSB, per_token, 0, unroll=4)


def _moe_invert(dest_blocks, n_rows):
    pairs = TOP_K * MOE_TSB
    return pl.pallas_call(
        _moe_invert_kernel,
        out_shape=jax.ShapeDtypeStruct((n_rows,), jnp.int32),
        grid=(dest_blocks.shape[0] // pairs,),
        in_specs=[pl.BlockSpec(memory_space=pl.ANY), pl.BlockSpec(memory_space=pl.ANY),
                  pl.BlockSpec(memory_space=pltpu.SMEM)],
        out_specs=pl.BlockSpec(memory_space=pltpu.SMEM),
        scratch_shapes=[pltpu.SMEM((pairs,), jnp.int32), pltpu.SemaphoreType.DMA],
        compiler_params=_params(("arbitrary",)),
        name="moe_invert",
    )(dest_blocks, jnp.zeros((n_rows,), jnp.int32),
      _token_code(jnp.arange(dest_blocks.shape[0] // TOP_K, dtype=jnp.int32)))


def _moe_expert_kernel(texp_ref, nused_ref, ids_ref, xp_ref, wg_ref, wu_ref, wd_ref, ys_ref,
                       xs_even, xs_odd, tmp, wgb, wub, wdb):
    i = pl.program_id(0)

    def gather(tile, xs):
        for r in range(MOE_TM):
            xs[r * TOK_SUB:(r + 1) * TOK_SUB, :] = _load_token(xp_ref, ids_ref[tile * MOE_TM + r])

    @pl.when(i == 0)
    def _():
        gather(0, xs_even)

    def tile_body(cur, nxt):
        gather(i + 1, nxt)
        rows = MOE_TM // MOE_ROW_GROUPS
        for grp in range(MOE_ROW_GROUPS):
            tok = slice(grp * rows * TOK_SUB, (grp + 1) * rows * TOK_SUB)
            lhs = _tiles_to_rows(cur.at[tok], rows).astype(BF16)
            act = jax.nn.silu(_dot(lhs, wgb[...])) * _dot(lhs, wub[...])
            ys_ref[grp * rows // 2:(grp + 1) * rows // 2] = _rows_to_tiles(
                _dot(act.astype(BF16), wdb[...]), tmp.at[tok])

    @pl.when(i < nused_ref[0])
    def _():
        @pl.when(jnp.logical_or(i == 0, texp_ref[i] != texp_ref[jnp.maximum(i - 1, 0)]))
        def _():
            wgb[...] = wg_ref[...].astype(BF16)
            wub[...] = wu_ref[...].astype(BF16)
            wdb[...] = wd_ref[...].astype(BF16)

        @pl.when(i % 2 == 0)
        def _():
            tile_body(xs_even, xs_odd)

        @pl.when(i % 2 == 1)
        def _():
            tile_body(xs_odd, xs_even)

    @pl.when(i >= nused_ref[0])
    def _():
        ys_ref[...] = jnp.zeros(ys_ref.shape, BF16)


def _moe_experts(plan, xp, w_gate, w_up, w_down, layer):
    nt = plan["tile_expert"].shape[0]
    grid_spec = pltpu.PrefetchScalarGridSpec(
        num_scalar_prefetch=3,
        grid=(nt,),
        in_specs=[
            pl.BlockSpec(memory_space=pltpu.VMEM),
            pl.BlockSpec((None, None, D_MODEL, D_EXPERT), lambda i, te, nu, ids: (layer, te[i], 0, 0)),
            pl.BlockSpec((None, None, D_MODEL, D_EXPERT), lambda i, te, nu, ids: (layer, te[i], 0, 0)),
            pl.BlockSpec((None, None, D_EXPERT, D_MODEL), lambda i, te, nu, ids: (layer, te[i], 0, 0)),
        ],
        out_specs=pl.BlockSpec((MOE_TM // 2, 2 * TOK_SUB, LANES), lambda i, te, nu, ids: (i, 0, 0)),
        scratch_shapes=[pltpu.VMEM((MOE_TM * TOK_SUB, LANES), F32),
                        pltpu.VMEM((MOE_TM * TOK_SUB, LANES), F32),
                        pltpu.VMEM((MOE_TM * TOK_SUB, LANES), F32),
                        pltpu.VMEM((D_MODEL, D_EXPERT), BF16),
                        pltpu.VMEM((D_MODEL, D_EXPERT), BF16),
                        pltpu.VMEM((D_EXPERT, D_MODEL), BF16)],
    )
    return pl.pallas_call(
        _moe_expert_kernel,
        out_shape=jax.ShapeDtypeStruct((nt * MOE_TM // 2, 2 * TOK_SUB, LANES), BF16),
        grid_spec=grid_spec,
        compiler_params=_params(("arbitrary",)),
        name="moe_experts",
    )(plan["tile_expert"], plan["n_used"], plan["row_token"], xp, w_gate, w_up, w_down)


def _moe_combine_kernel(cstart_ref, npiece_ref, lbase_ref, ltile_ref, wk_ref, ys_hbm, xp_ref, sg_ref, su_ref,
                        sd_ref, x_ref, mod_ref, g_ref, o_ref, ysb, ybuf, npending, sems, *, block0):
    step = pl.program_id(0)
    slot = step % 2

    def piece_copy(src_row, dst_row, buf):
        return pltpu.make_async_copy(ys_hbm.at[pl.ds(src_row // 2, MOE_PIECE // 2)],
                                     ysb.at[buf, pl.ds(dst_row // 2, MOE_PIECE // 2)], sems.at[buf])

    def fetch(blk, buf):
        def per_expert(e, total):
            j = blk * N_EXPERTS + e
            src, dst, npc = cstart_ref[j], lbase_ref[j], npiece_ref[j]

            def per_piece(p, carry):
                piece_copy(pl.multiple_of(src + p * MOE_PIECE, MOE_CHUNK_ALIGN),
                           pl.multiple_of(dst + p * MOE_PIECE, MOE_PIECE), buf).start()
                return carry

            lax.fori_loop(0, npc, per_piece, 0)
            return total + npc

        npending[buf] = lax.fori_loop(0, N_EXPERTS, per_expert, 0)

    @pl.when(step == 0)
    def _():
        fetch(block0, 0)

    @pl.when(step + 1 < pl.num_programs(0))
    def _():
        fetch(block0 + step + 1, 1 - slot)

    ybuf[...] = xp_ref[...].astype(F32).reshape(MOE_TSB * TOK_SUB, LANES)
    lhs = _tiles_to_rows(ybuf, MOE_TSB).astype(BF16)
    act = jax.nn.silu(_dot(lhs, sg_ref[...].astype(BF16))) * _dot(lhs, su_ref[...].astype(BF16))
    o_ref[...] = _dot(act.astype(BF16), sd_ref[...].astype(BF16))

    def wait_one(p, carry):
        piece_copy(0, 0, slot).wait()
        return carry

    lax.fori_loop(0, npending[slot], wait_one, 0)

    t0 = step * MOE_TSB
    yblk = ysb.at[slot]

    def weighted_row(p):
        w = jnp.broadcast_to(wk_ref[p], (TOK_SUB, LANES))
        pair = yblk[ltile_ref[p]].astype(F32)
        second = lax.bitcast_convert_type(w, jnp.int32) < 0
        return jnp.abs(w) * jnp.where(second, pair[TOK_SUB:], pair[:TOK_SUB])

    def per_token(t, carry):
        p = (t0 + t) * TOP_K
        acc = weighted_row(p)
        for k in range(1, TOP_K):
            acc = acc + weighted_row(p + k)
        ybuf[pl.ds(pl.multiple_of(t * TOK_SUB, TOK_SUB), TOK_SUB), :] = acc
        return carry

    lax.fori_loop(0, MOE_TSB, per_token, 0, unroll=4)
    routed = _tiles_to_rows(ybuf, MOE_TSB)
    o_ref[...] = x_ref[...] + mod_ref[5:6, :] * _rms(routed + o_ref[...], g_ref[3:4, :])


def _moe_combine(plan, ltile, wk, block0, ys, xp, s_gate, s_up, s_down, layer, x1, mod, g, rows_per_seq,
                 first_row):
    n = x1.shape[0]
    cap = TOP_K * MOE_TSB + N_EXPERTS * MOE_PIECE
    grid_spec = pltpu.PrefetchScalarGridSpec(
        num_scalar_prefetch=5,
        grid=(n // MOE_TSB,),
        in_specs=[
            pl.BlockSpec(memory_space=pl.ANY),
            pl.BlockSpec((MOE_TSB // 2, 2 * TOK_SUB, LANES), lambda s, *_: (s, 0, 0)),
            pl.BlockSpec((None, D_MODEL, D_SHARED), lambda s, *_: (layer, 0, 0)),
            pl.BlockSpec((None, D_MODEL, D_SHARED), lambda s, *_: (layer, 0, 0)),
            pl.BlockSpec((None, D_SHARED, D_MODEL), lambda s, *_: (layer, 0, 0)),
            pl.BlockSpec((MOE_TSB, D_MODEL), lambda s, *_: (s, 0)),
            _mod_spec(MOE_TSB, rows_per_seq, first_row),
            pl.BlockSpec((4, D_MODEL), lambda s, *_: (0, 0)),
        ],
        out_specs=pl.BlockSpec((MOE_TSB, D_MODEL), lambda s, *_: (s, 0)),
        scratch_shapes=[pltpu.VMEM((2, cap // 2, 2 * TOK_SUB, LANES), BF16),
                        pltpu.VMEM((MOE_TSB * TOK_SUB, LANES), F32),
                        pltpu.SMEM((2,), jnp.int32),
                        pltpu.SemaphoreType.DMA((2,))],
    )
    return pl.pallas_call(
        functools.partial(_moe_combine_kernel, block0=block0),
        out_shape=jax.ShapeDtypeStruct((n, D_MODEL), F32),
        grid_spec=grid_spec,
        compiler_params=_params(("arbitrary",)),
        name="moe_combine",
    )(plan["cstart"], plan["npiece"], plan["lbase"], ltile, wk, ys, xp, s_gate, s_up, s_down, x1, mod, g)


def _rope(x, cos, sin_up, sin_dn):
    out = []
    for h in range(N_HEADS):
        xs = x[:, h * V_DIM:(h + 1) * V_DIM]
        up = pltpu.roll(xs, V_DIM - ROPE_PAIRS, 1)
        dn = pltpu.roll(xs, ROPE_PAIRS, 1)
        out.append(xs * cos + up * sin_up + dn * sin_dn)
    return jnp.concatenate(out, axis=1)


def _qkv_kernel(x_ref, mod_ref, g_ref, w_ref, *rest, rope):
    if rope:
        cos_ref, sup_ref, sdn_ref, q_ref, k_ref, v_ref = rest
    else:
        q_ref, k_ref, v_ref = rest
    h = _mod_in(x_ref[...], g_ref[...], mod_ref[0:1, :], mod_ref[1:2, :])
    qkv = _dot(h.astype(BF16), w_ref[...])
    q, k, v = qkv[:, :QK_W], qkv[:, QK_W:2 * QK_W], qkv[:, 2 * QK_W:]
    if rope:
        q = _rope(q, cos_ref[...], sup_ref[...], sdn_ref[...])
        k = _rope(k, cos_ref[...], sup_ref[...], sdn_ref[...])
    q_ref[...] = (q * HEAD_DIM ** -0.5).astype(q_ref.dtype)
    k_ref[...] = k.astype(k_ref.dtype)
    v_ref[...] = v.astype(v_ref.dtype)


def _qkv(x, mod, g, w_qkv, rows_per_seq, first_row, rope_tables, kv_dtype):
    n = x.shape[0]
    tm = 512
    rope = rope_tables is not None
    in_specs = [
        pl.BlockSpec((tm, D_MODEL), lambda i: (i, 0)),
        _mod_spec(tm, rows_per_seq, first_row),
        pl.BlockSpec((1, D_MODEL), lambda i: (0, 0)),
        pl.BlockSpec((D_MODEL, 3 * QK_W), lambda i: (0, 0)),
    ]
    args = [x, mod, g, w_qkv]
    if rope:
        tiles_per_seq = rows_per_seq // tm
        in_specs += [pl.BlockSpec((tm, V_DIM), lambda i: (i % tiles_per_seq, 0))] * 3
        args += list(rope_tables)
    return pl.pallas_call(
        functools.partial(_qkv_kernel, rope=rope),
        out_shape=(jax.ShapeDtypeStruct((n, QK_W), BF16),
                   jax.ShapeDtypeStruct((n, QK_W), kv_dtype),
                   jax.ShapeDtypeStruct((n, N_HEADS * V_DIM), kv_dtype)),
        grid=(n // tm,),
        in_specs=in_specs,
        out_specs=(pl.BlockSpec((tm, QK_W), lambda i: (i, 0)),) * 3,
        compiler_params=_params(("arbitrary",)),
        name="attn_qkv_rope" if rope else "attn_qkv",
    )(*args)


def _softmax_parts(s):
    e = jnp.exp(s - jnp.max(s, axis=-1, keepdims=True))
    return e, 1.0 / jnp.sum(e, axis=-1, keepdims=True)


def _attn_kernel(lp_ref, sub_ref, q_ref, k_ref, v_ref, o_ref, *, lam_init):
    lp = lp_ref[...]
    lam = (jnp.exp(jnp.sum(lp[0:1, :] * lp[1:2, :], axis=1, keepdims=True))
           - jnp.exp(jnp.sum(lp[2:3, :] * lp[3:4, :], axis=1, keepdims=True)) + lam_init)
    for h in range(N_HEADS):
        c0 = h * V_DIM
        q = q_ref[:, c0:c0 + V_DIM]
        k = k_ref[:, c0:c0 + V_DIM].astype(BF16)
        e1, r1 = _softmax_parts(_dot_nt(q[:, :HEAD_DIM], k[:, :HEAD_DIM]))
        e2, r2 = _softmax_parts(_dot_nt(q[:, HEAD_DIM:], k[:, HEAD_DIM:]))
        w = e1 * r1 - lam * (e2 * r2)
        o = _dot(w.astype(BF16), v_ref[:, c0:c0 + V_DIM].astype(BF16))
        o_ref[:, c0:c0 + V_DIM] = (_rms(o, sub_ref[...]) * (1.0 - lam_init)).astype(BF16)


def _attention(lp, subln, q, k, v, lam_init, tq):
    bsz, t = q.shape[:2]
    tk = k.shape[1]
    return pl.pallas_call(
        functools.partial(_attn_kernel, lam_init=lam_init),
        out_shape=jax.ShapeDtypeStruct((bsz, t, N_HEADS * V_DIM), BF16),
        grid=(bsz, t // tq),
        in_specs=[
            pl.BlockSpec((4, HEAD_DIM), lambda b, i: (0, 0)),
            pl.BlockSpec((1, V_DIM), lambda b, i: (0, 0)),
            pl.BlockSpec((None, tq, QK_W), lambda b, i: (b, i, 0)),
            pl.BlockSpec((None, tk, QK_W), lambda b, i: (b, 0, 0)),
            pl.BlockSpec((None, tk, N_HEADS * V_DIM), lambda b, i: (b, 0, 0)),
        ],
        out_specs=pl.BlockSpec((None, tq, N_HEADS * V_DIM), lambda b, i: (b, i, 0)),
        compiler_params=_params(("arbitrary", "arbitrary")),
        name="diff_attention",
    )(lp, subln, q, k, v)


def _lru_gate_layout(gate_w, gate_b):
    nb = LRU_BLOCKS // 2
    w = gate_w.reshape(2, 2, 2, nb, LRU_BLOCK, LRU_BLOCK)
    bd = jnp.einsum("dgsbio,bc->sbidgco", w, jnp.eye(nb, dtype=w.dtype))
    bd = bd.reshape(2, LRU_HALF, 4 * LRU_HALF).astype(BF16)
    b = gate_b.reshape(2, 2, 2, LRU_HALF).transpose(2, 0, 1, 3).reshape(2, 1, 4 * LRU_HALF)
    return bd, b


def _rope_tables(n):
    rows = n // GRID_W
    row = jnp.repeat(jnp.arange(rows, dtype=F32), GRID_W)
    col = jnp.tile(jnp.arange(GRID_W, dtype=F32), rows)
    freqs = ROPE_THETA ** (-jnp.arange(ROPE_PAIRS, dtype=F32) / ROPE_PAIRS)
    ar = row[:, None] * freqs
    ac = col[:, None] * freqs
    zero = jnp.zeros_like(ar)
    cos = jnp.concatenate([jnp.cos(ar), jnp.cos(ar), jnp.cos(ac), jnp.cos(ac)], axis=-1)
    sin_up = jnp.concatenate([-jnp.sin(ar), zero, -jnp.sin(ac), zero], axis=-1)
    sin_dn = jnp.concatenate([zero, jnp.sin(ar), zero, jnp.sin(ac)], axis=-1)
    return tuple(jnp.tile(t, (1, 2)) for t in (cos, sin_up, sin_dn))


def kernel(x_prompt, x_sample, c, state_lru, cache_k, cache_v, c_ctx, mod_w, mod_b, norm_g,
           lru_w_in, lru_conv_w, lru_conv_b, lru_gate_w, lru_gate_b, lru_lambda, lru_w_out,
           attn_w_qkv, attn_lambda, attn_subln, attn_w_o,
           moe_router, moe_router_bias, moe_w_gate, moe_w_up, moe_w_down,
           shared_w_gate, shared_w_up, shared_w_down):
    bp, tp = x_prompt.shape[:2]
    bs, ts = x_sample.shape[:2]
    past = cache_k.shape[2]
    cond = jnp.concatenate([c_ctx[None], c, jnp.zeros((MOD_ROWS - 1 - bs, D_MODEL), F32)], axis=0)
    mod = _modulation(cond, mod_w, mod_b)
    streams = [dict(x=x_prompt.reshape(bp * tp, D_MODEL), rps=0, row=0, b=bp, t=tp),
               dict(x=x_sample.reshape(bs * ts, D_MODEL), rps=ts, row=1, b=bs, t=ts)]
    new_lru, new_k, new_v = [], [], []
    for i in range(DEPTH):
        j = i // N_MIXERS
        g = norm_g[i]
        router_t = moe_router[i].T
        router_b = moe_router_bias[i].reshape(N_EXPERTS, 1)
        if i % N_MIXERS == 0:
            w_in = lru_w_in[j].astype(BF16)
            w_mix = lru_w_out[j].astype(BF16)
            gate_w, gate_b = _lru_gate_layout(lru_gate_w[j], lru_gate_b[j])
        else:
            lam_init = 0.8 - 0.6 * math.exp(-0.3 * i)
            w_qkv = attn_w_qkv[j].astype(BF16)
            w_mix = attn_w_o[j].astype(BF16)
            tables = _rope_tables(ts)
        for si, s in enumerate(streams):
            if i % N_MIXERS == 0:
                gate, xr = _lru_in(s["x"], mod[i], g[0:1], w_in, s["rps"], s["row"])
                h0 = jnp.zeros((bp, 2, D_RNN), F32) if si == 0 else state_lru[:, j]
                m, h_last = _lru_core(xr, gate, h0, lru_conv_w[j], lru_conv_b[j][None], gate_w, gate_b,
                                      lru_lambda[j], s["b"], s["t"])
                if si == 0:
                    new_lru.append(h_last)
            else:
                if si == 0:
                    q, k, v = _qkv(s["x"], mod[i], g[0:1], w_qkv, s["rps"], s["row"], None, F32)
                    new_k.append(k.reshape(bp, tp, N_HEADS, 2 * HEAD_DIM))
                    new_v.append(v.reshape(bp, tp, N_HEADS, V_DIM))
                    k3 = k.reshape(bp, tp, QK_W)
                    v3 = v.reshape(bp, tp, N_HEADS * V_DIM)
                    tq = tp
                else:
                    q, k, v = _qkv(s["x"], mod[i], g[0:1], w_qkv, s["rps"], s["row"], tables, BF16)
                    k3 = jnp.concatenate([cache_k[:, j].reshape(bs, past, QK_W).astype(BF16),
                                          k.reshape(bs, ts, QK_W)], axis=1)
                    v3 = jnp.concatenate([cache_v[:, j].reshape(bs, past, N_HEADS * V_DIM).astype(BF16),
                                          v.reshape(bs, ts, N_HEADS * V_DIM)], axis=1)
                    tq = 512
                o = _attention(attn_lambda[j], attn_subln[j][None], q.reshape(s["b"], s["t"], QK_W),
                               k3, v3, lam_init, tq)
                m = o.reshape(s["b"] * s["t"], N_HEADS * V_DIM)
            s["x1"], s["xp"], s["comb"] = _mix_out(m, w_mix, s["x"], mod[i], g, router_t, router_b,
                                                   s["rps"], s["row"])
        xp = jnp.concatenate([s["xp"] for s in streams], axis=0)
        ek, rk, wk, cnt = _moe_plan(jnp.concatenate([s["comb"] for s in streams], axis=1))
        plan = _moe_layout(ek, rk, wk, cnt)
        ys = _moe_experts(plan, xp, moe_w_gate, moe_w_up, moe_w_down, i)
        tok0 = 0
        for s in streams:
            n = s["x1"].shape[0]
            pairs = slice(tok0 * TOP_K, (tok0 + n) * TOP_K)
            s["x"] = _moe_combine(plan, plan["ltile"][pairs], plan["wk"][pairs], tok0 // MOE_TSB, ys, s["xp"],
                                  shared_w_gate, shared_w_up,
                                  shared_w_down, i, s["x1"], mod[i], g, s["rps"], s["row"])
            tok0 += n
    return (streams[0]["x"].reshape(bp, tp, D_MODEL),
            streams[1]["x"].reshape(bs, ts, D_MODEL),
            jnp.stack(new_lru, axis=1),
            jnp.stack(new_k, axis=1),
            jnp.stack(new_v, axis=1))
```

```python
import functools
import math

import jax
import jax.numpy as jnp
from jax import lax
from jax.experimental import pallas as pl
from jax.experimental.pallas import tpu as pltpu

D_MODEL = 1024
DEPTH = 2
N_MIXERS = 2
GRID_W = 64
EPS = 1e-6
D_RNN = 1280
LRU_BLOCKS = 16
LRU_BLOCK = D_RNN // LRU_BLOCKS
CONV_W = 4
CONV_LEFT = 2
LRU_C = 8.0
N_HEADS = 8
HEAD_DIM = 64
V_DIM = 2 * HEAD_DIM
QK_W = N_HEADS * 2 * HEAD_DIM
ROPE_PAIRS = HEAD_DIM // 4
ROPE_THETA = 10000.0
N_EXPERTS = 64
N_GROUPS = 8
GROUP_SIZE = N_EXPERTS // N_GROUPS
TOPK_GROUPS = 4
TOP_K = 8
D_EXPERT = 256
D_SHARED = 256
ROUTED_SCALE = 2.5

MOD_ROWS = 8
LRU_HALF = D_RNN // 2
CONV_PAD = 8
MOE_TM = 512
MOE_ROW_GROUPS = 1
MOE_TSB = 512
MOE_CHUNK_ALIGN = 8
MOE_PIECE = 64
LANES = 128
LRU_STEP_SUB = 8
TOK_SUB = D_MODEL // LANES
VMEM_LIMIT = 56 * 1024 * 1024
BF16 = jnp.bfloat16
F32 = jnp.float32
NEG_INF = float("-inf")


def _params(sem):
    return pltpu.CompilerParams(dimension_semantics=sem, vmem_limit_bytes=VMEM_LIMIT)


def _rms(x, g):
    return x * lax.rsqrt(jnp.mean(x * x, axis=-1, keepdims=True) + EPS) * g


def _mod_in(x, g, shift, scale):
    return _rms(x, g) * (1.0 + scale) + shift


def _dot(a, b):
    return jnp.dot(a, b, preferred_element_type=F32)


def _dot_nt(a, b, precision=None):
    return lax.dot_general(a, b, (((1,), (1,)), ((), ())), precision=precision,
                           preferred_element_type=F32)


def _mod_kernel(cond_ref, w_ref, b_ref, o_ref):
    cnd = cond_ref[...]
    s = cnd * jax.nn.sigmoid(cnd)
    o_ref[...] = jnp.dot(s, w_ref[...], precision=lax.Precision.HIGHEST,
                         preferred_element_type=F32) + b_ref[...]


def _modulation(cond, mod_w, mod_b):
    tn = 1536
    out = pl.pallas_call(
        _mod_kernel,
        out_shape=jax.ShapeDtypeStruct((DEPTH, MOD_ROWS, 6 * D_MODEL), F32),
        grid=(DEPTH, 6 * D_MODEL // tn),
        in_specs=[
            pl.BlockSpec((MOD_ROWS, D_MODEL), lambda l, n: (0, 0)),
            pl.BlockSpec((None, D_MODEL, tn), lambda l, n: (l, 0, n)),
            pl.BlockSpec((None, 1, tn), lambda l, n: (l, 0, n)),
        ],
        out_specs=pl.BlockSpec((None, MOD_ROWS, tn), lambda l, n: (l, 0, n)),
        compiler_params=_params(("arbitrary", "arbitrary")),
        name="modulation",
    )(cond, mod_w, mod_b.reshape(DEPTH, 1, 6 * D_MODEL))
    return out.reshape(DEPTH, MOD_ROWS, 6, D_MODEL)


def _mod_spec(tm, rows_per_seq, first_row):
    def index(i, *_):
        return (first_row + (i * tm) // rows_per_seq if rows_per_seq else first_row, 0, 0)
    return pl.BlockSpec((None, 6, D_MODEL), index)


def _lru_in_kernel(x_ref, mod_ref, g_ref, w_ref, gate_ref, xr_ref):
    h = _mod_in(x_ref[...], g_ref[...], mod_ref[0:1, :], mod_ref[1:2, :])
    xb = _dot(h.astype(BF16), w_ref[...])
    gate_ref[...] = jax.nn.gelu(xb[:, :D_RNN]).astype(BF16)
    xr_ref[...] = xb[:, D_RNN:]


def _lru_in(x, mod, g, w_in, rows_per_seq, first_row):
    n = x.shape[0]
    tm = 512
    return pl.pallas_call(
        _lru_in_kernel,
        out_shape=(jax.ShapeDtypeStruct((n, D_RNN), BF16), jax.ShapeDtypeStruct((n, D_RNN), F32)),
        grid=(n // tm,),
        in_specs=[
            pl.BlockSpec((tm, D_MODEL), lambda i: (i, 0)),
            _mod_spec(tm, rows_per_seq, first_row),
            pl.BlockSpec((1, D_MODEL), lambda i: (0, 0)),
            pl.BlockSpec((D_MODEL, 2 * D_RNN), lambda i: (0, 0)),
        ],
        out_specs=(pl.BlockSpec((tm, D_RNN), lambda i: (i, 0)),
                   pl.BlockSpec((tm, D_RNN), lambda i: (i, 0))),
        compiler_params=_params(("arbitrary",)),
        name="lru_in",
    )(x, mod, g, w_in)


def _lru_core_kernel(xr_ref, gate_ref, h0_ref, cw_ref, cb_ref, gw_ref, gb_ref, lam_ref,
                     m_ref, hl_ref, xpad, a_f, u_f, a_b, u_b, hs, *, seq, chunk):
    c = LRU_HALF
    nslab = c // LANES

    @pl.when(jnp.logical_and(pl.program_id(0) == 0, pl.program_id(1) == 0))
    def _():
        for buf in (a_f, u_f, a_b, u_b, hs):
            buf[...] = jnp.zeros(buf.shape, F32)

    xpad[0:CONV_PAD, :] = jnp.zeros((CONV_PAD, c), F32)
    xpad[CONV_PAD + seq:, :] = jnp.zeros((CONV_PAD, c), F32)
    xpad[CONV_PAD:CONV_PAD + seq, :] = xr_ref[...]
    lam = lam_ref[...]
    coef = -LRU_C * jax.nn.softplus(-lam)
    for r0 in range(0, seq, chunk):
        xr = cb_ref[...]
        for j in range(CONV_W):
            off = CONV_PAD - CONV_LEFT + j + r0
            xr = xr + xpad[off:off + chunk, :] * cw_ref[j:j + 1, :]
        g = _dot(xr.astype(BF16), gw_ref[...]) + gb_ref[...]
        for d, (a_s, u_s) in enumerate(((a_f, u_f), (a_b, u_b))):
            r = 0.5 * (jnp.tanh(0.5 * g[:, 2 * d * c:(2 * d + 1) * c]) + 1.0)
            i = 0.5 * (jnp.tanh(0.5 * g[:, (2 * d + 1) * c:(2 * d + 2) * c]) + 1.0)
            log_a = coef[d:d + 1, :] * r
            a = jnp.exp(log_a)
            u = jnp.sqrt(-jnp.tanh(log_a) * (a * a + 1.0)) * (i * xr)
            for s in range(nslab):
                rows = pl.ds(r0 * LRU_STEP_SUB + s, chunk, stride=LRU_STEP_SUB)
                a_s[rows, :] = a[:, s * LANES:(s + 1) * LANES]
                u_s[rows, :] = u[:, s * LANES:(s + 1) * LANES]

    for d in range(2):
        for s in range(nslab):
            hs[d * LRU_STEP_SUB + s:d * LRU_STEP_SUB + s + 1, :] = h0_ref[d:d + 1, s * LANES:(s + 1) * LANES]

    def step(t, carry):
        hf, hb = carry
        rf = pl.ds(pl.multiple_of(t * LRU_STEP_SUB, LRU_STEP_SUB), LRU_STEP_SUB)
        rb = pl.ds(pl.multiple_of((seq - 1 - t) * LRU_STEP_SUB, LRU_STEP_SUB), LRU_STEP_SUB)
        hf = a_f[rf, :] * hf + u_f[rf, :]
        u_f[rf, :] = hf
        hb = a_b[rb, :] * hb + u_b[rb, :]
        u_b[rb, :] = hb
        return hf, hb

    hf, hb = lax.fori_loop(0, seq, step, (hs[0:LRU_STEP_SUB, :], hs[LRU_STEP_SUB:, :]), unroll=8)
    hs[0:LRU_STEP_SUB, :] = hf
    hs[LRU_STEP_SUB:, :] = hb
    for d in range(2):
        for s in range(nslab):
            hl_ref[d:d + 1, s * LANES:(s + 1) * LANES] = hs[d * LRU_STEP_SUB + s:d * LRU_STEP_SUB + s + 1, :]
    y = jnp.concatenate([u_f[pl.ds(s, seq, stride=LRU_STEP_SUB), :] + u_b[pl.ds(s, seq, stride=LRU_STEP_SUB), :]
                         for s in range(nslab)], axis=1)
    m_ref[...] = (y * gate_ref[...].astype(F32)).astype(BF16)


def _lru_core(xr, gate, h0, conv_w, conv_b, gate_w, gate_b, lam, n_seq, seq):
    c = LRU_HALF
    chunk = min(seq, 256)
    kernel = functools.partial(_lru_core_kernel, seq=seq, chunk=chunk)
    return pl.pallas_call(
        kernel,
        out_shape=(jax.ShapeDtypeStruct((n_seq * seq, D_RNN), BF16),
                   jax.ShapeDtypeStruct((n_seq, 2, D_RNN), F32)),
        grid=(n_seq, 2),
        in_specs=[
            pl.BlockSpec((seq, c), lambda b, j: (b, j)),
            pl.BlockSpec((seq, c), lambda b, j: (b, j)),
            pl.BlockSpec((None, 2, c), lambda b, j: (b, 0, j)),
            pl.BlockSpec((CONV_W, c), lambda b, j: (0, j)),
            pl.BlockSpec((1, c), lambda b, j: (0, j)),
            pl.BlockSpec((None, c, 4 * c), lambda b, j: (j, 0, 0)),
            pl.BlockSpec((None, 1, 4 * c), lambda b, j: (j, 0, 0)),
            pl.BlockSpec((2, c), lambda b, j: (0, j)),
        ],
        out_specs=(pl.BlockSpec((seq, c), lambda b, j: (b, j)),
                   pl.BlockSpec((None, 2, c), lambda b, j: (b, 0, j))),
        scratch_shapes=([pltpu.VMEM((seq + 2 * CONV_PAD, c), F32)]
                        + [pltpu.VMEM((seq * LRU_STEP_SUB, LANES), F32)] * 4
                        + [pltpu.VMEM((2 * LRU_STEP_SUB, LANES), F32)]),
        compiler_params=_params(("arbitrary", "arbitrary")),
        name="lru_core",
    )(xr, gate, h0, conv_w, conv_b, gate_w, gate_b, lam)


def _route(sel, scores):
    tm = sel.shape[1]
    io8 = lax.broadcasted_iota(jnp.int32, (GROUP_SIZE, tm), 0)
    blocks, gscore = [], []
    for g in range(N_GROUPS):
        blk = sel[g * GROUP_SIZE:(g + 1) * GROUP_SIZE, :]
        m1 = jnp.max(blk, axis=0, keepdims=True)
        first = jnp.min(jnp.where(blk == m1, io8, GROUP_SIZE), axis=0, keepdims=True)
        m2 = jnp.max(jnp.where(io8 == first, NEG_INF, blk), axis=0, keepdims=True)
        blocks.append(blk)
        gscore.append(m1 + m2)
    masked = []
    for g in range(N_GROUPS):
        rank = jnp.zeros((1, tm), jnp.int32)
        for o in range(N_GROUPS):
            if o == g:
                continue
            beats = (gscore[o] >= gscore[g]) if o < g else (gscore[o] > gscore[g])
            rank = rank + beats.astype(jnp.int32)
        masked.append(jnp.where(rank < TOPK_GROUPS, blocks[g], NEG_INF))
    v = jnp.concatenate(masked, axis=0)
    ioe = lax.broadcasted_iota(jnp.int32, (N_EXPERTS, tm), 0)
    chosen = jnp.zeros((N_EXPERTS, tm), F32)
    for _ in range(TOP_K):
        mx = jnp.max(v, axis=0, keepdims=True)
        first = jnp.min(jnp.where(v == mx, ioe, N_EXPERTS), axis=0, keepdims=True)
        pick = ioe == first
        chosen = jnp.where(pick, 1.0, chosen)
        v = jnp.where(pick, NEG_INF, v)
    wsel = chosen * scores
    comb = wsel / jnp.sum(wsel, axis=0, keepdims=True) * ROUTED_SCALE
    return jnp.where(chosen > 0.0, comb, -1.0)


def _rows_to_tiles(x, tmp):
    tm = x.shape[0]
    for c in range(TOK_SUB):
        tmp[pl.ds(c, tm, stride=TOK_SUB), :] = x[:, c * LANES:(c + 1) * LANES]
    return tmp[...].reshape(tm // 2, 2 * TOK_SUB, LANES).astype(BF16)


def _tiles_to_rows(tmp, tm):
    return jnp.concatenate([tmp[pl.ds(c, tm, stride=TOK_SUB), :] for c in range(TOK_SUB)], axis=1)


SECOND_HALF = -2 ** 31


def _token_code(t):
    return jnp.where((t & 1) == 1, (t >> 1) | SECOND_HALF, t >> 1)


def _load_token(pairs_ref, code):
    pair = pairs_ref[code & (2 ** 31 - 1)].astype(F32)
    second = jnp.broadcast_to(code, (TOK_SUB, LANES)) < 0
    return jnp.where(second, pair[TOK_SUB:], pair[:TOK_SUB])


def _mix_out_kernel(m_ref, w_ref, x_ref, mod_ref, g_ref, rt_ref, rb_ref, x1_ref, h2_ref, comb_ref, tmp):
    y = _dot(m_ref[...], w_ref[...])
    x1 = x_ref[...] + mod_ref[2:3, :] * _rms(y, g_ref[1:2, :])
    x1_ref[...] = x1
    h2 = _mod_in(x1, g_ref[2:3, :], mod_ref[3:4, :], mod_ref[4:5, :])
    h2_ref[...] = _rows_to_tiles(h2, tmp)
    logits = _dot_nt(rt_ref[...], h2, precision=lax.Precision.HIGHEST)
    scores = jax.nn.sigmoid(logits)
    comb_ref[...] = _route(scores + rb_ref[...], scores)


def _mix_out(m, w, x, mod, g, router_t, router_b, rows_per_seq, first_row):
    n, k = m.shape
    tm = 512
    return pl.pallas_call(
        _mix_out_kernel,
        out_shape=(jax.ShapeDtypeStruct((n, D_MODEL), F32),
                   jax.ShapeDtypeStruct((n // 2, 2 * TOK_SUB, LANES), BF16),
                   jax.ShapeDtypeStruct((N_EXPERTS, n), F32)),
        grid=(n // tm,),
        in_specs=[
            pl.BlockSpec((tm, k), lambda i: (i, 0)),
            pl.BlockSpec((k, D_MODEL), lambda i: (0, 0)),
            pl.BlockSpec((tm, D_MODEL), lambda i: (i, 0)),
            _mod_spec(tm, rows_per_seq, first_row),
            pl.BlockSpec((4, D_MODEL), lambda i: (0, 0)),
            pl.BlockSpec((N_EXPERTS, D_MODEL), lambda i: (0, 0)),
            pl.BlockSpec((N_EXPERTS, 1), lambda i: (0, 0)),
        ],
        out_specs=(pl.BlockSpec((tm, D_MODEL), lambda i: (i, 0)),
                   pl.BlockSpec((tm // 2, 2 * TOK_SUB, LANES), lambda i: (i, 0, 0)),
                   pl.BlockSpec((N_EXPERTS, tm), lambda i: (0, i))),
        scratch_shapes=[pltpu.VMEM((tm * TOK_SUB, LANES), F32)],
        compiler_params=_params(("arbitrary",)),
        name="mix_out",
    )(m, w, x, mod, g, router_t, router_b)


def _moe_plan_kernel(comb_ref, ek_ref, rk_ref, wk_ref, cnt_ref):
    c = comb_ref[...]
    n = c.shape[1]
    chosen = c >= 0.0
    ch = chosen.astype(BF16)
    tri = (lax.broadcasted_iota(jnp.int32, (n, n), 0) <= lax.broadcasted_iota(jnp.int32, (n, n), 1))
    incl = _dot(ch, tri.astype(BF16))
    rank = incl - ch.astype(F32)
    cnt_ref[...] = jnp.broadcast_to(incl[:, n - 1:n], cnt_ref.shape)
    low = (lax.broadcasted_iota(jnp.int32, (N_EXPERTS, N_EXPERTS), 1)
           < lax.broadcasted_iota(jnp.int32, (N_EXPERTS, N_EXPERTS), 0))
    slot = _dot(low.astype(BF16), ch)
    ioe = lax.broadcasted_iota(jnp.int32, c.shape, 0).astype(F32)
    for k in range(TOP_K):
        sel = jnp.logical_and(chosen, slot == float(k))
        ek_ref[k:k + 1, :] = jnp.sum(jnp.where(sel, ioe, 0.0), axis=0, keepdims=True).astype(jnp.int32)
        rk_ref[k:k + 1, :] = jnp.sum(jnp.where(sel, rank, 0.0), axis=0, keepdims=True).astype(jnp.int32)
        wk_ref[k:k + 1, :] = jnp.sum(jnp.where(sel, c, 0.0), axis=0, keepdims=True)


def _moe_plan(comb):
    n = comb.shape[1]
    nsb = n // MOE_TSB
    out_i = jax.ShapeDtypeStruct((TOP_K, n), jnp.int32)
    pair_spec = pl.BlockSpec((TOP_K, MOE_TSB), lambda s: (0, s))
    return pl.pallas_call(
        _moe_plan_kernel,
        out_shape=(out_i, out_i, jax.ShapeDtypeStruct((TOP_K, n), F32),
                   jax.ShapeDtypeStruct((N_EXPERTS, nsb * 128), F32)),
        grid=(nsb,),
        in_specs=[pl.BlockSpec((N_EXPERTS, MOE_TSB), lambda s: (0, s))],
        out_specs=(pair_spec, pair_spec, pair_spec, pl.BlockSpec((N_EXPERTS, 128), lambda s: (0, s))),
        compiler_params=_params(("arbitrary",)),
        name="moe_plan",
    )(comb)


def _moe_num_tiles(n_tok):
    rows = n_tok * TOP_K + N_EXPERTS * (n_tok // MOE_TSB) * (MOE_CHUNK_ALIGN - 1)
    return rows // MOE_TM + N_EXPERTS + 1


def _ceil_to(x, m):
    return (x + m - 1) // m * m


def _moe_layout(ek, rk, wk, cnt):
    n = ek.shape[1]
    nsb = n // MOE_TSB
    nt = _moe_num_tiles(n)
    n_es = cnt[:, ::128].astype(jnp.int32)
    c_al = _ceil_to(n_es, MOE_CHUNK_ALIGN)
    tiles_e = _ceil_to(jnp.sum(c_al, axis=1), MOE_TM) // MOE_TM
    tile0 = jnp.cumsum(tiles_e) - tiles_e
    cstart = tile0[:, None] * MOE_TM + jnp.cumsum(c_al, axis=1) - c_al
    npiece = _ceil_to(n_es, MOE_PIECE) // MOE_PIECE
    lbase = (jnp.cumsum(npiece, axis=0) - npiece) * MOE_PIECE
    dest, ldest = _moe_dest(ek, rk, cstart, lbase)
    dest_blocks = dest.reshape(TOP_K, nsb, MOE_TSB).transpose(1, 0, 2).reshape(-1)
    row_token = _moe_invert(dest_blocks, nt * MOE_TM)
    tile_expert = jnp.sum(jnp.arange(nt, dtype=jnp.int32)[:, None] >= tile0[None, :], axis=1) - 1
    wsigned = jnp.where((ldest & 1) == 1, -wk, wk)
    return dict(row_token=row_token,
                tile_expert=tile_expert.astype(jnp.int32), n_used=jnp.sum(tiles_e).reshape(1).astype(jnp.int32),
                cstart=cstart.T.reshape(-1), npiece=npiece.T.reshape(-1), lbase=lbase.T.reshape(-1),
                ltile=(ldest >> 1).T.reshape(-1), wk=wsigned.T.reshape(-1))


def _moe_dest_kernel(ek_ref, rk_ref, cs_ref, lb_ref, dest_ref, ldest_ref):
    ioe = lax.broadcasted_iota(jnp.int32, (N_EXPERTS, ek_ref.shape[1]), 0)
    cs = cs_ref[:, 0:1]
    lb = lb_ref[:, 0:1]
    for k in range(TOP_K):
        hit = ioe == ek_ref[k:k + 1, :]
        rk = rk_ref[k:k + 1, :]
        dest_ref[k:k + 1, :] = jnp.sum(jnp.where(hit, cs, 0.0), axis=0, keepdims=True).astype(jnp.int32) + rk
        ldest_ref[k:k + 1, :] = jnp.sum(jnp.where(hit, lb, 0.0), axis=0, keepdims=True).astype(jnp.int32) + rk


def _moe_dest(ek, rk, cstart, lbase):
    n = ek.shape[1]
    pair_spec = pl.BlockSpec((TOP_K, MOE_TSB), lambda s: (0, s))
    table_spec = pl.BlockSpec((N_EXPERTS, LANES), lambda s: (0, s))
    out = jax.ShapeDtypeStruct((TOP_K, n), jnp.int32)
    spread = lambda tab: jnp.repeat(tab.astype(F32), LANES, axis=1)
    return pl.pallas_call(
        _moe_dest_kernel,
        out_shape=(out, out),
        grid=(n // MOE_TSB,),
        in_specs=[pair_spec, pair_spec, table_spec, table_spec],
        out_specs=(pair_spec, pair_spec),
        compiler_params=_params(("arbitrary",)),
        name="moe_dest",
    )(ek, rk, spread(cstart), spread(lbase))


def _moe_invert_kernel(dest_hbm, zeros_hbm, code_ref, rt_ref, dsm_even, dsm_odd, sems):
    s = pl.program_id(0)
    pairs = TOP_K * MOE_TSB
    bufs = (dsm_even, dsm_odd)

    def chunk_copy(blk, parity):
        return pltpu.make_async_copy(dest_hbm.at[pl.ds(pl.multiple_of(blk * pairs, pairs), pairs)],
                                     bufs[parity], sems.at[parity])

    @pl.when(s == 0)
    def _():
        chunk_copy(0, 0).start()
        fill = pltpu.make_async_copy(zeros_hbm, rt_ref, sems.at[2])
        fill.start()
        fill.wait()

    def block(parity):
        @pl.when(s + 1 < pl.num_programs(0))
        def _():
            chunk_copy(s + 1, 1 - parity).start()

        chunk_copy(s, parity).wait()
        dsm = bufs[parity]

        def per_token(t, carry):
            code = code_ref[s * MOE_TSB + t]
            for k in range(TOP_K):
                rt_ref[dsm[k * MOE_TSB + t]] = code
            return carry

        lax.fori_loop(0, MOE_TSB, per_token, 0, unroll=8)

    @pl.when(s % 2 == 0)
    def _():
        block(0)

    @pl.when(s % 2 == 1)
    def _():
        block(1)


def _moe_invert(dest_blocks, n_rows):
    pairs = TOP_K * MOE_TSB
    return pl.pallas_call(
        _moe_invert_kernel,
        out_shape=jax.ShapeDtypeStruct((n_rows,), jnp.int32),
        grid=(dest_blocks.shape[0] // pairs,),
        in_specs=[pl.BlockSpec(memory_space=pl.ANY), pl.BlockSpec(memory_space=pl.ANY),
                  pl.BlockSpec(memory_space=pltpu.SMEM)],
        out_specs=pl.BlockSpec(memory_space=pltpu.SMEM),
        scratch_shapes=[pltpu.SMEM((pairs,), jnp.int32), pltpu.SMEM((pairs,), jnp.int32),
                        pltpu.SemaphoreType.DMA((3,))],
        compiler_params=_params(("arbitrary",)),
        name="moe_invert",
    )(dest_blocks, jnp.zeros((n_rows,), jnp.int32),
      _token_code(jnp.arange(dest_blocks.shape[0] // TOP_K, dtype=jnp.int32)))


def _moe_expert_kernel(texp_ref, nused_ref, ids_ref, xp_ref, wg_ref, wu_ref, wd_ref, ys_ref,
                       xs_even, xs_odd, tmp, wgb, wub, wdb):
    i = pl.program_id(0)

    def gather(tile, xs):
        for r in range(MOE_TM):
            xs[r * TOK_SUB:(r + 1) * TOK_SUB, :] = _load_token(xp_ref, ids_ref[tile * MOE_TM + r])

    @pl.when(i == 0)
    def _():
        gather(0, xs_even)

    def tile_body(cur, nxt):
        gather(i + 1, nxt)
        rows = MOE_TM // MOE_ROW_GROUPS
        for grp in range(MOE_ROW_GROUPS):
            tok = slice(grp * rows * TOK_SUB, (grp + 1) * rows * TOK_SUB)
            lhs = _tiles_to_rows(cur.at[tok], rows).astype(BF16)
            act = jax.nn.silu(_dot(lhs, wgb[...])) * _dot(lhs, wub[...])
            ys_ref[grp * rows // 2:(grp + 1) * rows // 2] = _rows_to_tiles(
                _dot(act.astype(BF16), wdb[...]), tmp.at[tok])

    @pl.when(i < nused_ref[0])
    def _():
        @pl.when(jnp.logical_or(i == 0, texp_ref[i] != texp_ref[jnp.maximum(i - 1, 0)]))
        def _():
            wgb[...] = wg_ref[...].astype(BF16)
            wub[...] = wu_ref[...].astype(BF16)
            wdb[...] = wd_ref[...].astype(BF16)

        @pl.when(i % 2 == 0)
        def _():
            tile_body(xs_even, xs_odd)

        @pl.when(i % 2 == 1)
        def _():
            tile_body(xs_odd, xs_even)

    @pl.when(i >= nused_ref[0])
    def _():
        ys_ref[...] = jnp.zeros(ys_ref.shape, BF16)


def _moe_experts(plan, xp, w_gate, w_up, w_down, layer):
    nt = plan["tile_expert"].shape[0]
    grid_spec = pltpu.PrefetchScalarGridSpec(
        num_scalar_prefetch=3,
        grid=(nt,),
        in_specs=[
            pl.BlockSpec(memory_space=pltpu.VMEM),
            pl.BlockSpec((None, None, D_MODEL, D_EXPERT), lambda i, te, nu, ids: (layer, te[i], 0, 0)),
            pl.BlockSpec((None, None, D_MODEL, D_EXPERT), lambda i, te, nu, ids: (layer, te[i], 0, 0)),
            pl.BlockSpec((None, None, D_EXPERT, D_MODEL), lambda i, te, nu, ids: (layer, te[i], 0, 0)),
        ],
        out_specs=pl.BlockSpec((MOE_TM // 2, 2 * TOK_SUB, LANES), lambda i, te, nu, ids: (i, 0, 0)),
        scratch_shapes=[pltpu.VMEM((MOE_TM * TOK_SUB, LANES), F32),
                        pltpu.VMEM((MOE_TM * TOK_SUB, LANES), F32),
                        pltpu.VMEM((MOE_TM * TOK_SUB, LANES), F32),
                        pltpu.VMEM((D_MODEL, D_EXPERT), BF16),
                        pltpu.VMEM((D_MODEL, D_EXPERT), BF16),
                        pltpu.VMEM((D_EXPERT, D_MODEL), BF16)],
    )
    return pl.pallas_call(
        _moe_expert_kernel,
        out_shape=jax.ShapeDtypeStruct((nt * MOE_TM // 2, 2 * TOK_SUB, LANES), BF16),
        grid_spec=grid_spec,
        compiler_params=_params(("arbitrary",)),
        name="moe_experts",
    )(plan["tile_expert"], plan["n_used"], plan["row_token"], xp, w_gate, w_up, w_down)


def _moe_combine_kernel(cstart_ref, npiece_ref, lbase_ref, ltile_ref, wk_ref, ys_hbm, xp_ref, sg_ref, su_ref,
                        sd_ref, x_ref, mod_ref, g_ref, o_ref, ysb, ybuf, npending, sems, *, block0):
    step = pl.program_id(0)
    slot = step % 2

    def piece_copy(src_row, dst_row, buf):
        return pltpu.make_async_copy(ys_hbm.at[pl.ds(src_row // 2, MOE_PIECE // 2)],
                                     ysb.at[buf, pl.ds(dst_row // 2, MOE_PIECE // 2)], sems.at[buf])

    def fetch(blk, buf):
        def per_expert(e, total):
            j = blk * N_EXPERTS + e
            src, dst, npc = cstart_ref[j], lbase_ref[j], npiece_ref[j]

            def per_piece(p, carry):
                piece_copy(pl.multiple_of(src + p * MOE_PIECE, MOE_CHUNK_ALIGN),
                           pl.multiple_of(dst + p * MOE_PIECE, MOE_PIECE), buf).start()
                return carry

            lax.fori_loop(0, npc, per_piece, 0)
            return total + npc

        npending[buf] = lax.fori_loop(0, N_EXPERTS, per_expert, 0)

    @pl.when(step == 0)
    def _():
        fetch(block0, 0)

    @pl.when(step + 1 < pl.num_programs(0))
    def _():
        fetch(block0 + step + 1, 1 - slot)

    ybuf[...] = xp_ref[...].astype(F32).reshape(MOE_TSB * TOK_SUB, LANES)
    lhs = _tiles_to_rows(ybuf, MOE_TSB).astype(BF16)
    act = jax.nn.silu(_dot(lhs, sg_ref[...].astype(BF16))) * _dot(lhs, su_ref[...].astype(BF16))
    o_ref[...] = _dot(act.astype(BF16), sd_ref[...].astype(BF16))

    def wait_one(p, carry):
        piece_copy(0, 0, slot).wait()
        return carry

    lax.fori_loop(0, npending[slot], wait_one, 0)

    t0 = step * MOE_TSB
    yblk = ysb.at[slot]

    def weighted_row(p):
        w = jnp.broadcast_to(wk_ref[p], (TOK_SUB, LANES))
        pair = yblk[ltile_ref[p]].astype(F32)
        second = lax.bitcast_convert_type(w, jnp.int32) < 0
        return jnp.abs(w) * jnp.where(second, pair[TOK_SUB:], pair[:TOK_SUB])

    def per_token(t, carry):
        p = (t0 + t) * TOP_K
        acc = weighted_row(p)
        for k in range(1, TOP_K):
            acc = acc + weighted_row(p + k)
        ybuf[pl.ds(pl.multiple_of(t * TOK_SUB, TOK_SUB), TOK_SUB), :] = acc
        return carry

    lax.fori_loop(0, MOE_TSB, per_token, 0, unroll=4)
    routed = _tiles_to_rows(ybuf, MOE_TSB)
    o_ref[...] = x_ref[...] + mod_ref[5:6, :] * _rms(routed + o_ref[...], g_ref[3:4, :])


def _moe_combine(plan, ltile, wk, block0, ys, xp, s_gate, s_up, s_down, layer, x1, mod, g, rows_per_seq,
                 first_row):
    n = x1.shape[0]
    cap = TOP_K * MOE_TSB + N_EXPERTS * MOE_PIECE
    grid_spec = pltpu.PrefetchScalarGridSpec(
        num_scalar_prefetch=5,
        grid=(n // MOE_TSB,),
        in_specs=[
            pl.BlockSpec(memory_space=pl.ANY),
            pl.BlockSpec((MOE_TSB // 2, 2 * TOK_SUB, LANES), lambda s, *_: (s, 0, 0)),
            pl.BlockSpec((None, D_MODEL, D_SHARED), lambda s, *_: (layer, 0, 0)),
            pl.BlockSpec((None, D_MODEL, D_SHARED), lambda s, *_: (layer, 0, 0)),
            pl.BlockSpec((None, D_SHARED, D_MODEL), lambda s, *_: (layer, 0, 0)),
            pl.BlockSpec((MOE_TSB, D_MODEL), lambda s, *_: (s, 0)),
            _mod_spec(MOE_TSB, rows_per_seq, first_row),
            pl.BlockSpec((4, D_MODEL), lambda s, *_: (0, 0)),
        ],
        out_specs=pl.BlockSpec((MOE_TSB, D_MODEL), lambda s, *_: (s, 0)),
        scratch_shapes=[pltpu.VMEM((2, cap // 2, 2 * TOK_SUB, LANES), BF16),
                        pltpu.VMEM((MOE_TSB * TOK_SUB, LANES), F32),
                        pltpu.SMEM((2,), jnp.int32),
                        pltpu.SemaphoreType.DMA((2,))],
    )
    return pl.pallas_call(
        functools.partial(_moe_combine_kernel, block0=block0),
        out_shape=jax.ShapeDtypeStruct((n, D_MODEL), F32),
        grid_spec=grid_spec,
        compiler_params=_params(("arbitrary",)),
        name="moe_combine",
    )(plan["cstart"], plan["npiece"], plan["lbase"], ltile, wk, ys, xp, s_gate, s_up, s_down, x1, mod, g)


def _rope(x, cos, sin_up, sin_dn):
    out = []
    for h in range(N_HEADS):
        xs = x[:, h * V_DIM:(h + 1) * V_DIM]
        up = pltpu.roll(xs, V_DIM - ROPE_PAIRS, 1)
        dn = pltpu.roll(xs, ROPE_PAIRS, 1)
        out.append(xs * cos + up * sin_up + dn * sin_dn)
    return jnp.concatenate(out, axis=1)


def _qkv_kernel(x_ref, mod_ref, g_ref, w_ref, *rest, rope):
    if rope:
        cos_ref, sup_ref, sdn_ref, q_ref, k_ref, v_ref = rest
    else:
        q_ref, k_ref, v_ref = rest
    h = _mod_in(x_ref[...], g_ref[...], mod_ref[0:1, :], mod_ref[1:2, :])
    qkv = _dot(h.astype(BF16), w_ref[...])
    q, k, v = qkv[:, :QK_W], qkv[:, QK_W:2 * QK_W], qkv[:, 2 * QK_W:]
    if rope:
        q = _rope(q, cos_ref[...], sup_ref[...], sdn_ref[...])
        k = _rope(k, cos_ref[...], sup_ref[...], sdn_ref[...])
    q_ref[...] = (q * HEAD_DIM ** -0.5).astype(q_ref.dtype)
    k_ref[...] = k.astype(k_ref.dtype)
    v_ref[...] = v.astype(v_ref.dtype)


def _qkv(x, mod, g, w_qkv, rows_per_seq, first_row, rope_tables, kv_dtype):
    n = x.shape[0]
    tm = 512
    rope = rope_tables is not None
    in_specs = [
        pl.BlockSpec((tm, D_MODEL), lambda i: (i, 0)),
        _mod_spec(tm, rows_per_seq, first_row),
        pl.BlockSpec((1, D_MODEL), lambda i: (0, 0)),
        pl.BlockSpec((D_MODEL, 3 * QK_W), lambda i: (0, 0)),
    ]
    args = [x, mod, g, w_qkv]
    if rope:
        tiles_per_seq = rows_per_seq // tm
        in_specs += [pl.BlockSpec((tm, V_DIM), lambda i: (i % tiles_per_seq, 0))] * 3
        args += list(rope_tables)
    return pl.pallas_call(
        functools.partial(_qkv_kernel, rope=rope),
        out_shape=(jax.ShapeDtypeStruct((n, QK_W), BF16),
                   jax.ShapeDtypeStruct((n, QK_W), kv_dtype),
                   jax.ShapeDtypeStruct((n, N_HEADS * V_DIM), kv_dtype)),
        grid=(n // tm,),
        in_specs=in_specs,
        out_specs=(pl.BlockSpec((tm, QK_W), lambda i: (i, 0)),) * 3,
        compiler_params=_params(("arbitrary",)),
        name="attn_qkv_rope" if rope else "attn_qkv",
    )(*args)


def _softmax_parts(s):
    e = jnp.exp(s - jnp.max(s, axis=-1, keepdims=True))
    return e, 1.0 / jnp.sum(e, axis=-1, keepdims=True)


def _attn_kernel(lp_ref, sub_ref, q_ref, k_ref, v_ref, o_ref, *, lam_init):
    lp = lp_ref[...]
    lam = (jnp.exp(jnp.sum(lp[0:1, :] * lp[1:2, :], axis=1, keepdims=True))
           - jnp.exp(jnp.sum(lp[2:3, :] * lp[3:4, :], axis=1, keepdims=True)) + lam_init)
    for h in range(N_HEADS):
        c0 = h * V_DIM
        q = q_ref[:, c0:c0 + V_DIM]
        k = k_ref[:, c0:c0 + V_DIM].astype(BF16)
        e1, r1 = _softmax_parts(_dot_nt(q[:, :HEAD_DIM], k[:, :HEAD_DIM]))
        e2, r2 = _softmax_parts(_dot_nt(q[:, HEAD_DIM:], k[:, HEAD_DIM:]))
        w = e1 * r1 - lam * (e2 * r2)
        o = _dot(w.astype(BF16), v_ref[:, c0:c0 + V_DIM].astype(BF16))
        o_ref[:, c0:c0 + V_DIM] = (_rms(o, sub_ref[...]) * (1.0 - lam_init)).astype(BF16)


def _attention(lp, subln, q, k, v, lam_init, tq):
    bsz, t = q.shape[:2]
    tk = k.shape[1]
    return pl.pallas_call(
        functools.partial(_attn_kernel, lam_init=lam_init),
        out_shape=jax.ShapeDtypeStruct((bsz, t, N_HEADS * V_DIM), BF16),
        grid=(bsz, t // tq),
        in_specs=[
            pl.BlockSpec((4, HEAD_DIM), lambda b, i: (0, 0)),
            pl.BlockSpec((1, V_DIM), lambda b, i: (0, 0)),
            pl.BlockSpec((None, tq, QK_W), lambda b, i: (b, i, 0)),
            pl.BlockSpec((None, tk, QK_W), lambda b, i: (b, 0, 0)),
            pl.BlockSpec((None, tk, N_HEADS * V_DIM), lambda b, i: (b, 0, 0)),
        ],
        out_specs=pl.BlockSpec((None, tq, N_HEADS * V_DIM), lambda b, i: (b, i, 0)),
        compiler_params=_params(("arbitrary", "arbitrary")),
        name="diff_attention",
    )(lp, subln, q, k, v)


def _lru_gate_layout(gate_w, gate_b):
    nb = LRU_BLOCKS // 2
    w = gate_w.reshape(2, 2, 2, nb, LRU_BLOCK, LRU_BLOCK)
    bd = jnp.einsum("dgsbio,bc->sbidgco", w, jnp.eye(nb, dtype=w.dtype))
    bd = bd.reshape(2, LRU_HALF, 4 * LRU_HALF).astype(BF16)
    b = gate_b.reshape(2, 2, 2, LRU_HALF).transpose(2, 0, 1, 3).reshape(2, 1, 4 * LRU_HALF)
    return bd, b


def _rope_tables(n):
    rows = n // GRID_W
    row = jnp.repeat(jnp.arange(rows, dtype=F32), GRID_W)
    col = jnp.tile(jnp.arange(GRID_W, dtype=F32), rows)
    freqs = ROPE_THETA ** (-jnp.arange(ROPE_PAIRS, dtype=F32) / ROPE_PAIRS)
    ar = row[:, None] * freqs
    ac = col[:, None] * freqs
    zero = jnp.zeros_like(ar)
    cos = jnp.concatenate([jnp.cos(ar), jnp.cos(ar), jnp.cos(ac), jnp.cos(ac)], axis=-1)
    sin_up = jnp.concatenate([-jnp.sin(ar), zero, -jnp.sin(ac), zero], axis=-1)
    sin_dn = jnp.concatenate([zero, jnp.sin(ar), zero, jnp.sin(ac)], axis=-1)
    return tuple(jnp.tile(t, (1, 2)) for t in (cos, sin_up, sin_dn))


def kernel(x_prompt, x_sample, c, state_lru, cache_k, cache_v, c_ctx, mod_w, mod_b, norm_g,
           lru_w_in, lru_conv_w, lru_conv_b, lru_gate_w, lru_gate_b, lru_lambda, lru_w_out,
           attn_w_qkv, attn_lambda, attn_subln, attn_w_o,
           moe_router, moe_router_bias, moe_w_gate, moe_w_up, moe_w_down,
           shared_w_gate, shared_w_up, shared_w_down):
    bp, tp = x_prompt.shape[:2]
    bs, ts = x_sample.shape[:2]
    past = cache_k.shape[2]
    cond = jnp.concatenate([c_ctx[None], c, jnp.zeros((MOD_ROWS - 1 - bs, D_MODEL), F32)], axis=0)
    mod = _modulation(cond, mod_w, mod_b)
    streams = [dict(x=x_prompt.reshape(bp * tp, D_MODEL), rps=0, row=0, b=bp, t=tp),
               dict(x=x_sample.reshape(bs * ts, D_MODEL), rps=ts, row=1, b=bs, t=ts)]
    new_lru, new_k, new_v = [], [], []
    for i in range(DEPTH):
        j = i // N_MIXERS
        g = norm_g[i]
        router_t = moe_router[i].T
        router_b = moe_router_bias[i].reshape(N_EXPERTS, 1)
        if i % N_MIXERS == 0:
            w_in = lru_w_in[j].astype(BF16)
            w_mix = lru_w_out[j].astype(BF16)
            gate_w, gate_b = _lru_gate_layout(lru_gate_w[j], lru_gate_b[j])
        else:
            lam_init = 0.8 - 0.6 * math.exp(-0.3 * i)
            w_qkv = attn_w_qkv[j].astype(BF16)
            w_mix = attn_w_o[j].astype(BF16)
            tables = _rope_tables(ts)
        for si, s in enumerate(streams):
            if i % N_MIXERS == 0:
                gate, xr = _lru_in(s["x"], mod[i], g[0:1], w_in, s["rps"], s["row"])
                h0 = jnp.zeros((bp, 2, D_RNN), F32) if si == 0 else state_lru[:, j]
                m, h_last = _lru_core(xr, gate, h0, lru_conv_w[j], lru_conv_b[j][None], gate_w, gate_b,
                                      lru_lambda[j], s["b"], s["t"])
                if si == 0:
                    new_lru.append(h_last)
            else:
                if si == 0:
                    q, k, v = _qkv(s["x"], mod[i], g[0:1], w_qkv, s["rps"], s["row"], None, F32)
                    new_k.append(k.reshape(bp, tp, N_HEADS, 2 * HEAD_DIM))
                    new_v.append(v.reshape(bp, tp, N_HEADS, V_DIM))
                    k3 = k.reshape(bp, tp, QK_W)
                    v3 = v.reshape(bp, tp, N_HEADS * V_DIM)
                    tq = tp
                else:
                    q, k, v = _qkv(s["x"], mod[i], g[0:1], w_qkv, s["rps"], s["row"], tables, BF16)
                    k3 = jnp.concatenate([cache_k[:, j].reshape(bs, past, QK_W).astype(BF16),
                                          k.reshape(bs, ts, QK_W)], axis=1)
                    v3 = jnp.concatenate([cache_v[:, j].reshape(bs, past, N_HEADS * V_DIM).astype(BF16),
                                          v.reshape(bs, ts, N_HEADS * V_DIM)], axis=1)
                    tq = 512
                o = _attention(attn_lambda[j], attn_subln[j][None], q.reshape(s["b"], s["t"], QK_W),
                               k3, v3, lam_init, tq)
                m = o.reshape(s["b"] * s["t"], N_HEADS * V_DIM)
            s["x1"], s["xp"], s["comb"] = _mix_out(m, w_mix, s["x"], mod[i], g, router_t, router_b,
                                                   s["rps"], s["row"])
        xp = jnp.concatenate([s["xp"] for s in streams], axis=0)
        ek, rk, wk, cnt = _moe_plan(jnp.concatenate([s["comb"] for s in streams], axis=1))
        plan = _moe_layout(ek, rk, wk, cnt)
        ys = _moe_experts(plan, xp, moe_w_gate, moe_w_up, moe_w_down, i)
        tok0 = 0
        for s in streams:
            n = s["x1"].shape[0]
            pairs = slice(tok0 * TOP_K, (tok0 + n) * TOP_K)
            s["x"] = _moe_combine(plan, plan["ltile"][pairs], plan["wk"][pairs], tok0 // MOE_TSB, ys, s["xp"],
                                  shared_w_gate, shared_w_up,
                                  shared_w_down, i, s["x1"], mod[i], g, s["rps"], s["row"])
            tok0 += n
    return (streams[0]["x"].reshape(bp, tp, D_MODEL),
            streams[1]["x"].reshape(bs, ts, D_MODEL),
            jnp.stack(new_lru, axis=1),
            jnp.stack(new_k, axis=1),
            jnp.stack(new_v, axis=1))
```

```python
import functools
import math

import jax
import jax.numpy as jnp
from jax import lax
from jax.experimental import pallas as pl
from jax.experimental.pallas import tpu as pltpu

D_MODEL = 1024
DEPTH = 2
N_MIXERS = 2
GRID_W = 64
EPS = 1e-6
D_RNN = 1280
LRU_BLOCKS = 16
LRU_BLOCK = D_RNN // LRU_BLOCKS
CONV_W = 4
CONV_LEFT = 2
LRU_C = 8.0
N_HEADS = 8
HEAD_DIM = 64
V_DIM = 2 * HEAD_DIM
QK_W = N_HEADS * 2 * HEAD_DIM
ROPE_PAIRS = HEAD_DIM // 4
ROPE_THETA = 10000.0
N_EXPERTS = 64
N_GROUPS = 8
GROUP_SIZE = N_EXPERTS // N_GROUPS
TOPK_GROUPS = 4
TOP_K = 8
D_EXPERT = 256
D_SHARED = 256
ROUTED_SCALE = 2.5

MOD_ROWS = 8
LRU_HALF = D_RNN // 2
CONV_PAD = 8
MOE_TM = 512
MOE_ROW_GROUPS = 1
MOE_TSB = 512
MOE_CHUNK_ALIGN = 8
MOE_PIECE = 64
LANES = 128
LRU_STEP_SUB = 8
TOK_SUB = D_MODEL // LANES
VMEM_LIMIT = 56 * 1024 * 1024
BF16 = jnp.bfloat16
F32 = jnp.float32
NEG_INF = float("-inf")


def _params(sem):
    return pltpu.CompilerParams(dimension_semantics=sem, vmem_limit_bytes=VMEM_LIMIT)


def _rms(x, g):
    return x * lax.rsqrt(jnp.mean(x * x, axis=-1, keepdims=True) + EPS) * g


def _mod_in(x, g, shift, scale):
    return _rms(x, g) * (1.0 + scale) + shift


def _dot(a, b):
    return jnp.dot(a, b, preferred_element_type=F32)


def _dot_nt(a, b, precision=None):
    return lax.dot_general(a, b, (((1,), (1,)), ((), ())), precision=precision,
                           preferred_element_type=F32)


def _mod_kernel(cond_ref, w_ref, b_ref, o_ref):
    cnd = cond_ref[...]
    s = cnd * jax.nn.sigmoid(cnd)
    o_ref[...] = jnp.dot(s, w_ref[...], precision=lax.Precision.HIGHEST,
                         preferred_element_type=F32) + b_ref[...]


def _modulation(cond, mod_w, mod_b):
    tn = 1536
    out = pl.pallas_call(
        _mod_kernel,
        out_shape=jax.ShapeDtypeStruct((DEPTH, MOD_ROWS, 6 * D_MODEL), F32),
        grid=(DEPTH, 6 * D_MODEL // tn),
        in_specs=[
            pl.BlockSpec((MOD_ROWS, D_MODEL), lambda l, n: (0, 0)),
            pl.BlockSpec((None, D_MODEL, tn), lambda l, n: (l, 0, n)),
            pl.BlockSpec((None, 1, tn), lambda l, n: (l, 0, n)),
        ],
        out_specs=pl.BlockSpec((None, MOD_ROWS, tn), lambda l, n: (l, 0, n)),
        compiler_params=_params(("arbitrary", "arbitrary")),
        name="modulation",
    )(cond, mod_w, mod_b.reshape(DEPTH, 1, 6 * D_MODEL))
    return out.reshape(DEPTH, MOD_ROWS, 6, D_MODEL)


def _mod_spec(tm, rows_per_seq, first_row):
    def index(i, *_):
        return (first_row + (i * tm) // rows_per_seq if rows_per_seq else first_row, 0, 0)
    return pl.BlockSpec((None, 6, D_MODEL), index)


def _lru_in_kernel(x_ref, mod_ref, g_ref, w_ref, gate_ref, xr_ref):
    h = _mod_in(x_ref[...], g_ref[...], mod_ref[0:1, :], mod_ref[1:2, :])
    xb = _dot(h.astype(BF16), w_ref[...])
    gate_ref[...] = jax.nn.gelu(xb[:, :D_RNN]).astype(BF16)
    xr_ref[...] = xb[:, D_RNN:]


def _lru_in(x, mod, g, w_in, rows_per_seq, first_row):
    n = x.shape[0]
    tm = 512
    return pl.pallas_call(
        _lru_in_kernel,
        out_shape=(jax.ShapeDtypeStruct((n, D_RNN), BF16), jax.ShapeDtypeStruct((n, D_RNN), F32)),
        grid=(n // tm,),
        in_specs=[
            pl.BlockSpec((tm, D_MODEL), lambda i: (i, 0)),
            _mod_spec(tm, rows_per_seq, first_row),
            pl.BlockSpec((1, D_MODEL), lambda i: (0, 0)),
            pl.BlockSpec((D_MODEL, 2 * D_RNN), lambda i: (0, 0)),
        ],
        out_specs=(pl.BlockSpec((tm, D_RNN), lambda i: (i, 0)),
                   pl.BlockSpec((tm, D_RNN), lambda i: (i, 0))),
        compiler_params=_params(("arbitrary",)),
        name="lru_in",
    )(x, mod, g, w_in)


def _lru_core_kernel(xr_ref, gate_ref, h0_ref, cw_ref, cb_ref, gw_ref, gb_ref, lam_ref,
                     m_ref, hl_ref, xpad, a_f, u_f, a_b, u_b, hs, *, seq, chunk):
    c = LRU_HALF
    nslab = c // LANES

    @pl.when(jnp.logical_and(pl.program_id(0) == 0, pl.program_id(1) == 0))
    def _():
        for buf in (a_f, u_f, a_b, u_b, hs):
            buf[...] = jnp.zeros(buf.shape, F32)

    xpad[0:CONV_PAD, :] = jnp.zeros((CONV_PAD, c), F32)
    xpad[CONV_PAD + seq:, :] = jnp.zeros((CONV_PAD, c), F32)
    xpad[CONV_PAD:CONV_PAD + seq, :] = xr_ref[...]
    lam = lam_ref[...]
    coef = -LRU_C * jax.nn.softplus(-lam)
    for r0 in range(0, seq, chunk):
        xr = cb_ref[...]
        for j in range(CONV_W):
            off = CONV_PAD - CONV_LEFT + j + r0
            xr = xr + xpad[off:off + chunk, :] * cw_ref[j:j + 1, :]
        g = _dot(xr.astype(BF16), gw_ref[...]) + gb_ref[...]
        for d, (a_s, u_s) in enumerate(((a_f, u_f), (a_b, u_b))):
            r = 0.5 * (jnp.tanh(0.5 * g[:, 2 * d * c:(2 * d + 1) * c]) + 1.0)
            i = 0.5 * (jnp.tanh(0.5 * g[:, (2 * d + 1) * c:(2 * d + 2) * c]) + 1.0)
            log_a = coef[d:d + 1, :] * r
            a = jnp.exp(log_a)
            u = jnp.sqrt(-jnp.tanh(log_a) * (a * a + 1.0)) * (i * xr)
            for s in range(nslab):
                rows = pl.ds(r0 * LRU_STEP_SUB + s, chunk, stride=LRU_STEP_SUB)
                a_s[rows, :] = a[:, s * LANES:(s + 1) * LANES]
                u_s[rows, :] = u[:, s * LANES:(s + 1) * LANES]

    for d in range(2):
        for s in range(nslab):
            hs[d * LRU_STEP_SUB + s:d * LRU_STEP_SUB + s + 1, :] = h0_ref[d:d + 1, s * LANES:(s + 1) * LANES]

    def step(t, carry):
        hf, hb = carry
        rf = pl.ds(pl.multiple_of(t * LRU_STEP_SUB, LRU_STEP_SUB), LRU_STEP_SUB)
        rb = pl.ds(pl.multiple_of((seq - 1 - t) * LRU_STEP_SUB, LRU_STEP_SUB), LRU_STEP_SUB)
        hf = a_f[rf, :] * hf + u_f[rf, :]
        u_f[rf, :] = hf
        hb = a_b[rb, :] * hb + u_b[rb, :]
        u_b[rb, :] = hb
        return hf, hb

    hf, hb = lax.fori_loop(0, seq, step, (hs[0:LRU_STEP_SUB, :], hs[LRU_STEP_SUB:, :]), unroll=8)
    hs[0:LRU_STEP_SUB, :] = hf
    hs[LRU_STEP_SUB:, :] = hb
    for d in range(2):
        for s in range(nslab):
            hl_ref[d:d + 1, s * LANES:(s + 1) * LANES] = hs[d * LRU_STEP_SUB + s:d * LRU_STEP_SUB + s + 1, :]
    y = jnp.concatenate([u_f[pl.ds(s, seq, stride=LRU_STEP_SUB), :] + u_b[pl.ds(s, seq, stride=LRU_STEP_SUB), :]
                         for s in range(nslab)], axis=1)
    m_ref[...] = (y * gate_ref[...].astype(F32)).astype(BF16)


def _lru_core(xr, gate, h0, conv_w, conv_b, gate_w, gate_b, lam, n_seq, seq):
    c = LRU_HALF
    chunk = min(seq, 256)
    kernel = functools.partial(_lru_core_kernel, seq=seq, chunk=chunk)
    return pl.pallas_call(
        kernel,
        out_shape=(jax.ShapeDtypeStruct((n_seq * seq, D_RNN), BF16),
                   jax.ShapeDtypeStruct((n_seq, 2, D_RNN), F32)),
        grid=(n_seq, 2),
        in_specs=[
            pl.BlockSpec((seq, c), lambda b, j: (b, j)),
            pl.BlockSpec((seq, c), lambda b, j: (b, j)),
            pl.BlockSpec((None, 2, c), lambda b, j: (b, 0, j)),
            pl.BlockSpec((CONV_W, c), lambda b, j: (0, j)),
            pl.BlockSpec((1, c), lambda b, j: (0, j)),
            pl.BlockSpec((None, c, 4 * c), lambda b, j: (j, 0, 0)),
            pl.BlockSpec((None, 1, 4 * c), lambda b, j: (j, 0, 0)),
            pl.BlockSpec((2, c), lambda b, j: (0, j)),
        ],
        out_specs=(pl.BlockSpec((seq, c), lambda b, j: (b, j)),
                   pl.BlockSpec((None, 2, c), lambda b, j: (b, 0, j))),
        scratch_shapes=([pltpu.VMEM((seq + 2 * CONV_PAD, c), F32)]
                        + [pltpu.VMEM((seq * LRU_STEP_SUB, LANES), F32)] * 4
                        + [pltpu.VMEM((2 * LRU_STEP_SUB, LANES), F32)]),
        compiler_params=_params(("arbitrary", "arbitrary")),
        name="lru_core",
    )(xr, gate, h0, conv_w, conv_b, gate_w, gate_b, lam)


def _route(sel, scores):
    tm = sel.shape[1]
    io8 = lax.broadcasted_iota(jnp.int32, (GROUP_SIZE, tm), 0)
    blocks, gscore = [], []
    for g in range(N_GROUPS):
        blk = sel[g * GROUP_SIZE:(g + 1) * GROUP_SIZE, :]
        m1 = jnp.max(blk, axis=0, keepdims=True)
        first = jnp.min(jnp.where(blk == m1, io8, GROUP_SIZE), axis=0, keepdims=True)
        m2 = jnp.max(jnp.where(io8 == first, NEG_INF, blk), axis=0, keepdims=True)
        blocks.append(blk)
        gscore.append(m1 + m2)
    masked = []
    for g in range(N_GROUPS):
        rank = jnp.zeros((1, tm), jnp.int32)
        for o in range(N_GROUPS):
            if o == g:
                continue
            beats = (gscore[o] >= gscore[g]) if o < g else (gscore[o] > gscore[g])
            rank = rank + beats.astype(jnp.int32)
        masked.append(jnp.where(rank < TOPK_GROUPS, blocks[g], NEG_INF))
    v = jnp.concatenate(masked, axis=0)
    ioe = lax.broadcasted_iota(jnp.int32, (N_EXPERTS, tm), 0)
    chosen = jnp.zeros((N_EXPERTS, tm), F32)
    for _ in range(TOP_K):
        mx = jnp.max(v, axis=0, keepdims=True)
        first = jnp.min(jnp.where(v == mx, ioe, N_EXPERTS), axis=0, keepdims=True)
        pick = ioe == first
        chosen = jnp.where(pick, 1.0, chosen)
        v = jnp.where(pick, NEG_INF, v)
    wsel = chosen * scores
    comb = wsel / jnp.sum(wsel, axis=0, keepdims=True) * ROUTED_SCALE
    return jnp.where(chosen > 0.0, comb, -1.0)


def _rows_to_tiles(x, tmp):
    tm = x.shape[0]
    for c in range(TOK_SUB):
        tmp[pl.ds(c, tm, stride=TOK_SUB), :] = x[:, c * LANES:(c + 1) * LANES]
    return tmp[...].reshape(tm // 2, 2 * TOK_SUB, LANES).astype(BF16)


def _tiles_to_rows(tmp, tm):
    return jnp.concatenate([tmp[pl.ds(c, tm, stride=TOK_SUB), :] for c in range(TOK_SUB)], axis=1)


SECOND_HALF = -2 ** 31


def _token_code(t):
    return jnp.where((t & 1) == 1, (t >> 1) | SECOND_HALF, t >> 1)


def _load_token(pairs_ref, code):
    pair = pairs_ref[code & (2 ** 31 - 1)].astype(F32)
    second = jnp.broadcast_to(code, (TOK_SUB, LANES)) < 0
    return jnp.where(second, pair[TOK_SUB:], pair[:TOK_SUB])


def _mix_out_kernel(m_ref, w_ref, x_ref, mod_ref, g_ref, rt_ref, rb_ref, x1_ref, h2_ref, comb_ref, tmp):
    y = _dot(m_ref[...], w_ref[...])
    x1 = x_ref[...] + mod_ref[2:3, :] * _rms(y, g_ref[1:2, :])
    x1_ref[...] = x1
    h2 = _mod_in(x1, g_ref[2:3, :], mod_ref[3:4, :], mod_ref[4:5, :])
    h2_ref[...] = _rows_to_tiles(h2, tmp)
    logits = _dot_nt(rt_ref[...], h2, precision=lax.Precision.HIGHEST)
    scores = jax.nn.sigmoid(logits)
    comb_ref[...] = _route(scores + rb_ref[...], scores)


def _mix_out(m, w, x, mod, g, router_t, router_b, rows_per_seq, first_row):
    n, k = m.shape
    tm = 512
    return pl.pallas_call(
        _mix_out_kernel,
        out_shape=(jax.ShapeDtypeStruct((n, D_MODEL), F32),
                   jax.ShapeDtypeStruct((n // 2, 2 * TOK_SUB, LANES), BF16),
                   jax.ShapeDtypeStruct((N_EXPERTS, n), F32)),
        grid=(n // tm,),
        in_specs=[
            pl.BlockSpec((tm, k), lambda i: (i, 0)),
            pl.BlockSpec((k, D_MODEL), lambda i: (0, 0)),
            pl.BlockSpec((tm, D_MODEL), lambda i: (i, 0)),
            _mod_spec(tm, rows_per_seq, first_row),
            pl.BlockSpec((4, D_MODEL), lambda i: (0, 0)),
            pl.BlockSpec((N_EXPERTS, D_MODEL), lambda i: (0, 0)),
            pl.BlockSpec((N_EXPERTS, 1), lambda i: (0, 0)),
        ],
        out_specs=(pl.BlockSpec((tm, D_MODEL), lambda i: (i, 0)),
                   pl.BlockSpec((tm // 2, 2 * TOK_SUB, LANES), lambda i: (i, 0, 0)),
                   pl.BlockSpec((N_EXPERTS, tm), lambda i: (0, i))),
        scratch_shapes=[pltpu.VMEM((tm * TOK_SUB, LANES), F32)],
        compiler_params=_params(("arbitrary",)),
        name="mix_out",
    )(m, w, x, mod, g, router_t, router_b)


def _moe_plan_kernel(comb_ref, ek_ref, rk_ref, wk_ref, cnt_ref):
    c = comb_ref[...]
    n = c.shape[1]
    chosen = c >= 0.0
    ch = chosen.astype(BF16)
    tri = (lax.broadcasted_iota(jnp.int32, (n, n), 0) <= lax.broadcasted_iota(jnp.int32, (n, n), 1))
    incl = _dot(ch, tri.astype(BF16))
    rank = incl - ch.astype(F32)
    cnt_ref[...] = jnp.broadcast_to(incl[:, n - 1:n], cnt_ref.shape)
    low = (lax.broadcasted_iota(jnp.int32, (N_EXPERTS, N_EXPERTS), 1)
           < lax.broadcasted_iota(jnp.int32, (N_EXPERTS, N_EXPERTS), 0))
    slot = _dot(low.astype(BF16), ch)
    ioe = lax.broadcasted_iota(jnp.int32, c.shape, 0).astype(F32)
    for k in range(TOP_K):
        sel = jnp.logical_and(chosen, slot == float(k))
        ek_ref[k:k + 1, :] = jnp.sum(jnp.where(sel, ioe, 0.0), axis=0, keepdims=True).astype(jnp.int32)
        rk_ref[k:k + 1, :] = jnp.sum(jnp.where(sel, rank, 0.0), axis=0, keepdims=True).astype(jnp.int32)
        wk_ref[k:k + 1, :] = jnp.sum(jnp.where(sel, c, 0.0), axis=0, keepdims=True)


def _moe_plan(comb):
    n = comb.shape[1]
    nsb = n // MOE_TSB
    out_i = jax.ShapeDtypeStruct((TOP_K, n), jnp.int32)
    pair_spec = pl.BlockSpec((TOP_K, MOE_TSB), lambda s: (0, s))
    return pl.pallas_call(
        _moe_plan_kernel,
        out_shape=(out_i, out_i, jax.ShapeDtypeStruct((TOP_K, n), F32),
                   jax.ShapeDtypeStruct((N_EXPERTS, nsb * 128), F32)),
        grid=(nsb,),
        in_specs=[pl.BlockSpec((N_EXPERTS, MOE_TSB), lambda s: (0, s))],
        out_specs=(pair_spec, pair_spec, pair_spec, pl.BlockSpec((N_EXPERTS, 128), lambda s: (0, s))),
        compiler_params=_params(("arbitrary",)),
        name="moe_plan",
    )(comb)


def _moe_num_tiles(n_tok):
    rows = n_tok * TOP_K + N_EXPERTS * (n_tok // MOE_TSB) * (MOE_CHUNK_ALIGN - 1)
    return rows // MOE_TM + N_EXPERTS + 1


def _ceil_to(x, m):
    return (x + m - 1) // m * m


def _moe_layout(ek, rk, wk, cnt):
    n = ek.shape[1]
    nsb = n // MOE_TSB
    nt = _moe_num_tiles(n)
    n_es = cnt[:, ::128].astype(jnp.int32)
    c_al = _ceil_to(n_es, MOE_CHUNK_ALIGN)
    tiles_e = _ceil_to(jnp.sum(c_al, axis=1), MOE_TM) // MOE_TM
    tile0 = jnp.cumsum(tiles_e) - tiles_e
    cstart = tile0[:, None] * MOE_TM + jnp.cumsum(c_al, axis=1) - c_al
    npiece = _ceil_to(n_es, MOE_PIECE) // MOE_PIECE
    lbase = (jnp.cumsum(npiece, axis=0) - npiece) * MOE_PIECE
    dest, ldest = _moe_dest(ek, rk, cstart, lbase)
    dest_blocks = dest.reshape(TOP_K, nsb, MOE_TSB).transpose(1, 2, 0).reshape(-1)
    row_token = _moe_invert(dest_blocks, nt * MOE_TM)
    tile_expert = jnp.sum(jnp.arange(nt, dtype=jnp.int32)[:, None] >= tile0[None, :], axis=1) - 1
    wsigned = jnp.where((ldest & 1) == 1, -wk, wk)
    return dict(row_token=row_token,
                tile_expert=tile_expert.astype(jnp.int32), n_used=jnp.sum(tiles_e).reshape(1).astype(jnp.int32),
                cstart=cstart.T.reshape(-1), npiece=npiece.T.reshape(-1), lbase=lbase.T.reshape(-1),
                ltile=(ldest >> 1).T.reshape(-1), wk=wsigned.T.reshape(-1))


def _moe_dest_kernel(ek_ref, rk_ref, cs_ref, lb_ref, dest_ref, ldest_ref):
    ioe = lax.broadcasted_iota(jnp.int32, (N_EXPERTS, ek_ref.shape[1]), 0)
    cs = cs_ref[:, 0:1]
    lb = lb_ref[:, 0:1]
    for k in range(TOP_K):
        hit = ioe == ek_ref[k:k + 1, :]
        rk = rk_ref[k:k + 1, :]
        dest_ref[k:k + 1, :] = jnp.sum(jnp.where(hit, cs, 0.0), axis=0, keepdims=True).astype(jnp.int32) + rk
        ldest_ref[k:k + 1, :] = jnp.sum(jnp.where(hit, lb, 0.0), axis=0, keepdims=True).astype(jnp.int32) + rk


def _moe_dest(ek, rk, cstart, lbase):
    n = ek.shape[1]
    pair_spec = pl.BlockSpec((TOP_K, MOE_TSB), lambda s: (0, s))
    table_spec = pl.BlockSpec((N_EXPERTS, LANES), lambda s: (0, s))
    out = jax.ShapeDtypeStruct((TOP_K, n), jnp.int32)
    spread = lambda tab: jnp.repeat(tab.astype(F32), LANES, axis=1)
    return pl.pallas_call(
        _moe_dest_kernel,
        out_shape=(out, out),
        grid=(n // MOE_TSB,),
        in_specs=[pair_spec, pair_spec, table_spec, table_spec],
        out_specs=(pair_spec, pair_spec),
        compiler_params=_params(("arbitrary",)),
        name="moe_dest",
    )(ek, rk, spread(cstart), spread(lbase))


def _moe_invert_kernel(dest_hbm, zeros_hbm, code_ref, rt_ref, dsm_even, dsm_odd, sems):
    s = pl.program_id(0)
    pairs = TOP_K * MOE_TSB
    bufs = (dsm_even, dsm_odd)

    def chunk_copy(blk, parity):
        return pltpu.make_async_copy(dest_hbm.at[pl.ds(pl.multiple_of(blk * pairs, pairs), pairs)],
                                     bufs[parity], sems.at[parity])

    @pl.when(s == 0)
    def _():
        chunk_copy(0, 0).start()
        fill = pltpu.make_async_copy(zeros_hbm, rt_ref, sems.at[2])
        fill.start()
        fill.wait()

    def block(parity):
        @pl.when(s + 1 < pl.num_programs(0))
        def _():
            chunk_copy(s + 1, 1 - parity).start()

        chunk_copy(s, parity).wait()
        dsm = bufs[parity]

        def per_token(t, carry):
            code = code_ref[s * MOE_TSB + t]
            for k in range(TOP_K):
                rt_ref[dsm[t * TOP_K + k]] = code
            return carry

        lax.fori_loop(0, MOE_TSB, per_token, 0, unroll=8)

    @pl.when(s % 2 == 0)
    def _():
        block(0)

    @pl.when(s % 2 == 1)
    def _():
        block(1)


def _moe_invert(dest_blocks, n_rows):
    pairs = TOP_K * MOE_TSB
    return pl.pallas_call(
        _moe_invert_kernel,
        out_shape=jax.ShapeDtypeStruct((n_rows,), jnp.int32),
        grid=(dest_blocks.shape[0] // pairs,),
        in_specs=[pl.BlockSpec(memory_space=pl.ANY), pl.BlockSpec(memory_space=pl.ANY),
                  pl.BlockSpec(memory_space=pltpu.SMEM)],
        out_specs=pl.BlockSpec(memory_space=pltpu.SMEM),
        scratch_shapes=[pltpu.SMEM((pairs,), jnp.int32), pltpu.SMEM((pairs,), jnp.int32),
                        pltpu.SemaphoreType.DMA((3,))],
        compiler_params=_params(("arbitrary",)),
        name="moe_invert",
    )(dest_blocks, jnp.zeros((n_rows,), jnp.int32),
      _token_code(jnp.arange(dest_blocks.shape[0] // TOP_K, dtype=jnp.int32)))


def _moe_expert_kernel(texp_ref, nused_ref, ids_ref, xp_ref, wg_ref, wu_ref, wd_ref, ys_ref,
                       xs_even, xs_odd, tmp, wgb, wub, wdb):
    i = pl.program_id(0)

    def gather(tile, xs):
        for r in range(MOE_TM):
            xs[r * TOK_SUB:(r + 1) * TOK_SUB, :] = _load_token(xp_ref, ids_ref[tile * MOE_TM + r])

    @pl.when(i == 0)
    def _():
        gather(0, xs_even)

    def tile_body(cur, nxt):
        gather(i + 1, nxt)
        rows = MOE_TM // MOE_ROW_GROUPS
        for grp in range(MOE_ROW_GROUPS):
            tok = slice(grp * rows * TOK_SUB, (grp + 1) * rows * TOK_SUB)
            lhs = _tiles_to_rows(cur.at[tok], rows).astype(BF16)
            act = jax.nn.silu(_dot(lhs, wgb[...])) * _dot(lhs, wub[...])
            ys_ref[grp * rows // 2:(grp + 1) * rows // 2] = _rows_to_tiles(
                _dot(act.astype(BF16), wdb[...]), tmp.at[tok])

    @pl.when(i < nused_ref[0])
    def _():
        @pl.when(jnp.logical_or(i == 0, texp_ref[i] != texp_ref[jnp.maximum(i - 1, 0)]))
        def _():
            wgb[...] = wg_ref[...].astype(BF16)
            wub[...] = wu_ref[...].astype(BF16)
            wdb[...] = wd_ref[...].astype(BF16)

        @pl.when(i % 2 == 0)
        def _():
            tile_body(xs_even, xs_odd)

        @pl.when(i % 2 == 1)
        def _():
            tile_body(xs_odd, xs_even)

    @pl.when(i >= nused_ref[0])
    def _():
        ys_ref[...] = jnp.zeros(ys_ref.shape, BF16)


def _moe_experts(plan, xp, w_gate, w_up, w_down, layer):
    nt = plan["tile_expert"].shape[0]
    grid_spec = pltpu.PrefetchScalarGridSpec(
        num_scalar_prefetch=3,
        grid=(nt,),
        in_specs=[
            pl.BlockSpec(memory_space=pltpu.VMEM),
            pl.BlockSpec((None, None, D_MODEL, D_EXPERT), lambda i, te, nu, ids: (layer, te[i], 0, 0)),
            pl.BlockSpec((None, None, D_MODEL, D_EXPERT), lambda i, te, nu, ids: (layer, te[i], 0, 0)),
            pl.BlockSpec((None, None, D_EXPERT, D_MODEL), lambda i, te, nu, ids: (layer, te[i], 0, 0)),
        ],
        out_specs=pl.BlockSpec((MOE_TM // 2, 2 * TOK_SUB, LANES), lambda i, te, nu, ids: (i, 0, 0)),
        scratch_shapes=[pltpu.VMEM((MOE_TM * TOK_SUB, LANES), F32),
                        pltpu.VMEM((MOE_TM * TOK_SUB, LANES), F32),
                        pltpu.VMEM((MOE_TM * TOK_SUB, LANES), F32),
                        pltpu.VMEM((D_MODEL, D_EXPERT), BF16),
                        pltpu.VMEM((D_MODEL, D_EXPERT), BF16),
                        pltpu.VMEM((D_EXPERT, D_MODEL), BF16)],
    )
    return pl.pallas_call(
        _moe_expert_kernel,
        out_shape=jax.ShapeDtypeStruct((nt * MOE_TM // 2, 2 * TOK_SUB, LANES), BF16),
        grid_spec=grid_spec,
        compiler_params=_params(("arbitrary",)),
        name="moe_experts",
    )(plan["tile_expert"], plan["n_used"], plan["row_token"], xp, w_gate, w_up, w_down)


def _moe_combine_kernel(cstart_ref, npiece_ref, lbase_ref, ltile_ref, wk_ref, ys_hbm, xp_ref, sg_ref, su_ref,
                        sd_ref, x_ref, mod_ref, g_ref, o_ref, ysb, ybuf, npending, sems, *, block0):
    step = pl.program_id(0)
    slot = step % 2

    def piece_copy(src_row, dst_row, buf):
        return pltpu.make_async_copy(ys_hbm.at[pl.ds(src_row // 2, MOE_PIECE // 2)],
                                     ysb.at[buf, pl.ds(dst_row // 2, MOE_PIECE // 2)], sems.at[buf])

    def fetch(blk, buf):
        def per_expert(e, total):
            j = blk * N_EXPERTS + e
            src, dst, npc = cstart_ref[j], lbase_ref[j], npiece_ref[j]

            def per_piece(p, carry):
                piece_copy(pl.multiple_of(src + p * MOE_PIECE, MOE_CHUNK_ALIGN),
                           pl.multiple_of(dst + p * MOE_PIECE, MOE_PIECE), buf).start()
                return carry

            lax.fori_loop(0, npc, per_piece, 0)
            return total + npc

        npending[buf] = lax.fori_loop(0, N_EXPERTS, per_expert, 0)

    @pl.when(step == 0)
    def _():
        fetch(block0, 0)

    @pl.when(step + 1 < pl.num_programs(0))
    def _():
        fetch(block0 + step + 1, 1 - slot)

    ybuf[...] = xp_ref[...].astype(F32).reshape(MOE_TSB * TOK_SUB, LANES)
    lhs = _tiles_to_rows(ybuf, MOE_TSB).astype(BF16)
    act = jax.nn.silu(_dot(lhs, sg_ref[...].astype(BF16))) * _dot(lhs, su_ref[...].astype(BF16))
    o_ref[...] = _dot(act.astype(BF16), sd_ref[...].astype(BF16))

    def wait_one(p, carry):
        piece_copy(0, 0, slot).wait()
        return carry

    lax.fori_loop(0, npending[slot], wait_one, 0)

    t0 = step * MOE_TSB
    yblk = ysb.at[slot]

    def weighted_row(p):
        w = jnp.broadcast_to(wk_ref[p], (TOK_SUB, LANES))
        pair = yblk[ltile_ref[p]].astype(F32)
        second = lax.bitcast_convert_type(w, jnp.int32) < 0
        return jnp.abs(w) * jnp.where(second, pair[TOK_SUB:], pair[:TOK_SUB])

    def per_token(t, carry):
        p = (t0 + t) * TOP_K
        acc = weighted_row(p)
        for k in range(1, TOP_K):
            acc = acc + weighted_row(p + k)
        ybuf[pl.ds(pl.multiple_of(t * TOK_SUB, TOK_SUB), TOK_SUB), :] = acc
        return carry

    lax.fori_loop(0, MOE_TSB, per_token, 0, unroll=8)
    routed = _tiles_to_rows(ybuf, MOE_TSB)
    o_ref[...] = x_ref[...] + mod_ref[5:6, :] * _rms(routed + o_ref[...], g_ref[3:4, :])


def _moe_combine(plan, ltile, wk, block0, ys, xp, s_gate, s_up, s_down, layer, x1, mod, g, rows_per_seq,
                 first_row):
    n = x1.shape[0]
    cap = TOP_K * MOE_TSB + N_EXPERTS * MOE_PIECE
    grid_spec = pltpu.PrefetchScalarGridSpec(
        num_scalar_prefetch=5,
        grid=(n // MOE_TSB,),
        in_specs=[
            pl.BlockSpec(memory_space=pl.ANY),
            pl.BlockSpec((MOE_TSB // 2, 2 * TOK_SUB, LANES), lambda s, *_: (s, 0, 0)),
            pl.BlockSpec((None, D_MODEL, D_SHARED), lambda s, *_: (layer, 0, 0)),
            pl.BlockSpec((None, D_MODEL, D_SHARED), lambda s, *_: (layer, 0, 0)),
            pl.BlockSpec((None, D_SHARED, D_MODEL), lambda s, *_: (layer, 0, 0)),
            pl.BlockSpec((MOE_TSB, D_MODEL), lambda s, *_: (s, 0)),
            _mod_spec(MOE_TSB, rows_per_seq, first_row),
            pl.BlockSpec((4, D_MODEL), lambda s, *_: (0, 0)),
        ],
        out_specs=pl.BlockSpec((MOE_TSB, D_MODEL), lambda s, *_: (s, 0)),
        scratch_shapes=[pltpu.VMEM((2, cap // 2, 2 * TOK_SUB, LANES), BF16),
                        pltpu.VMEM((MOE_TSB * TOK_SUB, LANES), F32),
                        pltpu.SMEM((2,), jnp.int32),
                        pltpu.SemaphoreType.DMA((2,))],
    )
    return pl.pallas_call(
        functools.partial(_moe_combine_kernel, block0=block0),
        out_shape=jax.ShapeDtypeStruct((n, D_MODEL), F32),
        grid_spec=grid_spec,
        compiler_params=_params(("arbitrary",)),
        name="moe_combine",
    )(plan["cstart"], plan["npiece"], plan["lbase"], ltile, wk, ys, xp, s_gate, s_up, s_down, x1, mod, g)


def _rope(x, cos, sin_up, sin_dn):
    out = []
    for h in range(N_HEADS):
        xs = x[:, h * V_DIM:(h + 1) * V_DIM]
        up = pltpu.roll(xs, V_DIM - ROPE_PAIRS, 1)
        dn = pltpu.roll(xs, ROPE_PAIRS, 1)
        out.append(xs * cos + up * sin_up + dn * sin_dn)
    return jnp.concatenate(out, axis=1)


def _qkv_kernel(x_ref, mod_ref, g_ref, w_ref, *rest, rope):
    if rope:
        cos_ref, sup_ref, sdn_ref, q_ref, k_ref, v_ref = rest
    else:
        q_ref, k_ref, v_ref = rest
    h = _mod_in(x_ref[...], g_ref[...], mod_ref[0:1, :], mod_ref[1:2, :])
    qkv = _dot(h.astype(BF16), w_ref[...])
    q, k, v = qkv[:, :QK_W], qkv[:, QK_W:2 * QK_W], qkv[:, 2 * QK_W:]
    if rope:
        q = _rope(q, cos_ref[...], sup_ref[...], sdn_ref[...])
        k = _rope(k, cos_ref[...], sup_ref[...], sdn_ref[...])
    q_ref[...] = (q * HEAD_DIM ** -0.5).astype(q_ref.dtype)
    k_ref[...] = k.astype(k_ref.dtype)
    v_ref[...] = v.astype(v_ref.dtype)


def _qkv(x, mod, g, w_qkv, rows_per_seq, first_row, rope_tables, kv_dtype):
    n = x.shape[0]
    tm = 512
    rope = rope_tables is not None
    in_specs = [
        pl.BlockSpec((tm, D_MODEL), lambda i: (i, 0)),
        _mod_spec(tm, rows_per_seq, first_row),
        pl.BlockSpec((1, D_MODEL), lambda i: (0, 0)),
        pl.BlockSpec((D_MODEL, 3 * QK_W), lambda i: (0, 0)),
    ]
    args = [x, mod, g, w_qkv]
    if rope:
        tiles_per_seq = rows_per_seq // tm
        in_specs += [pl.BlockSpec((tm, V_DIM), lambda i: (i % tiles_per_seq, 0))] * 3
        args += list(rope_tables)
    return pl.pallas_call(
        functools.partial(_qkv_kernel, rope=rope),
        out_shape=(jax.ShapeDtypeStruct((n, QK_W), BF16),
                   jax.ShapeDtypeStruct((n, QK_W), kv_dtype),
                   jax.ShapeDtypeStruct((n, N_HEADS * V_DIM), kv_dtype)),
        grid=(n // tm,),
        in_specs=in_specs,
        out_specs=(pl.BlockSpec((tm, QK_W), lambda i: (i, 0)),) * 3,
        compiler_params=_params(("arbitrary",)),
        name="attn_qkv_rope" if rope else "attn_qkv",
    )(*args)


def _softmax_parts(s):
    e = jnp.exp(s - jnp.max(s, axis=-1, keepdims=True))
    return e, 1.0 / jnp.sum(e, axis=-1, keepdims=True)


def _attn_kernel(lp_ref, sub_ref, q_ref, k_ref, v_ref, o_ref, *, lam_init):
    lp = lp_ref[...]
    lam = (jnp.exp(jnp.sum(lp[0:1, :] * lp[1:2, :], axis=1, keepdims=True))
           - jnp.exp(jnp.sum(lp[2:3, :] * lp[3:4, :], axis=1, keepdims=True)) + lam_init)
    for h in range(N_HEADS):
        c0 = h * V_DIM
        q = q_ref[:, c0:c0 + V_DIM]
        k = k_ref[:, c0:c0 + V_DIM].astype(BF16)
        e1, r1 = _softmax_parts(_dot_nt(q[:, :HEAD_DIM], k[:, :HEAD_DIM]))
        e2, r2 = _softmax_parts(_dot_nt(q[:, HEAD_DIM:], k[:, HEAD_DIM:]))
        w = e1 * r1 - lam * (e2 * r2)
        o = _dot(w.astype(BF16), v_ref[:, c0:c0 + V_DIM].astype(BF16))
        o_ref[:, c0:c0 + V_DIM] = (_rms(o, sub_ref[...]) * (1.0 - lam_init)).astype(BF16)


def _attention(lp, subln, q, k, v, lam_init, tq):
    bsz, t = q.shape[:2]
    tk = k.shape[1]
    return pl.pallas_call(
        functools.partial(_attn_kernel, lam_init=lam_init),
        out_shape=jax.ShapeDtypeStruct((bsz, t, N_HEADS * V_DIM), BF16),
        grid=(bsz, t // tq),
        in_specs=[
            pl.BlockSpec((4, HEAD_DIM), lambda b, i: (0, 0)),
            pl.BlockSpec((1, V_DIM), lambda b, i: (0, 0)),
            pl.BlockSpec((None, tq, QK_W), lambda b, i: (b, i, 0)),
            pl.BlockSpec((None, tk, QK_W), lambda b, i: (b, 0, 0)),
            pl.BlockSpec((None, tk, N_HEADS * V_DIM), lambda b, i: (b, 0, 0)),
        ],
        out_specs=pl.BlockSpec((None, tq, N_HEADS * V_DIM), lambda b, i: (b, i, 0)),
        compiler_params=_params(("arbitrary", "arbitrary")),
        name="diff_attention",
    )(lp, subln, q, k, v)


def _lru_gate_layout(gate_w, gate_b):
    nb = LRU_BLOCKS // 2
    w = gate_w.reshape(2, 2, 2, nb, LRU_BLOCK, LRU_BLOCK)
    bd = jnp.einsum("dgsbio,bc->sbidgco", w, jnp.eye(nb, dtype=w.dtype))
    bd = bd.reshape(2, LRU_HALF, 4 * LRU_HALF).astype(BF16)
    b = gate_b.reshape(2, 2, 2, LRU_HALF).transpose(2, 0, 1, 3).reshape(2, 1, 4 * LRU_HALF)
    return bd, b


def _rope_tables(n):
    rows = n // GRID_W
    row = jnp.repeat(jnp.arange(rows, dtype=F32), GRID_W)
    col = jnp.tile(jnp.arange(GRID_W, dtype=F32), rows)
    freqs = ROPE_THETA ** (-jnp.arange(ROPE_PAIRS, dtype=F32) / ROPE_PAIRS)
    ar = row[:, None] * freqs
    ac = col[:, None] * freqs
    zero = jnp.zeros_like(ar)
    cos = jnp.concatenate([jnp.cos(ar), jnp.cos(ar), jnp.cos(ac), jnp.cos(ac)], axis=-1)
    sin_up = jnp.concatenate([-jnp.sin(ar), zero, -jnp.sin(ac), zero], axis=-1)
    sin_dn = jnp.concatenate([zero, jnp.sin(ar), zero, jnp.sin(ac)], axis=-1)
    return tuple(jnp.tile(t, (1, 2)) for t in (cos, sin_up, sin_dn))


def kernel(x_prompt, x_sample, c, state_lru, cache_k, cache_v, c_ctx, mod_w, mod_b, norm_g,
           lru_w_in, lru_conv_w, lru_conv_b, lru_gate_w, lru_gate_b, lru_lambda, lru_w_out,
           attn_w_qkv, attn_lambda, attn_subln, attn_w_o,
           moe_router, moe_router_bias, moe_w_gate, moe_w_up, moe_w_down,
           shared_w_gate, shared_w_up, shared_w_down):
    bp, tp = x_prompt.shape[:2]
    bs, ts = x_sample.shape[:2]
    past = cache_k.shape[2]
    cond = jnp.concatenate([c_ctx[None], c, jnp.zeros((MOD_ROWS - 1 - bs, D_MODEL), F32)], axis=0)
    mod = _modulation(cond, mod_w, mod_b)
    streams = [dict(x=x_prompt.reshape(bp * tp, D_MODEL), rps=0, row=0, b=bp, t=tp),
               dict(x=x_sample.reshape(bs * ts, D_MODEL), rps=ts, row=1, b=bs, t=ts)]
    new_lru, new_k, new_v = [], [], []
    for i in range(DEPTH):
        j = i // N_MIXERS
        g = norm_g[i]
        router_t = moe_router[i].T
        router_b = moe_router_bias[i].reshape(N_EXPERTS, 1)
        if i % N_MIXERS == 0:
            w_in = lru_w_in[j].astype(BF16)
            w_mix = lru_w_out[j].astype(BF16)
            gate_w, gate_b = _lru_gate_layout(lru_gate_w[j], lru_gate_b[j])
        else:
            lam_init = 0.8 - 0.6 * math.exp(-0.3 * i)
            w_qkv = attn_w_qkv[j].astype(BF16)
            w_mix = attn_w_o[j].astype(BF16)
            tables = _rope_tables(ts)
        for si, s in enumerate(streams):
            if i % N_MIXERS == 0:
                gate, xr = _lru_in(s["x"], mod[i], g[0:1], w_in, s["rps"], s["row"])
                h0 = jnp.zeros((bp, 2, D_RNN), F32) if si == 0 else state_lru[:, j]
                m, h_last = _lru_core(xr, gate, h0, lru_conv_w[j], lru_conv_b[j][None], gate_w, gate_b,
                                      lru_lambda[j], s["b"], s["t"])
                if si == 0:
                    new_lru.append(h_last)
            else:
                if si == 0:
                    q, k, v = _qkv(s["x"], mod[i], g[0:1], w_qkv, s["rps"], s["row"], None, F32)
                    new_k.append(k.reshape(bp, tp, N_HEADS, 2 * HEAD_DIM))
                    new_v.append(v.reshape(bp, tp, N_HEADS, V_DIM))
                    k3 = k.reshape(bp, tp, QK_W)
                    v3 = v.reshape(bp, tp, N_HEADS * V_DIM)
                    tq = tp
                else:
                    q, k, v = _qkv(s["x"], mod[i], g[0:1], w_qkv, s["rps"], s["row"], tables, BF16)
                    k3 = jnp.concatenate([cache_k[:, j].reshape(bs, past, QK_W).astype(BF16),
                                          k.reshape(bs, ts, QK_W)], axis=1)
                    v3 = jnp.concatenate([cache_v[:, j].reshape(bs, past, N_HEADS * V_DIM).astype(BF16),
                                          v.reshape(bs, ts, N_HEADS * V_DIM)], axis=1)
                    tq = 512
                o = _attention(attn_lambda[j], attn_subln[j][None], q.reshape(s["b"], s["t"], QK_W),
                               k3, v3, lam_init, tq)
                m = o.reshape(s["b"] * s["t"], N_HEADS * V_DIM)
            s["x1"], s["xp"], s["comb"] = _mix_out(m, w_mix, s["x"], mod[i], g, router_t, router_b,
                                                   s["rps"], s["row"])
        xp = jnp.concatenate([s["xp"] for s in streams], axis=0)
        ek, rk, wk, cnt = _moe_plan(jnp.concatenate([s["comb"] for s in streams], axis=1))
        plan = _moe_layout(ek, rk, wk, cnt)
        ys = _moe_experts(plan, xp, moe_w_gate, moe_w_up, moe_w_down, i)
        tok0 = 0
        for s in streams:
            n = s["x1"].shape[0]
            pairs = slice(tok0 * TOP_K, (tok0 + n) * TOP_K)
            s["x"] = _moe_combine(plan, plan["ltile"][pairs], plan["wk"][pairs], tok0 // MOE_TSB, ys, s["xp"],
                                  shared_w_gate, shared_w_up,
                                  shared_w_down, i, s["x1"], mod[i], g, s["rps"], s["row"])
            tok0 += n
    return (streams[0]["x"].reshape(bp, tp, D_MODEL),
            streams[1]["x"].reshape(bs, ts, D_MODEL),
            jnp.stack(new_lru, axis=1),
            jnp.stack(new_k, axis=1),
            jnp.stack(new_v, axis=1))
```

```python
import functools
import math

import jax
import jax.numpy as jnp
from jax import lax
from jax.experimental import pallas as pl
from jax.experimental.pallas import tpu as pltpu

D_MODEL = 1024
DEPTH = 2
N_MIXERS = 2
GRID_W = 64
EPS = 1e-6
D_RNN = 1280
LRU_BLOCKS = 16
LRU_BLOCK = D_RNN // LRU_BLOCKS
CONV_W = 4
CONV_LEFT = 2
LRU_C = 8.0
N_HEADS = 8
HEAD_DIM = 64
V_DIM = 2 * HEAD_DIM
QK_W = N_HEADS * 2 * HEAD_DIM
ROPE_PAIRS = HEAD_DIM // 4
ROPE_THETA = 10000.0
N_EXPERTS = 64
N_GROUPS = 8
GROUP_SIZE = N_EXPERTS // N_GROUPS
TOPK_GROUPS = 4
TOP_K = 8
D_EXPERT = 256
D_SHARED = 256
ROUTED_SCALE = 2.5

MOD_ROWS = 8
LRU_HALF = D_RNN // 2
CONV_PAD = 8
MOE_TM = 512
MOE_ROW_GROUPS = 1
MOE_TSB = 512
MOE_CHUNK_ALIGN = 8
MOE_PIECE = 64
LANES = 128
LRU_STEP_SUB = 8
TOK_SUB = D_MODEL // LANES
VMEM_LIMIT = 56 * 1024 * 1024
BF16 = jnp.bfloat16
F32 = jnp.float32
NEG_INF = float("-inf")


def _params(sem):
    return pltpu.CompilerParams(dimension_semantics=sem, vmem_limit_bytes=VMEM_LIMIT)


def _rms(x, g):
    return x * lax.rsqrt(jnp.mean(x * x, axis=-1, keepdims=True) + EPS) * g


def _mod_in(x, g, shift, scale):
    return _rms(x, g) * (1.0 + scale) + shift


def _dot(a, b):
    return jnp.dot(a, b, preferred_element_type=F32)


def _dot_nt(a, b, precision=None):
    return lax.dot_general(a, b, (((1,), (1,)), ((), ())), precision=precision,
                           preferred_element_type=F32)


def _mod_kernel(cond_ref, w_ref, b_ref, o_ref):
    cnd = cond_ref[...]
    s = cnd * jax.nn.sigmoid(cnd)
    o_ref[...] = jnp.dot(s, w_ref[...], precision=lax.Precision.HIGHEST,
                         preferred_element_type=F32) + b_ref[...]


def _modulation(cond, mod_w, mod_b):
    tn = 1536
    out = pl.pallas_call(
        _mod_kernel,
        out_shape=jax.ShapeDtypeStruct((DEPTH, MOD_ROWS, 6 * D_MODEL), F32),
        grid=(DEPTH, 6 * D_MODEL // tn),
        in_specs=[
            pl.BlockSpec((MOD_ROWS, D_MODEL), lambda l, n: (0, 0)),
            pl.BlockSpec((None, D_MODEL, tn), lambda l, n: (l, 0, n)),
            pl.BlockSpec((None, 1, tn), lambda l, n: (l, 0, n)),
        ],
        out_specs=pl.BlockSpec((None, MOD_ROWS, tn), lambda l, n: (l, 0, n)),
        compiler_params=_params(("arbitrary", "arbitrary")),
        name="modulation",
    )(cond, mod_w, mod_b.reshape(DEPTH, 1, 6 * D_MODEL))
    return out.reshape(DEPTH, MOD_ROWS, 6, D_MODEL)


def _mod_spec(tm, rows_per_seq, first_row):
    def index(i, *_):
        return (first_row + (i * tm) // rows_per_seq if rows_per_seq else first_row, 0, 0)
    return pl.BlockSpec((None, 6, D_MODEL), index)


def _lru_in_kernel(x_ref, mod_ref, g_ref, w_ref, gate_ref, xr_ref):
    h = _mod_in(x_ref[...], g_ref[...], mod_ref[0:1, :], mod_ref[1:2, :])
    xb = _dot(h.astype(BF16), w_ref[...])
    gate_ref[...] = jax.nn.gelu(xb[:, :D_RNN]).astype(BF16)
    xr_ref[...] = xb[:, D_RNN:]


def _lru_in(x, mod, g, w_in, rows_per_seq, first_row):
    n = x.shape[0]
    tm = 512
    return pl.pallas_call(
        _lru_in_kernel,
        out_shape=(jax.ShapeDtypeStruct((n, D_RNN), BF16), jax.ShapeDtypeStruct((n, D_RNN), F32)),
        grid=(n // tm,),
        in_specs=[
            pl.BlockSpec((tm, D_MODEL), lambda i: (i, 0)),
            _mod_spec(tm, rows_per_seq, first_row),
            pl.BlockSpec((1, D_MODEL), lambda i: (0, 0)),
            pl.BlockSpec((D_MODEL, 2 * D_RNN), lambda i: (0, 0)),
        ],
        out_specs=(pl.BlockSpec((tm, D_RNN), lambda i: (i, 0)),
                   pl.BlockSpec((tm, D_RNN), lambda i: (i, 0))),
        compiler_params=_params(("arbitrary",)),
        name="lru_in",
    )(x, mod, g, w_in)


def _lru_core_kernel(xr_ref, gate_ref, h0_ref, cw_ref, cb_ref, gw_ref, gb_ref, lam_ref,
                     m_ref, hl_ref, xpad, a_f, u_f, a_b, u_b, hs, *, seq, chunk):
    c = LRU_HALF
    nslab = c // LANES

    @pl.when(jnp.logical_and(pl.program_id(0) == 0, pl.program_id(1) == 0))
    def _():
        for buf in (a_f, u_f, a_b, u_b, hs):
            buf[...] = jnp.zeros(buf.shape, F32)

    xpad[0:CONV_PAD, :] = jnp.zeros((CONV_PAD, c), F32)
    xpad[CONV_PAD + seq:, :] = jnp.zeros((CONV_PAD, c), F32)
    xpad[CONV_PAD:CONV_PAD + seq, :] = xr_ref[...]
    lam = lam_ref[...]
    coef = -LRU_C * jax.nn.softplus(-lam)
    for r0 in range(0, seq, chunk):
        xr = cb_ref[...]
        for j in range(CONV_W):
            off = CONV_PAD - CONV_LEFT + j + r0
            xr = xr + xpad[off:off + chunk, :] * cw_ref[j:j + 1, :]
        g = _dot(xr.astype(BF16), gw_ref[...]) + gb_ref[...]
        for d, (a_s, u_s) in enumerate(((a_f, u_f), (a_b, u_b))):
            r = 0.5 * (jnp.tanh(0.5 * g[:, 2 * d * c:(2 * d + 1) * c]) + 1.0)
            i = 0.5 * (jnp.tanh(0.5 * g[:, (2 * d + 1) * c:(2 * d + 2) * c]) + 1.0)
            log_a = coef[d:d + 1, :] * r
            a = jnp.exp(log_a)
            u = jnp.sqrt(-jnp.tanh(log_a) * (a * a + 1.0)) * (i * xr)
            for s in range(nslab):
                rows = pl.ds(r0 * LRU_STEP_SUB + s, chunk, stride=LRU_STEP_SUB)
                a_s[rows, :] = a[:, s * LANES:(s + 1) * LANES]
                u_s[rows, :] = u[:, s * LANES:(s + 1) * LANES]

    for d in range(2):
        for s in range(nslab):
            hs[d * LRU_STEP_SUB + s:d * LRU_STEP_SUB + s + 1, :] = h0_ref[d:d + 1, s * LANES:(s + 1) * LANES]

    def step(t, carry):
        hf, hb = carry
        rf = pl.ds(pl.multiple_of(t * LRU_STEP_SUB, LRU_STEP_SUB), LRU_STEP_SUB)
        rb = pl.ds(pl.multiple_of((seq - 1 - t) * LRU_STEP_SUB, LRU_STEP_SUB), LRU_STEP_SUB)
        hf = a_f[rf, :] * hf + u_f[rf, :]
        u_f[rf, :] = hf
        hb = a_b[rb, :] * hb + u_b[rb, :]
        u_b[rb, :] = hb
        return hf, hb

    hf, hb = lax.fori_loop(0, seq, step, (hs[0:LRU_STEP_SUB, :], hs[LRU_STEP_SUB:, :]), unroll=8)
    hs[0:LRU_STEP_SUB, :] = hf
    hs[LRU_STEP_SUB:, :] = hb
    for d in range(2):
        for s in range(nslab):
            hl_ref[d:d + 1, s * LANES:(s + 1) * LANES] = hs[d * LRU_STEP_SUB + s:d * LRU_STEP_SUB + s + 1, :]
    y = jnp.concatenate([u_f[pl.ds(s, seq, stride=LRU_STEP_SUB), :] + u_b[pl.ds(s, seq, stride=LRU_STEP_SUB), :]
                         for s in range(nslab)], axis=1)
    m_ref[...] = (y * gate_ref[...].astype(F32)).astype(BF16)


def _lru_core(xr, gate, h0, conv_w, conv_b, gate_w, gate_b, lam, n_seq, seq):
    c = LRU_HALF
    chunk = min(seq, 256)
    kernel = functools.partial(_lru_core_kernel, seq=seq, chunk=chunk)
    return pl.pallas_call(
        kernel,
        out_shape=(jax.ShapeDtypeStruct((n_seq * seq, D_RNN), BF16),
                   jax.ShapeDtypeStruct((n_seq, 2, D_RNN), F32)),
        grid=(n_seq, 2),
        in_specs=[
            pl.BlockSpec((seq, c), lambda b, j: (b, j)),
            pl.BlockSpec((seq, c), lambda b, j: (b, j)),
            pl.BlockSpec((None, 2, c), lambda b, j: (b, 0, j)),
            pl.BlockSpec((CONV_W, c), lambda b, j: (0, j)),
            pl.BlockSpec((1, c), lambda b, j: (0, j)),
            pl.BlockSpec((None, c, 4 * c), lambda b, j: (j, 0, 0)),
            pl.BlockSpec((None, 1, 4 * c), lambda b, j: (j, 0, 0)),
            pl.BlockSpec((2, c), lambda b, j: (0, j)),
        ],
        out_specs=(pl.BlockSpec((seq, c), lambda b, j: (b, j)),
                   pl.BlockSpec((None, 2, c), lambda b, j: (b, 0, j))),
        scratch_shapes=([pltpu.VMEM((seq + 2 * CONV_PAD, c), F32)]
                        + [pltpu.VMEM((seq * LRU_STEP_SUB, LANES), F32)] * 4
                        + [pltpu.VMEM((2 * LRU_STEP_SUB, LANES), F32)]),
        compiler_params=_params(("arbitrary", "arbitrary")),
        name="lru_core",
    )(xr, gate, h0, conv_w, conv_b, gate_w, gate_b, lam)


def _route(sel, scores):
    tm = sel.shape[1]
    io8 = lax.broadcasted_iota(jnp.int32, (GROUP_SIZE, tm), 0)
    blocks, gscore = [], []
    for g in range(N_GROUPS):
        blk = sel[g * GROUP_SIZE:(g + 1) * GROUP_SIZE, :]
        m1 = jnp.max(blk, axis=0, keepdims=True)
        first = jnp.min(jnp.where(blk == m1, io8, GROUP_SIZE), axis=0, keepdims=True)
        m2 = jnp.max(jnp.where(io8 == first, NEG_INF, blk), axis=0, keepdims=True)
        blocks.append(blk)
        gscore.append(m1 + m2)
    masked = []
    for g in range(N_GROUPS):
        rank = jnp.zeros((1, tm), jnp.int32)
        for o in range(N_GROUPS):
            if o == g:
                continue
            beats = (gscore[o] >= gscore[g]) if o < g else (gscore[o] > gscore[g])
            rank = rank + beats.astype(jnp.int32)
        masked.append(jnp.where(rank < TOPK_GROUPS, blocks[g], NEG_INF))
    v = jnp.concatenate(masked, axis=0)
    ioe = lax.broadcasted_iota(jnp.int32, (N_EXPERTS, tm), 0)
    chosen = jnp.zeros((N_EXPERTS, tm), F32)
    for _ in range(TOP_K):
        mx = jnp.max(v, axis=0, keepdims=True)
        first = jnp.min(jnp.where(v == mx, ioe, N_EXPERTS), axis=0, keepdims=True)
        pick = ioe == first
        chosen = jnp.where(pick, 1.0, chosen)
        v = jnp.where(pick, NEG_INF, v)
    wsel = chosen * scores
    comb = wsel / jnp.sum(wsel, axis=0, keepdims=True) * ROUTED_SCALE
    return jnp.where(chosen > 0.0, comb, -1.0)


def _rows_to_tiles(x, tmp):
    tm = x.shape[0]
    for c in range(TOK_SUB):
        tmp[pl.ds(c, tm, stride=TOK_SUB), :] = x[:, c * LANES:(c + 1) * LANES]
    return tmp[...].reshape(tm // 2, 2 * TOK_SUB, LANES).astype(BF16)


def _tiles_to_rows(tmp, tm):
    return jnp.concatenate([tmp[pl.ds(c, tm, stride=TOK_SUB), :] for c in range(TOK_SUB)], axis=1)


SECOND_HALF = -2 ** 31


def _token_code(t):
    return jnp.where((t & 1) == 1, (t >> 1) | SECOND_HALF, t >> 1)


def _load_token(pairs_ref, code):
    pair = pairs_ref[code & (2 ** 31 - 1)].astype(F32)
    second = jnp.broadcast_to(code, (TOK_SUB, LANES)) < 0
    return jnp.where(second, pair[TOK_SUB:], pair[:TOK_SUB])


def _mix_out_kernel(m_ref, w_ref, x_ref, mod_ref, g_ref, rt_ref, rb_ref, x1_ref, h2_ref, comb_ref, tmp):
    y = _dot(m_ref[...], w_ref[...])
    x1 = x_ref[...] + mod_ref[2:3, :] * _rms(y, g_ref[1:2, :])
    x1_ref[...] = x1
    h2 = _mod_in(x1, g_ref[2:3, :], mod_ref[3:4, :], mod_ref[4:5, :])
    h2_ref[...] = _rows_to_tiles(h2, tmp)
    logits = _dot_nt(rt_ref[...], h2, precision=lax.Precision.HIGHEST)
    scores = jax.nn.sigmoid(logits)
    comb_ref[...] = _route(scores + rb_ref[...], scores)


def _mix_out(m, w, x, mod, g, router_t, router_b, rows_per_seq, first_row):
    n, k = m.shape
    tm = 512
    return pl.pallas_call(
        _mix_out_kernel,
        out_shape=(jax.ShapeDtypeStruct((n, D_MODEL), F32),
                   jax.ShapeDtypeStruct((n // 2, 2 * TOK_SUB, LANES), BF16),
                   jax.ShapeDtypeStruct((N_EXPERTS, n), F32)),
        grid=(n // tm,),
        in_specs=[
            pl.BlockSpec((tm, k), lambda i: (i, 0)),
            pl.BlockSpec((k, D_MODEL), lambda i: (0, 0)),
            pl.BlockSpec((tm, D_MODEL), lambda i: (i, 0)),
            _mod_spec(tm, rows_per_seq, first_row),
            pl.BlockSpec((4, D_MODEL), lambda i: (0, 0)),
            pl.BlockSpec((N_EXPERTS, D_MODEL), lambda i: (0, 0)),
            pl.BlockSpec((N_EXPERTS, 1), lambda i: (0, 0)),
        ],
        out_specs=(pl.BlockSpec((tm, D_MODEL), lambda i: (i, 0)),
                   pl.BlockSpec((tm // 2, 2 * TOK_SUB, LANES), lambda i: (i, 0, 0)),
                   pl.BlockSpec((N_EXPERTS, tm), lambda i: (0, i))),
        scratch_shapes=[pltpu.VMEM((tm * TOK_SUB, LANES), F32)],
        compiler_params=_params(("arbitrary",)),
        name="mix_out",
    )(m, w, x, mod, g, router_t, router_b)


def _moe_plan_kernel(comb_ref, ek_ref, rk_ref, wk_ref, cnt_ref):
    c = comb_ref[...]
    n = c.shape[1]
    chosen = c >= 0.0
    ch = chosen.astype(BF16)
    tri = (lax.broadcasted_iota(jnp.int32, (n, n), 0) <= lax.broadcasted_iota(jnp.int32, (n, n), 1))
    incl = _dot(ch, tri.astype(BF16))
    rank = incl - ch.astype(F32)
    cnt_ref[...] = jnp.broadcast_to(incl[:, n - 1:n], cnt_ref.shape)
    low = (lax.broadcasted_iota(jnp.int32, (N_EXPERTS, N_EXPERTS), 1)
           < lax.broadcasted_iota(jnp.int32, (N_EXPERTS, N_EXPERTS), 0))
    slot = _dot(low.astype(BF16), ch)
    ioe = lax.broadcasted_iota(jnp.int32, c.shape, 0).astype(F32)
    for k in range(TOP_K):
        sel = jnp.logical_and(chosen, slot == float(k))
        ek_ref[k:k + 1, :] = jnp.sum(jnp.where(sel, ioe, 0.0), axis=0, keepdims=True).astype(jnp.int32)
        rk_ref[k:k + 1, :] = jnp.sum(jnp.where(sel, rank, 0.0), axis=0, keepdims=True).astype(jnp.int32)
        wk_ref[k:k + 1, :] = jnp.sum(jnp.where(sel, c, 0.0), axis=0, keepdims=True)


def _moe_plan(comb):
    n = comb.shape[1]
    nsb = n // MOE_TSB
    out_i = jax.ShapeDtypeStruct((TOP_K, n), jnp.int32)
    pair_spec = pl.BlockSpec((TOP_K, MOE_TSB), lambda s: (0, s))
    return pl.pallas_call(
        _moe_plan_kernel,
        out_shape=(out_i, out_i, jax.ShapeDtypeStruct((TOP_K, n), F32),
                   jax.ShapeDtypeStruct((N_EXPERTS, nsb * 128), F32)),
        grid=(nsb,),
        in_specs=[pl.BlockSpec((N_EXPERTS, MOE_TSB), lambda s: (0, s))],
        out_specs=(pair_spec, pair_spec, pair_spec, pl.BlockSpec((N_EXPERTS, 128), lambda s: (0, s))),
        compiler_params=_params(("arbitrary",)),
        name="moe_plan",
    )(comb)


def _moe_num_tiles(n_tok):
    rows = n_tok * TOP_K + N_EXPERTS * (n_tok // MOE_TSB) * (MOE_CHUNK_ALIGN - 1)
    return rows // MOE_TM + N_EXPERTS + 1


def _ceil_to(x, m):
    return (x + m - 1) // m * m


def _moe_layout(ek, rk, wk, cnt):
    n = ek.shape[1]
    nsb = n // MOE_TSB
    nt = _moe_num_tiles(n)
    n_es = cnt[:, ::128].astype(jnp.int32)
    c_al = _ceil_to(n_es, MOE_CHUNK_ALIGN)
    tiles_e = _ceil_to(jnp.sum(c_al, axis=1), MOE_TM) // MOE_TM
    tile0 = jnp.cumsum(tiles_e) - tiles_e
    cstart = tile0[:, None] * MOE_TM + jnp.cumsum(c_al, axis=1) - c_al
    npiece = _ceil_to(n_es, MOE_PIECE) // MOE_PIECE
    lbase = (jnp.cumsum(npiece, axis=0) - npiece) * MOE_PIECE
    dest, ldest = _moe_dest(ek, rk, cstart, lbase)
    dest_blocks = dest.reshape(TOP_K, nsb, MOE_TSB).transpose(1, 0, 2).reshape(-1)
    row_token = _moe_invert(dest_blocks, nt * MOE_TM)
    tile_expert = jnp.sum(jnp.arange(nt, dtype=jnp.int32)[:, None] >= tile0[None, :], axis=1) - 1
    wsigned = jnp.where((ldest & 1) == 1, -wk, wk)
    return dict(row_token=row_token,
                tile_expert=tile_expert.astype(jnp.int32), n_used=jnp.sum(tiles_e).reshape(1).astype(jnp.int32),
                cstart=cstart.T.reshape(-1), npiece=npiece.T.reshape(-1), lbase=lbase.T.reshape(-1),
                ltile=(ldest >> 1).T.reshape(-1), wk=wsigned.T.reshape(-1))


def _moe_dest_kernel(ek_ref, rk_ref, cs_ref, lb_ref, dest_ref, ldest_ref):
    ioe = lax.broadcasted_iota(jnp.int32, (N_EXPERTS, ek_ref.shape[1]), 0)
    cs = cs_ref[:, 0:1]
    lb = lb_ref[:, 0:1]
    for k in range(TOP_K):
        hit = ioe == ek_ref[k:k + 1, :]
        rk = rk_ref[k:k + 1, :]
        dest_ref[k:k + 1, :] = jnp.sum(jnp.where(hit, cs, 0.0), axis=0, keepdims=True).astype(jnp.int32) + rk
        ldest_ref[k:k + 1, :] = jnp.sum(jnp.where(hit, lb, 0.0), axis=0, keepdims=True).astype(jnp.int32) + rk


def _moe_dest(ek, rk, cstart, lbase):
    n = ek.shape[1]
    pair_spec = pl.BlockSpec((TOP_K, MOE_TSB), lambda s: (0, s))
    table_spec = pl.BlockSpec((N_EXPERTS, LANES), lambda s: (0, s))
    out = jax.ShapeDtypeStruct((TOP_K, n), jnp.int32)
    spread = lambda tab: jnp.repeat(tab.astype(F32), LANES, axis=1)
    return pl.pallas_call(
        _moe_dest_kernel,
        out_shape=(out, out),
        grid=(n // MOE_TSB,),
        in_specs=[pair_spec, pair_spec, table_spec, table_spec],
        out_specs=(pair_spec, pair_spec),
        compiler_params=_params(("arbitrary",)),
        name="moe_dest",
    )(ek, rk, spread(cstart), spread(lbase))


def _moe_invert_kernel(dest_hbm, zeros_hbm, code_ref, rt_ref, dsm_even, dsm_odd, sems):
    s = pl.program_id(0)
    pairs = TOP_K * MOE_TSB
    bufs = (dsm_even, dsm_odd)

    def chunk_copy(blk, parity):
        return pltpu.make_async_copy(dest_hbm.at[pl.ds(pl.multiple_of(blk * pairs, pairs), pairs)],
                                     bufs[parity], sems.at[parity])

    @pl.when(s == 0)
    def _():
        chunk_copy(0, 0).start()
        fill = pltpu.make_async_copy(zeros_hbm, rt_ref, sems.at[2])
        fill.start()
        fill.wait()

    def block(parity):
        @pl.when(s + 1 < pl.num_programs(0))
        def _():
            chunk_copy(s + 1, 1 - parity).start()

        chunk_copy(s, parity).wait()
        dsm = bufs[parity]

        def per_token(t, carry):
            code = code_ref[s * MOE_TSB + t]
            for k in range(TOP_K):
                rt_ref[dsm[k * MOE_TSB + t]] = code
            return carry

        lax.fori_loop(0, MOE_TSB, per_token, 0, unroll=8)

    @pl.when(s % 2 == 0)
    def _():
        block(0)

    @pl.when(s % 2 == 1)
    def _():
        block(1)


def _moe_invert(dest_blocks, n_rows):
    pairs = TOP_K * MOE_TSB
    return pl.pallas_call(
        _moe_invert_kernel,
        out_shape=jax.ShapeDtypeStruct((n_rows,), jnp.int32),
        grid=(dest_blocks.shape[0] // pairs,),
        in_specs=[pl.BlockSpec(memory_space=pl.ANY), pl.BlockSpec(memory_space=pl.ANY),
                  pl.BlockSpec(memory_space=pltpu.SMEM)],
        out_specs=pl.BlockSpec(memory_space=pltpu.SMEM),
        scratch_shapes=[pltpu.SMEM((pairs,), jnp.int32), pltpu.SMEM((pairs,), jnp.int32),
                        pltpu.SemaphoreType.DMA((3,))],
        compiler_params=_params(("arbitrary",)),
        name="moe_invert",
    )(dest_blocks, jnp.zeros((n_rows,), jnp.int32),
      _token_code(jnp.arange(dest_blocks.shape[0] // TOP_K, dtype=jnp.int32)))


def _moe_expert_kernel(texp_ref, nused_ref, ids_ref, xp_ref, wg_ref, wu_ref, wd_ref, ys_ref,
                       xs_even, xs_odd, tmp, wgb, wub, wdb):
    i = pl.program_id(0)

    def gather(tile, xs):
        for r in range(MOE_TM):
            xs[r * TOK_SUB:(r + 1) * TOK_SUB, :] = _load_token(xp_ref, ids_ref[tile * MOE_TM + r])

    @pl.when(i == 0)
    def _():
        gather(0, xs_even)

    def tile_body(cur, nxt):
        gather(i + 1, nxt)
        rows = MOE_TM // MOE_ROW_GROUPS
        for grp in range(MOE_ROW_GROUPS):
            tok = slice(grp * rows * TOK_SUB, (grp + 1) * rows * TOK_SUB)
            lhs = _tiles_to_rows(cur.at[tok], rows).astype(BF16)
            act = jax.nn.silu(_dot(lhs, wgb[...])) * _dot(lhs, wub[...])
            ys_ref[grp * rows // 2:(grp + 1) * rows // 2] = _rows_to_tiles(
                _dot(act.astype(BF16), wdb[...]), tmp.at[tok])

    @pl.when(i < nused_ref[0])
    def _():
        @pl.when(jnp.logical_or(i == 0, texp_ref[i] != texp_ref[jnp.maximum(i - 1, 0)]))
        def _():
            wgb[...] = wg_ref[...].astype(BF16)
            wub[...] = wu_ref[...].astype(BF16)
            wdb[...] = wd_ref[...].astype(BF16)

        @pl.when(i % 2 == 0)
        def _():
            tile_body(xs_even, xs_odd)

        @pl.when(i % 2 == 1)
        def _():
            tile_body(xs_odd, xs_even)

    @pl.when(i >= nused_ref[0])
    def _():
        ys_ref[...] = jnp.zeros(ys_ref.shape, BF16)


def _moe_experts(plan, xp, w_gate, w_up, w_down, layer):
    nt = plan["tile_expert"].shape[0]
    grid_spec = pltpu.PrefetchScalarGridSpec(
        num_scalar_prefetch=3,
        grid=(nt,),
        in_specs=[
            pl.BlockSpec(memory_space=pltpu.VMEM),
            pl.BlockSpec((None, None, D_MODEL, D_EXPERT), lambda i, te, nu, ids: (layer, te[i], 0, 0)),
            pl.BlockSpec((None, None, D_MODEL, D_EXPERT), lambda i, te, nu, ids: (layer, te[i], 0, 0)),
            pl.BlockSpec((None, None, D_EXPERT, D_MODEL), lambda i, te, nu, ids: (layer, te[i], 0, 0)),
        ],
        out_specs=pl.BlockSpec((MOE_TM // 2, 2 * TOK_SUB, LANES), lambda i, te, nu, ids: (i, 0, 0)),
        scratch_shapes=[pltpu.VMEM((MOE_TM * TOK_SUB, LANES), F32),
                        pltpu.VMEM((MOE_TM * TOK_SUB, LANES), F32),
                        pltpu.VMEM((MOE_TM * TOK_SUB, LANES), F32),
                        pltpu.VMEM((D_MODEL, D_EXPERT), BF16),
                        pltpu.VMEM((D_MODEL, D_EXPERT), BF16),
                        pltpu.VMEM((D_EXPERT, D_MODEL), BF16)],
    )
    return pl.pallas_call(
        _moe_expert_kernel,
        out_shape=jax.ShapeDtypeStruct((nt * MOE_TM // 2, 2 * TOK_SUB, LANES), BF16),
        grid_spec=grid_spec,
        compiler_params=_params(("arbitrary",)),
        name="moe_experts",
    )(plan["tile_expert"], plan["n_used"], plan["row_token"], xp, w_gate, w_up, w_down)


def _moe_combine_kernel(cstart_ref, npiece_ref, lbase_ref, ltile_ref, wk_ref, ys_hbm, xp_ref, sg_ref, su_ref,
                        sd_ref, x_ref, mod_ref, g_ref, o_ref, ysb, ybuf, npending, sems, *, block0):
    step = pl.program_id(0)
    slot = step % 2

    def piece_copy(src_row, dst_row, buf):
        return pltpu.make_async_copy(ys_hbm.at[pl.ds(src_row // 2, MOE_PIECE // 2)],
                                     ysb.at[buf, pl.ds(dst_row // 2, MOE_PIECE // 2)], sems.at[buf])

    def fetch(blk, buf):
        def per_expert(e, total):
            j = blk * N_EXPERTS + e
            src, dst, npc = cstart_ref[j], lbase_ref[j], npiece_ref[j]

            def per_piece(p, carry):
                piece_copy(pl.multiple_of(src + p * MOE_PIECE, MOE_CHUNK_ALIGN),
                           pl.multiple_of(dst + p * MOE_PIECE, MOE_PIECE), buf).start()
                return carry

            lax.fori_loop(0, npc, per_piece, 0)
            return total + npc

        npending[buf] = lax.fori_loop(0, N_EXPERTS, per_expert, 0)

    @pl.when(step == 0)
    def _():
        fetch(block0, 0)

    @pl.when(step + 1 < pl.num_programs(0))
    def _():
        fetch(block0 + step + 1, 1 - slot)

    ybuf[...] = xp_ref[...].astype(F32).reshape(MOE_TSB * TOK_SUB, LANES)
    lhs = _tiles_to_rows(ybuf, MOE_TSB).astype(BF16)
    act = jax.nn.silu(_dot(lhs, sg_ref[...].astype(BF16))) * _dot(lhs, su_ref[...].astype(BF16))
    o_ref[...] = _dot(act.astype(BF16), sd_ref[...].astype(BF16))

    def wait_one(p, carry):
        piece_copy(0, 0, slot).wait()
        return carry

    lax.fori_loop(0, npending[slot], wait_one, 0)

    t0 = step * MOE_TSB
    yblk = ysb.at[slot]

    def weighted_row(p):
        w = jnp.broadcast_to(wk_ref[p], (TOK_SUB, LANES))
        pair = yblk[ltile_ref[p]].astype(F32)
        second = lax.bitcast_convert_type(w, jnp.int32) < 0
        return jnp.abs(w) * jnp.where(second, pair[TOK_SUB:], pair[:TOK_SUB])

    def per_token(t, carry):
        p = (t0 + t) * TOP_K
        acc = weighted_row(p)
        for k in range(1, TOP_K):
            acc = acc + weighted_row(p + k)
        ybuf[pl.ds(pl.multiple_of(t * TOK_SUB, TOK_SUB), TOK_SUB), :] = acc
        return carry

    lax.fori_loop(0, MOE_TSB, per_token, 0, unroll=8)
    routed = _tiles_to_rows(ybuf, MOE_TSB)
    o_ref[...] = x_ref[...] + mod_ref[5:6, :] * _rms(routed + o_ref[...], g_ref[3:4, :])


def _moe_combine(plan, ltile, wk, block0, ys, xp, s_gate, s_up, s_down, layer, x1, mod, g, rows_per_seq,
                 first_row):
    n = x1.shape[0]
    cap = TOP_K * MOE_TSB + N_EXPERTS * MOE_PIECE
    grid_spec = pltpu.PrefetchScalarGridSpec(
        num_scalar_prefetch=5,
        grid=(n // MOE_TSB,),
        in_specs=[
            pl.BlockSpec(memory_space=pl.ANY),
            pl.BlockSpec((MOE_TSB // 2, 2 * TOK_SUB, LANES), lambda s, *_: (s, 0, 0)),
            pl.BlockSpec((None, D_MODEL, D_SHARED), lambda s, *_: (layer, 0, 0)),
            pl.BlockSpec((None, D_MODEL, D_SHARED), lambda s, *_: (layer, 0, 0)),
            pl.BlockSpec((None, D_SHARED, D_MODEL), lambda s, *_: (layer, 0, 0)),
            pl.BlockSpec((MOE_TSB, D_MODEL), lambda s, *_: (s, 0)),
            _mod_spec(MOE_TSB, rows_per_seq, first_row),
            pl.BlockSpec((4, D_MODEL), lambda s, *_: (0, 0)),
        ],
        out_specs=pl.BlockSpec((MOE_TSB, D_MODEL), lambda s, *_: (s, 0)),
        scratch_shapes=[pltpu.VMEM((2, cap // 2, 2 * TOK_SUB, LANES), BF16),
                        pltpu.VMEM((MOE_TSB * TOK_SUB, LANES), F32),
                        pltpu.SMEM((2,), jnp.int32),
                        pltpu.SemaphoreType.DMA((2,))],
    )
    return pl.pallas_call(
        functools.partial(_moe_combine_kernel, block0=block0),
        out_shape=jax.ShapeDtypeStruct((n, D_MODEL), F32),
        grid_spec=grid_spec,
        compiler_params=_params(("arbitrary",)),
        name="moe_combine",
    )(plan["cstart"], plan["npiece"], plan["lbase"], ltile, wk, ys, xp, s_gate, s_up, s_down, x1, mod, g)


def _rope(x, cos, sin_up, sin_dn):
    out = []
    for h in range(N_HEADS):
        xs = x[:, h * V_DIM:(h + 1) * V_DIM]
        up = pltpu.roll(xs, V_DIM - ROPE_PAIRS, 1)
        dn = pltpu.roll(xs, ROPE_PAIRS, 1)
        out.append(xs * cos + up * sin_up + dn * sin_dn)
    return jnp.concatenate(out, axis=1)


def _qkv_kernel(x_ref, mod_ref, g_ref, w_ref, *rest, rope):
    if rope:
        cos_ref, sup_ref, sdn_ref, q_ref, k_ref, v_ref = rest
    else:
        q_ref, k_ref, v_ref = rest
    h = _mod_in(x_ref[...], g_ref[...], mod_ref[0:1, :], mod_ref[1:2, :])
    qkv = _dot(h.astype(BF16), w_ref[...])
    q, k, v = qkv[:, :QK_W], qkv[:, QK_W:2 * QK_W], qkv[:, 2 * QK_W:]
    if rope:
        q = _rope(q, cos_ref[...], sup_ref[...], sdn_ref[...])
        k = _rope(k, cos_ref[...], sup_ref[...], sdn_ref[...])
    q_ref[...] = (q * HEAD_DIM ** -0.5).astype(q_ref.dtype)
    k_ref[...] = k.astype(k_ref.dtype)
    v_ref[...] = v.astype(v_ref.dtype)


def _qkv(x, mod, g, w_qkv, rows_per_seq, first_row, rope_tables, kv_dtype):
    n = x.shape[0]
    tm = 512
    rope = rope_tables is not None
    in_specs = [
        pl.BlockSpec((tm, D_MODEL), lambda i: (i, 0)),
        _mod_spec(tm, rows_per_seq, first_row),
        pl.BlockSpec((1, D_MODEL), lambda i: (0, 0)),
        pl.BlockSpec((D_MODEL, 3 * QK_W), lambda i: (0, 0)),
    ]
    args = [x, mod, g, w_qkv]
    if rope:
        tiles_per_seq = rows_per_seq // tm
        in_specs += [pl.BlockSpec((tm, V_DIM), lambda i: (i % tiles_per_seq, 0))] * 3
        args += list(rope_tables)
    return pl.pallas_call(
        functools.partial(_qkv_kernel, rope=rope),
        out_shape=(jax.ShapeDtypeStruct((n, QK_W), BF16),
                   jax.ShapeDtypeStruct((n, QK_W), kv_dtype),
                   jax.ShapeDtypeStruct((n, N_HEADS * V_DIM), kv_dtype)),
        grid=(n // tm,),
        in_specs=in_specs,
        out_specs=(pl.BlockSpec((tm, QK_W), lambda i: (i, 0)),) * 3,
        compiler_params=_params(("arbitrary",)),
        name="attn_qkv_rope" if rope else "attn_qkv",
    )(*args)


def _softmax_parts(s):
    e = jnp.exp(s - jnp.max(s, axis=-1, keepdims=True))
    return e, 1.0 / jnp.sum(e, axis=-1, keepdims=True)


def _attn_kernel(lp_ref, sub_ref, q_ref, k_ref, v_ref, o_ref, *, lam_init):
    lp = lp_ref[...]
    lam = (jnp.exp(jnp.sum(lp[0:1, :] * lp[1:2, :], axis=1, keepdims=True))
           - jnp.exp(jnp.sum(lp[2:3, :] * lp[3:4, :], axis=1, keepdims=True)) + lam_init)
    for h in range(N_HEADS):
        c0 = h * V_DIM
        q = q_ref[:, c0:c0 + V_DIM]
        k = k_ref[:, c0:c0 + V_DIM].astype(BF16)
        e1, r1 = _softmax_parts(_dot_nt(q[:, :HEAD_DIM], k[:, :HEAD_DIM]))
        e2, r2 = _softmax_parts(_dot_nt(q[:, HEAD_DIM:], k[:, HEAD_DIM:]))
        w = e1 * r1 - lam * (e2 * r2)
        o = _dot(w.astype(BF16), v_ref[:, c0:c0 + V_DIM].astype(BF16))
        o_ref[:, c0:c0 + V_DIM] = (_rms(o, sub_ref[...]) * (1.0 - lam_init)).astype(BF16)


def _attention(lp, subln, q, k, v, lam_init, tq):
    bsz, t = q.shape[:2]
    tk = k.shape[1]
    return pl.pallas_call(
        functools.partial(_attn_kernel, lam_init=lam_init),
        out_shape=jax.ShapeDtypeStruct((bsz, t, N_HEADS * V_DIM), BF16),
        grid=(bsz, t // tq),
        in_specs=[
            pl.BlockSpec((4, HEAD_DIM), lambda b, i: (0, 0)),
            pl.BlockSpec((1, V_DIM), lambda b, i: (0, 0)),
            pl.BlockSpec((None, tq, QK_W), lambda b, i: (b, i, 0)),
            pl.BlockSpec((None, tk, QK_W), lambda b, i: (b, 0, 0)),
            pl.BlockSpec((None, tk, N_HEADS * V_DIM), lambda b, i: (b, 0, 0)),
        ],
        out_specs=pl.BlockSpec((None, tq, N_HEADS * V_DIM), lambda b, i: (b, i, 0)),
        compiler_params=_params(("arbitrary", "arbitrary")),
        name="diff_attention",
    )(lp, subln, q, k, v)


def _lru_gate_layout(gate_w, gate_b):
    nb = LRU_BLOCKS // 2
    w = gate_w.reshape(2, 2, 2, nb, LRU_BLOCK, LRU_BLOCK)
    bd = jnp.einsum("dgsbio,bc->sbidgco", w, jnp.eye(nb, dtype=w.dtype))
    bd = bd.reshape(2, LRU_HALF, 4 * LRU_HALF).astype(BF16)
    b = gate_b.reshape(2, 2, 2, LRU_HALF).transpose(2, 0, 1, 3).reshape(2, 1, 4 * LRU_HALF)
    return bd, b


def _rope_tables(n):
    rows = n // GRID_W
    row = jnp.repeat(jnp.arange(rows, dtype=F32), GRID_W)
    col = jnp.tile(jnp.arange(GRID_W, dtype=F32), rows)
    freqs = ROPE_THETA ** (-jnp.arange(ROPE_PAIRS, dtype=F32) / ROPE_PAIRS)
    ar = row[:, None] * freqs
    ac = col[:, None] * freqs
    zero = jnp.zeros_like(ar)
    cos = jnp.concatenate([jnp.cos(ar), jnp.cos(ar), jnp.cos(ac), jnp.cos(ac)], axis=-1)
    sin_up = jnp.concatenate([-jnp.sin(ar), zero, -jnp.sin(ac), zero], axis=-1)
    sin_dn = jnp.concatenate([zero, jnp.sin(ar), zero, jnp.sin(ac)], axis=-1)
    return tuple(jnp.tile(t, (1, 2)) for t in (cos, sin_up, sin_dn))


def kernel(x_prompt, x_sample, c, state_lru, cache_k, cache_v, c_ctx, mod_w, mod_b, norm_g,
           lru_w_in, lru_conv_w, lru_conv_b, lru_gate_w, lru_gate_b, lru_lambda, lru_w_out,
           attn_w_qkv, attn_lambda, attn_subln, attn_w_o,
           moe_router, moe_router_bias, moe_w_gate, moe_w_up, moe_w_down,
           shared_w_gate, shared_w_up, shared_w_down):
    bp, tp = x_prompt.shape[:2]
    bs, ts = x_sample.shape[:2]
    past = cache_k.shape[2]
    cond = jnp.concatenate([c_ctx[None], c, jnp.zeros((MOD_ROWS - 1 - bs, D_MODEL), F32)], axis=0)
    mod = _modulation(cond, mod_w, mod_b)
    streams = [dict(x=x_prompt.reshape(bp * tp, D_MODEL), rps=0, row=0, b=bp, t=tp),
               dict(x=x_sample.reshape(bs * ts, D_MODEL), rps=ts, row=1, b=bs, t=ts)]
    new_lru, new_k, new_v = [], [], []
    for i in range(DEPTH):
        j = i // N_MIXERS
        g = norm_g[i]
        router_t = moe_router[i].T
        router_b = moe_router_bias[i].reshape(N_EXPERTS, 1)
        if i % N_MIXERS == 0:
            w_in = lru_w_in[j].astype(BF16)
            w_mix = lru_w_out[j].astype(BF16)
            gate_w, gate_b = _lru_gate_layout(lru_gate_w[j], lru_gate_b[j])
        else:
            lam_init = 0.8 - 0.6 * math.exp(-0.3 * i)
            w_qkv = attn_w_qkv[j].astype(BF16)
            w_mix = attn_w_o[j].astype(BF16)
            tables = _rope_tables(ts)
        for si, s in enumerate(streams):
            if i % N_MIXERS == 0:
                gate, xr = _lru_in(s["x"], mod[i], g[0:1], w_in, s["rps"], s["row"])
                h0 = jnp.zeros((bp, 2, D_RNN), F32) if si == 0 else state_lru[:, j]
                m, h_last = _lru_core(xr, gate, h0, lru_conv_w[j], lru_conv_b[j][None], gate_w, gate_b,
                                      lru_lambda[j], s["b"], s["t"])
                if si == 0:
                    new_lru.append(h_last)
            else:
                if si == 0:
                    q, k, v = _qkv(s["x"], mod[i], g[0:1], w_qkv, s["rps"], s["row"], None, F32)
                    new_k.append(k.reshape(bp, tp, N_HEADS, 2 * HEAD_DIM))
                    new_v.append(v.reshape(bp, tp, N_HEADS, V_DIM))
                    k3 = k.reshape(bp, tp, QK_W)
                    v3 = v.reshape(bp, tp, N_HEADS * V_DIM)
                    tq = tp
                else:
                    q, k, v = _qkv(s["x"], mod[i], g[0:1], w_qkv, s["rps"], s["row"], tables, BF16)
                    k3 = jnp.concatenate([cache_k[:, j].reshape(bs, past, QK_W).astype(BF16),
                                          k.reshape(bs, ts, QK_W)], axis=1)
                    v3 = jnp.concatenate([cache_v[:, j].reshape(bs, past, N_HEADS * V_DIM).astype(BF16),
                                          v.reshape(bs, ts, N_HEADS * V_DIM)], axis=1)
                    tq = 512
                o = _attention(attn_lambda[j], attn_subln[j][None], q.reshape(s["b"], s["t"], QK_W),
                               k3, v3, lam_init, tq)
                m = o.reshape(s["b"] * s["t"], N_HEADS * V_DIM)
            s["x1"], s["xp"], s["comb"] = _mix_out(m, w_mix, s["x"], mod[i], g, router_t, router_b,
                                                   s["rps"], s["row"])
        xp = jnp.concatenate([s["xp"] for s in streams], axis=0)
        ek, rk, wk, cnt = _moe_plan(jnp.concatenate([s["comb"] for s in streams], axis=1))
        plan = _moe_layout(ek, rk, wk, cnt)
        ys = _moe_experts(plan, xp, moe_w_gate, moe_w_up, moe_w_down, i)
        tok0 = 0
        for s in streams:
            n = s["x1"].shape[0]
            pairs = slice(tok0 * TOP_K, (tok0 + n) * TOP_K)
            s["x"] = _moe_combine(plan, plan["ltile"][pairs], plan["wk"][pairs], tok0 // MOE_TSB, ys, s["xp"],
                                  shared_w_gate, shared_w_up,
                                  shared_w_down, i, s["x1"], mod[i], g, s["rps"], s["row"])
            tok0 += n
    return (streams[0]["x"].reshape(bp, tp, D_MODEL),
            streams[1]["x"].reshape(bs, ts, D_MODEL),
            jnp.stack(new_lru, axis=1),
            jnp.stack(new_k, axis=1),
            jnp.stack(new_v, axis=1))
```

```python
import functools
import math

import jax
import jax.numpy as jnp
from jax import lax
from jax.experimental import pallas as pl
from jax.experimental.pallas import tpu as pltpu

D_MODEL = 1024
DEPTH = 2
N_MIXERS = 2
GRID_W = 64
EPS = 1e-6
D_RNN = 1280
LRU_BLOCKS = 16
LRU_BLOCK = D_RNN // LRU_BLOCKS
CONV_W = 4
CONV_LEFT = 2
LRU_C = 8.0
N_HEADS = 8
HEAD_DIM = 64
V_DIM = 2 * HEAD_DIM
QK_W = N_HEADS * 2 * HEAD_DIM
ROPE_PAIRS = HEAD_DIM // 4
ROPE_THETA = 10000.0
N_EXPERTS = 64
N_GROUPS = 8
GROUP_SIZE = N_EXPERTS // N_GROUPS
TOPK_GROUPS = 4
TOP_K = 8
D_EXPERT = 256
D_SHARED = 256
ROUTED_SCALE = 2.5

MOD_ROWS = 8
LRU_HALF = D_RNN // 2
CONV_PAD = 8
MOE_TM = 512
MOE_ROW_GROUPS = 1
MOE_TSB = 512
MOE_CHUNK_ALIGN = 8
MOE_PIECE = 64
LANES = 128
LRU_STEP_SUB = 8
TOK_SUB = D_MODEL // LANES
VMEM_LIMIT = 56 * 1024 * 1024
BF16 = jnp.bfloat16
F32 = jnp.float32
NEG_INF = float("-inf")


def _params(sem):
    return pltpu.CompilerParams(dimension_semantics=sem, vmem_limit_bytes=VMEM_LIMIT)


def _rms(x, g):
    return x * lax.rsqrt(jnp.mean(x * x, axis=-1, keepdims=True) + EPS) * g


def _mod_in(x, g, shift, scale):
    return _rms(x, g) * (1.0 + scale) + shift


def _dot(a, b):
    return jnp.dot(a, b, preferred_element_type=F32)


def _dot_nt(a, b, precision=None):
    return lax.dot_general(a, b, (((1,), (1,)), ((), ())), precision=precision,
                           preferred_element_type=F32)


def _mod_kernel(cond_ref, w_ref, b_ref, o_ref):
    cnd = cond_ref[...]
    s = cnd * jax.nn.sigmoid(cnd)
    o_ref[...] = jnp.dot(s, w_ref[...], precision=lax.Precision.HIGHEST,
                         preferred_element_type=F32) + b_ref[...]


def _modulation(cond, mod_w, mod_b):
    tn = 1536
    out = pl.pallas_call(
        _mod_kernel,
        out_shape=jax.ShapeDtypeStruct((DEPTH, MOD_ROWS, 6 * D_MODEL), F32),
        grid=(DEPTH, 6 * D_MODEL // tn),
        in_specs=[
            pl.BlockSpec((MOD_ROWS, D_MODEL), lambda l, n: (0, 0)),
            pl.BlockSpec((None, D_MODEL, tn), lambda l, n: (l, 0, n)),
            pl.BlockSpec((None, 1, tn), lambda l, n: (l, 0, n)),
        ],
        out_specs=pl.BlockSpec((None, MOD_ROWS, tn), lambda l, n: (l, 0, n)),
        compiler_params=_params(("arbitrary", "arbitrary")),
        name="modulation",
    )(cond, mod_w, mod_b.reshape(DEPTH, 1, 6 * D_MODEL))
    return out.reshape(DEPTH, MOD_ROWS, 6, D_MODEL)


def _mod_spec(tm, rows_per_seq, first_row):
    def index(i, *_):
        return (first_row + (i * tm) // rows_per_seq if rows_per_seq else first_row, 0, 0)
    return pl.BlockSpec((None, 6, D_MODEL), index)


def _lru_in_kernel(x_ref, mod_ref, g_ref, w_ref, gate_ref, xr_ref):
    h = _mod_in(x_ref[...], g_ref[...], mod_ref[0:1, :], mod_ref[1:2, :])
    xb = _dot(h.astype(BF16), w_ref[...])
    gate_ref[...] = jax.nn.gelu(xb[:, :D_RNN]).astype(BF16)
    xr_ref[...] = xb[:, D_RNN:]


def _lru_in(x, mod, g, w_in, rows_per_seq, first_row):
    n = x.shape[0]
    tm = 512
    return pl.pallas_call(
        _lru_in_kernel,
        out_shape=(jax.ShapeDtypeStruct((n, D_RNN), BF16), jax.ShapeDtypeStruct((n, D_RNN), F32)),
        grid=(n // tm,),
        in_specs=[
            pl.BlockSpec((tm, D_MODEL), lambda i: (i, 0)),
            _mod_spec(tm, rows_per_seq, first_row),
            pl.BlockSpec((1, D_MODEL), lambda i: (0, 0)),
            pl.BlockSpec((D_MODEL, 2 * D_RNN), lambda i: (0, 0)),
        ],
        out_specs=(pl.BlockSpec((tm, D_RNN), lambda i: (i, 0)),
                   pl.BlockSpec((tm, D_RNN), lambda i: (i, 0))),
        compiler_params=_params(("arbitrary",)),
        name="lru_in",
    )(x, mod, g, w_in)


def _lru_core_kernel(xr_ref, gate_ref, h0_ref, cw_ref, cb_ref, gw_ref, gb_ref, lam_ref,
                     m_ref, hl_ref, xpad, a_f, u_f, a_b, u_b, hs, *, seq, chunk):
    c = LRU_HALF
    nslab = c // LANES

    @pl.when(jnp.logical_and(pl.program_id(0) == 0, pl.program_id(1) == 0))
    def _():
        for buf in (a_f, u_f, a_b, u_b, hs):
            buf[...] = jnp.zeros(buf.shape, F32)

    xpad[0:CONV_PAD, :] = jnp.zeros((CONV_PAD, c), F32)
    xpad[CONV_PAD + seq:, :] = jnp.zeros((CONV_PAD, c), F32)
    xpad[CONV_PAD:CONV_PAD + seq, :] = xr_ref[...]
    lam = lam_ref[...]
    coef = -LRU_C * jax.nn.softplus(-lam)
    for r0 in range(0, seq, chunk):
        xr = cb_ref[...]
        for j in range(CONV_W):
            off = CONV_PAD - CONV_LEFT + j + r0
            xr = xr + xpad[off:off + chunk, :] * cw_ref[j:j + 1, :]
        g = _dot(xr.astype(BF16), gw_ref[...]) + gb_ref[...]
        for d, (a_s, u_s) in enumerate(((a_f, u_f), (a_b, u_b))):
            r = 0.5 * (jnp.tanh(0.5 * g[:, 2 * d * c:(2 * d + 1) * c]) + 1.0)
            i = 0.5 * (jnp.tanh(0.5 * g[:, (2 * d + 1) * c:(2 * d + 2) * c]) + 1.0)
            log_a = coef[d:d + 1, :] * r
            a = jnp.exp(log_a)
            u = jnp.sqrt(-jnp.tanh(log_a) * (a * a + 1.0)) * (i * xr)
            for s in range(nslab):
                rows = pl.ds(r0 * LRU_STEP_SUB + s, chunk, stride=LRU_STEP_SUB)
                a_s[rows, :] = a[:, s * LANES:(s + 1) * LANES]
                u_s[rows, :] = u[:, s * LANES:(s + 1) * LANES]

    for d in range(2):
        for s in range(nslab):
            hs[d * LRU_STEP_SUB + s:d * LRU_STEP_SUB + s + 1, :] = h0_ref[d:d + 1, s * LANES:(s + 1) * LANES]

    def step(t, carry):
        hf, hb = carry
        rf = pl.ds(pl.multiple_of(t * LRU_STEP_SUB, LRU_STEP_SUB), LRU_STEP_SUB)
        rb = pl.ds(pl.multiple_of((seq - 1 - t) * LRU_STEP_SUB, LRU_STEP_SUB), LRU_STEP_SUB)
        hf = a_f[rf, :] * hf + u_f[rf, :]
        u_f[rf, :] = hf
        hb = a_b[rb, :] * hb + u_b[rb, :]
        u_b[rb, :] = hb
        return hf, hb

    hf, hb = lax.fori_loop(0, seq, step, (hs[0:LRU_STEP_SUB, :], hs[LRU_STEP_SUB:, :]), unroll=16)
    hs[0:LRU_STEP_SUB, :] = hf
    hs[LRU_STEP_SUB:, :] = hb
    for d in range(2):
        for s in range(nslab):
            hl_ref[d:d + 1, s * LANES:(s + 1) * LANES] = hs[d * LRU_STEP_SUB + s:d * LRU_STEP_SUB + s + 1, :]
    y = jnp.concatenate([u_f[pl.ds(s, seq, stride=LRU_STEP_SUB), :] + u_b[pl.ds(s, seq, stride=LRU_STEP_SUB), :]
                         for s in range(nslab)], axis=1)
    m_ref[...] = (y * gate_ref[...].astype(F32)).astype(BF16)


def _lru_core(xr, gate, h0, conv_w, conv_b, gate_w, gate_b, lam, n_seq, seq):
    c = LRU_HALF
    chunk = min(seq, 256)
    kernel = functools.partial(_lru_core_kernel, seq=seq, chunk=chunk)
    return pl.pallas_call(
        kernel,
        out_shape=(jax.ShapeDtypeStruct((n_seq * seq, D_RNN), BF16),
                   jax.ShapeDtypeStruct((n_seq, 2, D_RNN), F32)),
        grid=(n_seq, 2),
        in_specs=[
            pl.BlockSpec((seq, c), lambda b, j: (b, j)),
            pl.BlockSpec((seq, c), lambda b, j: (b, j)),
            pl.BlockSpec((None, 2, c), lambda b, j: (b, 0, j)),
            pl.BlockSpec((CONV_W, c), lambda b, j: (0, j)),
            pl.BlockSpec((1, c), lambda b, j: (0, j)),
            pl.BlockSpec((None, c, 4 * c), lambda b, j: (j, 0, 0)),
            pl.BlockSpec((None, 1, 4 * c), lambda b, j: (j, 0, 0)),
            pl.BlockSpec((2, c), lambda b, j: (0, j)),
        ],
        out_specs=(pl.BlockSpec((seq, c), lambda b, j: (b, j)),
                   pl.BlockSpec((None, 2, c), lambda b, j: (b, 0, j))),
        scratch_shapes=([pltpu.VMEM((seq + 2 * CONV_PAD, c), F32)]
                        + [pltpu.VMEM((seq * LRU_STEP_SUB, LANES), F32)] * 4
                        + [pltpu.VMEM((2 * LRU_STEP_SUB, LANES), F32)]),
        compiler_params=_params(("arbitrary", "arbitrary")),
        name="lru_core",
    )(xr, gate, h0, conv_w, conv_b, gate_w, gate_b, lam)


def _route(sel, scores):
    tm = sel.shape[1]
    io8 = lax.broadcasted_iota(jnp.int32, (GROUP_SIZE, tm), 0)
    blocks, gscore = [], []
    for g in range(N_GROUPS):
        blk = sel[g * GROUP_SIZE:(g + 1) * GROUP_SIZE, :]
        m1 = jnp.max(blk, axis=0, keepdims=True)
        first = jnp.min(jnp.where(blk == m1, io8, GROUP_SIZE), axis=0, keepdims=True)
        m2 = jnp.max(jnp.where(io8 == first, NEG_INF, blk), axis=0, keepdims=True)
        blocks.append(blk)
        gscore.append(m1 + m2)
    masked = []
    for g in range(N_GROUPS):
        rank = jnp.zeros((1, tm), jnp.int32)
        for o in range(N_GROUPS):
            if o == g:
                continue
            beats = (gscore[o] >= gscore[g]) if o < g else (gscore[o] > gscore[g])
            rank = rank + beats.astype(jnp.int32)
        masked.append(jnp.where(rank < TOPK_GROUPS, blocks[g], NEG_INF))
    v = jnp.concatenate(masked, axis=0)
    ioe = lax.broadcasted_iota(jnp.int32, (N_EXPERTS, tm), 0)
    chosen = jnp.zeros((N_EXPERTS, tm), F32)
    for _ in range(TOP_K):
        mx = jnp.max(v, axis=0, keepdims=True)
        first = jnp.min(jnp.where(v == mx, ioe, N_EXPERTS), axis=0, keepdims=True)
        pick = ioe == first
        chosen = jnp.where(pick, 1.0, chosen)
        v = jnp.where(pick, NEG_INF, v)
    wsel = chosen * scores
    comb = wsel / jnp.sum(wsel, axis=0, keepdims=True) * ROUTED_SCALE
    return jnp.where(chosen > 0.0, comb, -1.0)


def _rows_to_tiles(x, tmp):
    tm = x.shape[0]
    for c in range(TOK_SUB):
        tmp[pl.ds(c, tm, stride=TOK_SUB), :] = x[:, c * LANES:(c + 1) * LANES]
    return tmp[...].reshape(tm // 2, 2 * TOK_SUB, LANES).astype(BF16)


def _tiles_to_rows(tmp, tm):
    return jnp.concatenate([tmp[pl.ds(c, tm, stride=TOK_SUB), :] for c in range(TOK_SUB)], axis=1)


SECOND_HALF = -2 ** 31


def _token_code(t):
    return jnp.where((t & 1) == 1, (t >> 1) | SECOND_HALF, t >> 1)


def _load_token(pairs_ref, code):
    pair = pairs_ref[code & (2 ** 31 - 1)].astype(F32)
    second = jnp.broadcast_to(code, (TOK_SUB, LANES)) < 0
    return jnp.where(second, pair[TOK_SUB:], pair[:TOK_SUB])


def _mix_out_kernel(m_ref, w_ref, x_ref, mod_ref, g_ref, rt_ref, rb_ref, x1_ref, h2_ref, comb_ref, tmp):
    y = _dot(m_ref[...], w_ref[...])
    x1 = x_ref[...] + mod_ref[2:3, :] * _rms(y, g_ref[1:2, :])
    x1_ref[...] = x1
    h2 = _mod_in(x1, g_ref[2:3, :], mod_ref[3:4, :], mod_ref[4:5, :])
    h2_ref[...] = _rows_to_tiles(h2, tmp)
    logits = _dot_nt(rt_ref[...], h2, precision=lax.Precision.HIGHEST)
    scores = jax.nn.sigmoid(logits)
    comb_ref[...] = _route(scores + rb_ref[...], scores)


def _mix_out(m, w, x, mod, g, router_t, router_b, rows_per_seq, first_row):
    n, k = m.shape
    tm = 512
    return pl.pallas_call(
        _mix_out_kernel,
        out_shape=(jax.ShapeDtypeStruct((n, D_MODEL), F32),
                   jax.ShapeDtypeStruct((n // 2, 2 * TOK_SUB, LANES), BF16),
                   jax.ShapeDtypeStruct((N_EXPERTS, n), F32)),
        grid=(n // tm,),
        in_specs=[
            pl.BlockSpec((tm, k), lambda i: (i, 0)),
            pl.BlockSpec((k, D_MODEL), lambda i: (0, 0)),
            pl.BlockSpec((tm, D_MODEL), lambda i: (i, 0)),
            _mod_spec(tm, rows_per_seq, first_row),
            pl.BlockSpec((4, D_MODEL), lambda i: (0, 0)),
            pl.BlockSpec((N_EXPERTS, D_MODEL), lambda i: (0, 0)),
            pl.BlockSpec((N_EXPERTS, 1), lambda i: (0, 0)),
        ],
        out_specs=(pl.BlockSpec((tm, D_MODEL), lambda i: (i, 0)),
                   pl.BlockSpec((tm // 2, 2 * TOK_SUB, LANES), lambda i: (i, 0, 0)),
                   pl.BlockSpec((N_EXPERTS, tm), lambda i: (0, i))),
        scratch_shapes=[pltpu.VMEM((tm * TOK_SUB, LANES), F32)],
        compiler_params=_params(("arbitrary",)),
        name="mix_out",
    )(m, w, x, mod, g, router_t, router_b)


def _moe_plan_kernel(comb_ref, ek_ref, rk_ref, wk_ref, cnt_ref):
    c = comb_ref[...]
    n = c.shape[1]
    chosen = c >= 0.0
    ch = chosen.astype(BF16)
    tri = (lax.broadcasted_iota(jnp.int32, (n, n), 0) <= lax.broadcasted_iota(jnp.int32, (n, n), 1))
    incl = _dot(ch, tri.astype(BF16))
    rank = incl - ch.astype(F32)
    cnt_ref[...] = jnp.broadcast_to(incl[:, n - 1:n], cnt_ref.shape)
    low = (lax.broadcasted_iota(jnp.int32, (N_EXPERTS, N_EXPERTS), 1)
           < lax.broadcasted_iota(jnp.int32, (N_EXPERTS, N_EXPERTS), 0))
    slot = _dot(low.astype(BF16), ch)
    ioe = lax.broadcasted_iota(jnp.int32, c.shape, 0).astype(F32)
    for k in range(TOP_K):
        sel = jnp.logical_and(chosen, slot == float(k))
        ek_ref[k:k + 1, :] = jnp.sum(jnp.where(sel, ioe, 0.0), axis=0, keepdims=True).astype(jnp.int32)
        rk_ref[k:k + 1, :] = jnp.sum(jnp.where(sel, rank, 0.0), axis=0, keepdims=True).astype(jnp.int32)
        wk_ref[k:k + 1, :] = jnp.sum(jnp.where(sel, c, 0.0), axis=0, keepdims=True)


def _moe_plan(comb):
    n = comb.shape[1]
    nsb = n // MOE_TSB
    out_i = jax.ShapeDtypeStruct((TOP_K, n), jnp.int32)
    pair_spec = pl.BlockSpec((TOP_K, MOE_TSB), lambda s: (0, s))
    return pl.pallas_call(
        _moe_plan_kernel,
        out_shape=(out_i, out_i, jax.ShapeDtypeStruct((TOP_K, n), F32),
                   jax.ShapeDtypeStruct((N_EXPERTS, nsb * 128), F32)),
        grid=(nsb,),
        in_specs=[pl.BlockSpec((N_EXPERTS, MOE_TSB), lambda s: (0, s))],
        out_specs=(pair_spec, pair_spec, pair_spec, pl.BlockSpec((N_EXPERTS, 128), lambda s: (0, s))),
        compiler_params=_params(("arbitrary",)),
        name="moe_plan",
    )(comb)


def _moe_num_tiles(n_tok):
    rows = n_tok * TOP_K + N_EXPERTS * (n_tok // MOE_TSB) * (MOE_CHUNK_ALIGN - 1)
    return rows // MOE_TM + N_EXPERTS + 1


def _ceil_to(x, m):
    return (x + m - 1) // m * m


def _moe_layout(ek, rk, wk, cnt):
    n = ek.shape[1]
    nsb = n // MOE_TSB
    nt = _moe_num_tiles(n)
    n_es = cnt[:, ::128].astype(jnp.int32)
    c_al = _ceil_to(n_es, MOE_CHUNK_ALIGN)
    tiles_e = _ceil_to(jnp.sum(c_al, axis=1), MOE_TM) // MOE_TM
    tile0 = jnp.cumsum(tiles_e) - tiles_e
    cstart = tile0[:, None] * MOE_TM + jnp.cumsum(c_al, axis=1) - c_al
    npiece = _ceil_to(n_es, MOE_PIECE) // MOE_PIECE
    lbase = (jnp.cumsum(npiece, axis=0) - npiece) * MOE_PIECE
    dest, ldest = _moe_dest(ek, rk, cstart, lbase)
    dest_blocks = dest.reshape(TOP_K, nsb, MOE_TSB).transpose(1, 0, 2).reshape(-1)
    row_token = _moe_invert(dest_blocks, nt * MOE_TM)
    tile_expert = jnp.sum(jnp.arange(nt, dtype=jnp.int32)[:, None] >= tile0[None, :], axis=1) - 1
    wsigned = jnp.where((ldest & 1) == 1, -wk, wk)
    return dict(row_token=row_token,
                tile_expert=tile_expert.astype(jnp.int32), n_used=jnp.sum(tiles_e).reshape(1).astype(jnp.int32),
                cstart=cstart.T.reshape(-1), npiece=npiece.T.reshape(-1), lbase=lbase.T.reshape(-1),
                ltile=(ldest >> 1).T.reshape(-1), wk=wsigned.T.reshape(-1))


def _moe_dest_kernel(ek_ref, rk_ref, cs_ref, lb_ref, dest_ref, ldest_ref):
    ioe = lax.broadcasted_iota(jnp.int32, (N_EXPERTS, ek_ref.shape[1]), 0)
    cs = cs_ref[:, 0:1]
    lb = lb_ref[:, 0:1]
    for k in range(TOP_K):
        hit = ioe == ek_ref[k:k + 1, :]
        rk = rk_ref[k:k + 1, :]
        dest_ref[k:k + 1, :] = jnp.sum(jnp.where(hit, cs, 0.0), axis=0, keepdims=True).astype(jnp.int32) + rk
        ldest_ref[k:k + 1, :] = jnp.sum(jnp.where(hit, lb, 0.0), axis=0, keepdims=True).astype(jnp.int32) + rk


def _moe_dest(ek, rk, cstart, lbase):
    n = ek.shape[1]
    pair_spec = pl.BlockSpec((TOP_K, MOE_TSB), lambda s: (0, s))
    table_spec = pl.BlockSpec((N_EXPERTS, LANES), lambda s: (0, s))
    out = jax.ShapeDtypeStruct((TOP_K, n), jnp.int32)
    spread = lambda tab: jnp.repeat(tab.astype(F32), LANES, axis=1)
    return pl.pallas_call(
        _moe_dest_kernel,
        out_shape=(out, out),
        grid=(n // MOE_TSB,),
        in_specs=[pair_spec, pair_spec, table_spec, table_spec],
        out_specs=(pair_spec, pair_spec),
        compiler_params=_params(("arbitrary",)),
        name="moe_dest",
    )(ek, rk, spread(cstart), spread(lbase))


def _moe_invert_kernel(dest_hbm, zeros_hbm, code_ref, rt_ref, dsm_even, dsm_odd, sems):
    s = pl.program_id(0)
    pairs = TOP_K * MOE_TSB
    bufs = (dsm_even, dsm_odd)

    def chunk_copy(blk, parity):
        return pltpu.make_async_copy(dest_hbm.at[pl.ds(pl.multiple_of(blk * pairs, pairs), pairs)],
                                     bufs[parity], sems.at[parity])

    @pl.when(s == 0)
    def _():
        chunk_copy(0, 0).start()
        fill = pltpu.make_async_copy(zeros_hbm, rt_ref, sems.at[2])
        fill.start()
        fill.wait()

    def block(parity):
        @pl.when(s + 1 < pl.num_programs(0))
        def _():
            chunk_copy(s + 1, 1 - parity).start()

        chunk_copy(s, parity).wait()
        dsm = bufs[parity]

        def per_token(t, carry):
            code = code_ref[s * MOE_TSB + t]
            for k in range(TOP_K):
                rt_ref[dsm[k * MOE_TSB + t]] = code
            return carry

        lax.fori_loop(0, MOE_TSB, per_token, 0, unroll=8)

    @pl.when(s % 2 == 0)
    def _():
        block(0)

    @pl.when(s % 2 == 1)
    def _():
        block(1)


def _moe_invert(dest_blocks, n_rows):
    pairs = TOP_K * MOE_TSB
    return pl.pallas_call(
        _moe_invert_kernel,
        out_shape=jax.ShapeDtypeStruct((n_rows,), jnp.int32),
        grid=(dest_blocks.shape[0] // pairs,),
        in_specs=[pl.BlockSpec(memory_space=pl.ANY), pl.BlockSpec(memory_space=pl.ANY),
                  pl.BlockSpec(memory_space=pltpu.SMEM)],
        out_specs=pl.BlockSpec(memory_space=pltpu.SMEM),
        scratch_shapes=[pltpu.SMEM((pairs,), jnp.int32), pltpu.SMEM((pairs,), jnp.int32),
                        pltpu.SemaphoreType.DMA((3,))],
        compiler_params=_params(("arbitrary",)),
        name="moe_invert",
    )(dest_blocks, jnp.zeros((n_rows,), jnp.int32),
      _token_code(jnp.arange(dest_blocks.shape[0] // TOP_K, dtype=jnp.int32)))


def _moe_expert_kernel(texp_ref, nused_ref, ids_ref, xp_ref, wg_ref, wu_ref, wd_ref, ys_ref,
                       xs_even, xs_odd, tmp, wgb, wub, wdb):
    i = pl.program_id(0)

    def gather(tile, xs):
        for r in range(MOE_TM):
            xs[r * TOK_SUB:(r + 1) * TOK_SUB, :] = _load_token(xp_ref, ids_ref[tile * MOE_TM + r])

    @pl.when(i == 0)
    def _():
        gather(0, xs_even)

    def tile_body(cur, nxt):
        gather(i + 1, nxt)
        rows = MOE_TM // MOE_ROW_GROUPS
        for grp in range(MOE_ROW_GROUPS):
            tok = slice(grp * rows * TOK_SUB, (grp + 1) * rows * TOK_SUB)
            lhs = _tiles_to_rows(cur.at[tok], rows).astype(BF16)
            act = jax.nn.silu(_dot(lhs, wgb[...])) * _dot(lhs, wub[...])
            ys_ref[grp * rows // 2:(grp + 1) * rows // 2] = _rows_to_tiles(
                _dot(act.astype(BF16), wdb[...]), tmp.at[tok])

    @pl.when(i < nused_ref[0])
    def _():
        @pl.when(jnp.logical_or(i == 0, texp_ref[i] != texp_ref[jnp.maximum(i - 1, 0)]))
        def _():
            wgb[...] = wg_ref[...].astype(BF16)
            wub[...] = wu_ref[...].astype(BF16)
            wdb[...] = wd_ref[...].astype(BF16)

        @pl.when(i % 2 == 0)
        def _():
            tile_body(xs_even, xs_odd)

        @pl.when(i % 2 == 1)
        def _():
            tile_body(xs_odd, xs_even)

    @pl.when(i >= nused_ref[0])
    def _():
        ys_ref[...] = jnp.zeros(ys_ref.shape, BF16)


def _moe_experts(plan, xp, w_gate, w_up, w_down, layer):
    nt = plan["tile_expert"].shape[0]
    grid_spec = pltpu.PrefetchScalarGridSpec(
        num_scalar_prefetch=3,
        grid=(nt,),
        in_specs=[
            pl.BlockSpec(memory_space=pltpu.VMEM),
            pl.BlockSpec((None, None, D_MODEL, D_EXPERT), lambda i, te, nu, ids: (layer, te[i], 0, 0)),
            pl.BlockSpec((None, None, D_MODEL, D_EXPERT), lambda i, te, nu, ids: (layer, te[i], 0, 0)),
            pl.BlockSpec((None, None, D_EXPERT, D_MODEL), lambda i, te, nu, ids: (layer, te[i], 0, 0)),
        ],
        out_specs=pl.BlockSpec((MOE_TM // 2, 2 * TOK_SUB, LANES), lambda i, te, nu, ids: (i, 0, 0)),
        scratch_shapes=[pltpu.VMEM((MOE_TM * TOK_SUB, LANES), F32),
                        pltpu.VMEM((MOE_TM * TOK_SUB, LANES), F32),
                        pltpu.VMEM((MOE_TM * TOK_SUB, LANES), F32),
                        pltpu.VMEM((D_MODEL, D_EXPERT), BF16),
                        pltpu.VMEM((D_MODEL, D_EXPERT), BF16),
                        pltpu.VMEM((D_EXPERT, D_MODEL), BF16)],
    )
    return pl.pallas_call(
        _moe_expert_kernel,
        out_shape=jax.ShapeDtypeStruct((nt * MOE_TM // 2, 2 * TOK_SUB, LANES), BF16),
        grid_spec=grid_spec,
        compiler_params=_params(("arbitrary",)),
        name="moe_experts",
    )(plan["tile_expert"], plan["n_used"], plan["row_token"], xp, w_gate, w_up, w_down)


def _moe_combine_kernel(cstart_ref, npiece_ref, lbase_ref, ltile_ref, wk_ref, ys_hbm, xp_ref, sg_ref, su_ref,
                        sd_ref, x_ref, mod_ref, g_ref, o_ref, ysb, ybuf, npending, sems, *, block0):
    step = pl.program_id(0)
    slot = step % 2

    def piece_copy(src_row, dst_row, buf):
        return pltpu.make_async_copy(ys_hbm.at[pl.ds(src_row // 2, MOE_PIECE // 2)],
                                     ysb.at[buf, pl.ds(dst_row // 2, MOE_PIECE // 2)], sems.at[buf])

    def fetch(blk, buf):
        def per_expert(e, total):
            j = blk * N_EXPERTS + e
            src, dst, npc = cstart_ref[j], lbase_ref[j], npiece_ref[j]

            def per_piece(p, carry):
                piece_copy(pl.multiple_of(src + p * MOE_PIECE, MOE_CHUNK_ALIGN),
                           pl.multiple_of(dst + p * MOE_PIECE, MOE_PIECE), buf).start()
                return carry

            lax.fori_loop(0, npc, per_piece, 0)
            return total + npc

        npending[buf] = lax.fori_loop(0, N_EXPERTS, per_expert, 0)

    @pl.when(step == 0)
    def _():
        fetch(block0, 0)

    @pl.when(step + 1 < pl.num_programs(0))
    def _():
        fetch(block0 + step + 1, 1 - slot)

    ybuf[...] = xp_ref[...].astype(F32).reshape(MOE_TSB * TOK_SUB, LANES)
    lhs = _tiles_to_rows(ybuf, MOE_TSB).astype(BF16)
    act = jax.nn.silu(_dot(lhs, sg_ref[...].astype(BF16))) * _dot(lhs, su_ref[...].astype(BF16))
    o_ref[...] = _dot(act.astype(BF16), sd_ref[...].astype(BF16))

    def wait_one(p, carry):
        piece_copy(0, 0, slot).wait()
        return carry

    lax.fori_loop(0, npending[slot], wait_one, 0)

    t0 = step * MOE_TSB
    yblk = ysb.at[slot]

    def weighted_row(p):
        w = jnp.broadcast_to(wk_ref[p], (TOK_SUB, LANES))
        pair = yblk[ltile_ref[p]].astype(F32)
        second = lax.bitcast_convert_type(w, jnp.int32) < 0
        return jnp.abs(w) * jnp.where(second, pair[TOK_SUB:], pair[:TOK_SUB])

    def per_token(t, carry):
        p = (t0 + t) * TOP_K
        acc = weighted_row(p)
        for k in range(1, TOP_K):
            acc = acc + weighted_row(p + k)
        ybuf[pl.ds(pl.multiple_of(t * TOK_SUB, TOK_SUB), TOK_SUB), :] = acc
        return carry

    lax.fori_loop(0, MOE_TSB, per_token, 0, unroll=8)
    routed = _tiles_to_rows(ybuf, MOE_TSB)
    o_ref[...] = x_ref[...] + mod_ref[5:6, :] * _rms(routed + o_ref[...], g_ref[3:4, :])


def _moe_combine(plan, ltile, wk, block0, ys, xp, s_gate, s_up, s_down, layer, x1, mod, g, rows_per_seq,
                 first_row):
    n = x1.shape[0]
    cap = TOP_K * MOE_TSB + N_EXPERTS * MOE_PIECE
    grid_spec = pltpu.PrefetchScalarGridSpec(
        num_scalar_prefetch=5,
        grid=(n // MOE_TSB,),
        in_specs=[
            pl.BlockSpec(memory_space=pl.ANY),
            pl.BlockSpec((MOE_TSB // 2, 2 * TOK_SUB, LANES), lambda s, *_: (s, 0, 0)),
            pl.BlockSpec((None, D_MODEL, D_SHARED), lambda s, *_: (layer, 0, 0)),
            pl.BlockSpec((None, D_MODEL, D_SHARED), lambda s, *_: (layer, 0, 0)),
            pl.BlockSpec((None, D_SHARED, D_MODEL), lambda s, *_: (layer, 0, 0)),
            pl.BlockSpec((MOE_TSB, D_MODEL), lambda s, *_: (s, 0)),
            _mod_spec(MOE_TSB, rows_per_seq, first_row),
            pl.BlockSpec((4, D_MODEL), lambda s, *_: (0, 0)),
        ],
        out_specs=pl.BlockSpec((MOE_TSB, D_MODEL), lambda s, *_: (s, 0)),
        scratch_shapes=[pltpu.VMEM((2, cap // 2, 2 * TOK_SUB, LANES), BF16),
                        pltpu.VMEM((MOE_TSB * TOK_SUB, LANES), F32),
                        pltpu.SMEM((2,), jnp.int32),
                        pltpu.SemaphoreType.DMA((2,))],
    )
    return pl.pallas_call(
        functools.partial(_moe_combine_kernel, block0=block0),
        out_shape=jax.ShapeDtypeStruct((n, D_MODEL), F32),
        grid_spec=grid_spec,
        compiler_params=_params(("arbitrary",)),
        name="moe_combine",
    )(plan["cstart"], plan["npiece"], plan["lbase"], ltile, wk, ys, xp, s_gate, s_up, s_down, x1, mod, g)


def _rope(x, cos, sin_up, sin_dn):
    out = []
    for h in range(N_HEADS):
        xs = x[:, h * V_DIM:(h + 1) * V_DIM]
        up = pltpu.roll(xs, V_DIM - ROPE_PAIRS, 1)
        dn = pltpu.roll(xs, ROPE_PAIRS, 1)
        out.append(xs * cos + up * sin_up + dn * sin_dn)
    return jnp.concatenate(out, axis=1)


def _qkv_kernel(x_ref, mod_ref, g_ref, w_ref, *rest, rope):
    if rope:
        cos_ref, sup_ref, sdn_ref, q_ref, k_ref, v_ref = rest
    else:
        q_ref, k_ref, v_ref = rest
    h = _mod_in(x_ref[...], g_ref[...], mod_ref[0:1, :], mod_ref[1:2, :])
    qkv = _dot(h.astype(BF16), w_ref[...])
    q, k, v = qkv[:, :QK_W], qkv[:, QK_W:2 * QK_W], qkv[:, 2 * QK_W:]
    if rope:
        q = _rope(q, cos_ref[...], sup_ref[...], sdn_ref[...])
        k = _rope(k, cos_ref[...], sup_ref[...], sdn_ref[...])
    q_ref[...] = (q * HEAD_DIM ** -0.5).astype(q_ref.dtype)
    k_ref[...] = k.astype(k_ref.dtype)
    v_ref[...] = v.astype(v_ref.dtype)


def _qkv(x, mod, g, w_qkv, rows_per_seq, first_row, rope_tables, kv_dtype):
    n = x.shape[0]
    tm = 512
    rope = rope_tables is not None
    in_specs = [
        pl.BlockSpec((tm, D_MODEL), lambda i: (i, 0)),
        _mod_spec(tm, rows_per_seq, first_row),
        pl.BlockSpec((1, D_MODEL), lambda i: (0, 0)),
        pl.BlockSpec((D_MODEL, 3 * QK_W), lambda i: (0, 0)),
    ]
    args = [x, mod, g, w_qkv]
    if rope:
        tiles_per_seq = rows_per_seq // tm
        in_specs += [pl.BlockSpec((tm, V_DIM), lambda i: (i % tiles_per_seq, 0))] * 3
        args += list(rope_tables)
    return pl.pallas_call(
        functools.partial(_qkv_kernel, rope=rope),
        out_shape=(jax.ShapeDtypeStruct((n, QK_W), BF16),
                   jax.ShapeDtypeStruct((n, QK_W), kv_dtype),
                   jax.ShapeDtypeStruct((n, N_HEADS * V_DIM), kv_dtype)),
        grid=(n // tm,),
        in_specs=in_specs,
        out_specs=(pl.BlockSpec((tm, QK_W), lambda i: (i, 0)),) * 3,
        compiler_params=_params(("arbitrary",)),
        name="attn_qkv_rope" if rope else "attn_qkv",
    )(*args)


def _softmax_parts(s):
    e = jnp.exp(s - jnp.max(s, axis=-1, keepdims=True))
    return e, 1.0 / jnp.sum(e, axis=-1, keepdims=True)


def _attn_kernel(lp_ref, sub_ref, q_ref, k_ref, v_ref, o_ref, *, lam_init):
    lp = lp_ref[...]
    lam = (jnp.exp(jnp.sum(lp[0:1, :] * lp[1:2, :], axis=1, keepdims=True))
           - jnp.exp(jnp.sum(lp[2:3, :] * lp[3:4, :], axis=1, keepdims=True)) + lam_init)
    for h in range(N_HEADS):
        c0 = h * V_DIM
        q = q_ref[:, c0:c0 + V_DIM]
        k = k_ref[:, c0:c0 + V_DIM].astype(BF16)
        e1, r1 = _softmax_parts(_dot_nt(q[:, :HEAD_DIM], k[:, :HEAD_DIM]))
        e2, r2 = _softmax_parts(_dot_nt(q[:, HEAD_DIM:], k[:, HEAD_DIM:]))
        w = e1 * r1 - lam * (e2 * r2)
        o = _dot(w.astype(BF16), v_ref[:, c0:c0 + V_DIM].astype(BF16))
        o_ref[:, c0:c0 + V_DIM] = (_rms(o, sub_ref[...]) * (1.0 - lam_init)).astype(BF16)


def _attention(lp, subln, q, k, v, lam_init, tq):
    bsz, t = q.shape[:2]
    tk = k.shape[1]
    return pl.pallas_call(
        functools.partial(_attn_kernel, lam_init=lam_init),
        out_shape=jax.ShapeDtypeStruct((bsz, t, N_HEADS * V_DIM), BF16),
        grid=(bsz, t // tq),
        in_specs=[
            pl.BlockSpec((4, HEAD_DIM), lambda b, i: (0, 0)),
            pl.BlockSpec((1, V_DIM), lambda b, i: (0, 0)),
            pl.BlockSpec((None, tq, QK_W), lambda b, i: (b, i, 0)),
            pl.BlockSpec((None, tk, QK_W), lambda b, i: (b, 0, 0)),
            pl.BlockSpec((None, tk, N_HEADS * V_DIM), lambda b, i: (b, 0, 0)),
        ],
        out_specs=pl.BlockSpec((None, tq, N_HEADS * V_DIM), lambda b, i: (b, i, 0)),
        compiler_params=_params(("arbitrary", "arbitrary")),
        name="diff_attention",
    )(lp, subln, q, k, v)


def _lru_gate_layout(gate_w, gate_b):
    nb = LRU_BLOCKS // 2
    w = gate_w.reshape(2, 2, 2, nb, LRU_BLOCK, LRU_BLOCK)
    bd = jnp.einsum("dgsbio,bc->sbidgco", w, jnp.eye(nb, dtype=w.dtype))
    bd = bd.reshape(2, LRU_HALF, 4 * LRU_HALF).astype(BF16)
    b = gate_b.reshape(2, 2, 2, LRU_HALF).transpose(2, 0, 1, 3).reshape(2, 1, 4 * LRU_HALF)
    return bd, b


def _rope_tables(n):
    rows = n // GRID_W
    row = jnp.repeat(jnp.arange(rows, dtype=F32), GRID_W)
    col = jnp.tile(jnp.arange(GRID_W, dtype=F32), rows)
    freqs = ROPE_THETA ** (-jnp.arange(ROPE_PAIRS, dtype=F32) / ROPE_PAIRS)
    ar = row[:, None] * freqs
    ac = col[:, None] * freqs
    zero = jnp.zeros_like(ar)
    cos = jnp.concatenate([jnp.cos(ar), jnp.cos(ar), jnp.cos(ac), jnp.cos(ac)], axis=-1)
    sin_up = jnp.concatenate([-jnp.sin(ar), zero, -jnp.sin(ac), zero], axis=-1)
    sin_dn = jnp.concatenate([zero, jnp.sin(ar), zero, jnp.sin(ac)], axis=-1)
    return tuple(jnp.tile(t, (1, 2)) for t in (cos, sin_up, sin_dn))


def kernel(x_prompt, x_sample, c, state_lru, cache_k, cache_v, c_ctx, mod_w, mod_b, norm_g,
           lru_w_in, lru_conv_w, lru_conv_b, lru_gate_w, lru_gate_b, lru_lambda, lru_w_out,
           attn_w_qkv, attn_lambda, attn_subln, attn_w_o,
           moe_router, moe_router_bias, moe_w_gate, moe_w_up, moe_w_down,
           shared_w_gate, shared_w_up, shared_w_down):
    bp, tp = x_prompt.shape[:2]
    bs, ts = x_sample.shape[:2]
    past = cache_k.shape[2]
    cond = jnp.concatenate([c_ctx[None], c, jnp.zeros((MOD_ROWS - 1 - bs, D_MODEL), F32)], axis=0)
    mod = _modulation(cond, mod_w, mod_b)
    streams = [dict(x=x_prompt.reshape(bp * tp, D_MODEL), rps=0, row=0, b=bp, t=tp),
               dict(x=x_sample.reshape(bs * ts, D_MODEL), rps=ts, row=1, b=bs, t=ts)]
    new_lru, new_k, new_v = [], [], []
    for i in range(DEPTH):
        j = i // N_MIXERS
        g = norm_g[i]
        router_t = moe_router[i].T
        router_b = moe_router_bias[i].reshape(N_EXPERTS, 1)
        if i % N_MIXERS == 0:
            w_in = lru_w_in[j].astype(BF16)
            w_mix = lru_w_out[j].astype(BF16)
            gate_w, gate_b = _lru_gate_layout(lru_gate_w[j], lru_gate_b[j])
        else:
            lam_init = 0.8 - 0.6 * math.exp(-0.3 * i)
            w_qkv = attn_w_qkv[j].astype(BF16)
            w_mix = attn_w_o[j].astype(BF16)
            tables = _rope_tables(ts)
        for si, s in enumerate(streams):
            if i % N_MIXERS == 0:
                gate, xr = _lru_in(s["x"], mod[i], g[0:1], w_in, s["rps"], s["row"])
                h0 = jnp.zeros((bp, 2, D_RNN), F32) if si == 0 else state_lru[:, j]
                m, h_last = _lru_core(xr, gate, h0, lru_conv_w[j], lru_conv_b[j][None], gate_w, gate_b,
                                      lru_lambda[j], s["b"], s["t"])
                if si == 0:
                    new_lru.append(h_last)
            else:
                if si == 0:
                    q, k, v = _qkv(s["x"], mod[i], g[0:1], w_qkv, s["rps"], s["row"], None, F32)
                    new_k.append(k.reshape(bp, tp, N_HEADS, 2 * HEAD_DIM))
                    new_v.append(v.reshape(bp, tp, N_HEADS, V_DIM))
                    k3 = k.reshape(bp, tp, QK_W)
                    v3 = v.reshape(bp, tp, N_HEADS * V_DIM)
                    tq = tp
                else:
                    q, k, v = _qkv(s["x"], mod[i], g[0:1], w_qkv, s["rps"], s["row"], tables, BF16)
                    k3 = jnp.concatenate([cache_k[:, j].reshape(bs, past, QK_W).astype(BF16),
                                          k.reshape(bs, ts, QK_W)], axis=1)
                    v3 = jnp.concatenate([cache_v[:, j].reshape(bs, past, N_HEADS * V_DIM).astype(BF16),
                                          v.reshape(bs, ts, N_HEADS * V_DIM)], axis=1)
                    tq = 512
                o = _attention(attn_lambda[j], attn_subln[j][None], q.reshape(s["b"], s["t"], QK_W),
                               k3, v3, lam_init, tq)
                m = o.reshape(s["b"] * s["t"], N_HEADS * V_DIM)
            s["x1"], s["xp"], s["comb"] = _mix_out(m, w_mix, s["x"], mod[i], g, router_t, router_b,
                                                   s["rps"], s["row"])
        xp = jnp.concatenate([s["xp"] for s in streams], axis=0)
        ek, rk, wk, cnt = _moe_plan(jnp.concatenate([s["comb"] for s in streams], axis=1))
        plan = _moe_layout(ek, rk, wk, cnt)
        ys = _moe_experts(plan, xp, moe_w_gate, moe_w_up, moe_w_down, i)
        tok0 = 0
        for s in streams:
            n = s["x1"].shape[0]
            pairs = slice(tok0 * TOP_K, (tok0 + n) * TOP_K)
            s["x"] = _moe_combine(plan, plan["ltile"][pairs], plan["wk"][pairs], tok0 // MOE_TSB, ys, s["xp"],
                                  shared_w_gate, shared_w_up,
                                  shared_w_down, i, s["x1"], mod[i], g, s["rps"], s["row"])
            tok0 += n
    return (streams[0]["x"].reshape(bp, tp, D_MODEL),
            streams[1]["x"].reshape(bs, ts, D_MODEL),
            jnp.stack(new_lru, axis=1),
            jnp.stack(new_k, axis=1),
            jnp.stack(new_v, axis=1))
```
